```python
import jax
import jax.numpy as jnp
from jax import lax
import numpy as np

D_MODEL = 2048
BATCH = 16
SEQ = 256
DEPTH = 4
DEC_BATCH = 4
DEC_SEQ = 1024
PAST_LEN = 256

GRID_W = 64
ROPE_THETA = 10000.0
EPS = 1e-6
Q_BLOCK = 128
D_FF = 5632
N_BRANCH = 3
BR_W = 1024
H_A = 8
DK_A = 128
DV_A = 128
HK_A = H_A * DK_A
HGRN_CHUNK = 64
H_B = 8
Q_LORA = 512
KV_LORA = 256
NOPE_B = 128
ROPE_B = 64
V_B = 128
H_C = 8
KVH_C = 4
HD_C = 128

IN_SIZES = (HK_A, H_A * DV_A, HK_A, HK_A, H_A * DV_A,
            Q_LORA, KV_LORA, ROPE_B,
            H_C * HD_C, KVH_C * HD_C, KVH_C * HD_C,
            N_BRANCH * D_MODEL)
N_IN = 3 * HK_A + 2 * H_A * DV_A + Q_LORA + KV_LORA + ROPE_B + H_C * HD_C + 2 * KVH_C * HD_C + N_BRANCH * D_MODEL

kernel_name = 'hybrid_diffusion_prefix_hgrn2_mla_gqa_step'


def rms_norm(x, w):
    xf = x.astype(jnp.float32)
    y = xf * lax.rsqrt(jnp.mean(xf * xf, axis=-1, keepdims=True) + EPS)
    return (y * w.astype(jnp.float32)).astype(x.dtype)


def split_cols(z, sizes):
    parts, start = [], 0
    for s in sizes:
        parts.append(z[..., start:start + s])
        start += s
    return parts


def axial_rope(x):
    T, d = x.shape[1], x.shape[-1]
    half, quarter = d // 2, d // 4
    rows = T // GRID_W
    row_idx = jnp.repeat(jnp.arange(rows), GRID_W)
    col_idx = jnp.tile(jnp.arange(GRID_W), rows)
    inv = ROPE_THETA ** (-jnp.arange(quarter, dtype=jnp.float32) / quarter)

    def rot(xh, pos):
        ang = pos.astype(jnp.float32)[:, None] * inv[None, :]
        cos = jnp.cos(ang)[None, :, None, :]
        sin = jnp.sin(ang)[None, :, None, :]
        xh = xh.astype(jnp.float32)
        x1, x2 = xh[..., :quarter], xh[..., quarter:]
        return jnp.concatenate([x1 * cos - x2 * sin, x2 * cos + x1 * sin], axis=-1)

    out = jnp.concatenate([rot(x[..., :half], row_idx), rot(x[..., half:], col_idx)], axis=-1)
    return out.astype(x.dtype)


def block_attention(q, k, v, scale):
    B, T, H, dq = q.shape
    kvh, dv = k.shape[2], v.shape[-1]
    r = H // kvh
    nb = T // Q_BLOCK
    qb = jnp.moveaxis(q.reshape(B, nb, Q_BLOCK, kvh, r, dq), 1, 0)

    def one_block(qblk):
        s = jnp.einsum('bqgrd,bkgd->bgrqk', qblk, k).astype(jnp.float32) * scale
        p = jax.nn.softmax(s, axis=-1).astype(v.dtype)
        return jnp.einsum('bgrqk,bkgd->bqgrd', p, v)

    o = lax.map(one_block, qb)
    return jnp.moveaxis(o, 0, 1).reshape(B, T, H, dv)


def layer_lower_bounds(lb_param):
    p = jax.nn.softmax(lb_param.astype(jnp.float32), axis=0)
    cum = jnp.cumsum(p, axis=0)
    return cum - cum[0:1]


def log_forget(pre, lb):
    pre = pre.astype(jnp.float32)
    return jnp.logaddexp(jnp.log(lb), jnp.log1p(-lb) + jax.nn.log_sigmoid(pre))


def hgrn_chunk_scan(q, k, v, logf, s0):
    B, T, H, DK = q.shape
    DV = v.shape[-1]
    n = T // HGRN_CHUNK
    causal = jnp.tril(jnp.ones((HGRN_CHUNK, HGRN_CHUNK), dtype=bool))[None, :, :, None, None]

    def to_chunks(a):
        return jnp.moveaxis(a.astype(jnp.float32).reshape(B, n, HGRN_CHUNK, H, a.shape[-1]), 1, 0)

    def step(S, inp):
        qc, kc, vc, gc = inp
        b = jnp.cumsum(gc, axis=1)
        o_inter = jnp.einsum('bchk,bhkv->bchv', qc * jnp.exp(b), S)
        diff = b[:, :, None] - b[:, None, :]
        decay = jnp.exp(jnp.where(causal, diff, -jnp.inf))
        attn = jnp.einsum('bthk,btshk,bshk->bhts', qc, decay, kc)
        o_intra = jnp.einsum('bhts,bshv->bthv', attn, vc)
        b_last = b[:, -1]
        S_new = jnp.exp(b_last)[..., None] * S + jnp.einsum(
            'bshk,bshv->bhkv', kc * jnp.exp(b_last[:, None] - b), vc)
        return S_new, o_inter + o_intra

    s_fin, o = lax.scan(step, s0.astype(jnp.float32),
                        (to_chunks(q), to_chunks(k), to_chunks(v), to_chunks(logf)))
    o = jnp.moveaxis(o, 0, 1).reshape(B, T, H, DV)
    return o.astype(v.dtype), s_fin.astype(v.dtype)


def adaln_params(cond, w, b):
    m = jax.nn.silu(cond) @ w + b
    return m.reshape(cond.shape[0], 9, D_MODEL)[:, :, None, :]


def half_step_ffn(x, shift, scale, gate, norm_w, w13, w2):
    h = rms_norm(x, norm_w) * (1 + scale) + shift
    a, u = jnp.split(h @ w13, 2, axis=-1)
    return x + 0.5 * gate * ((jax.nn.silu(a) * u) @ w2)


def token_mixing(h, lp, lb, ctx):
    B, T, _ = h.shape
    latent = ctx is not None
    z = h @ lp['w_in']
    (a_q, a_v, a_ff, a_fb, a_g, b_q, b_kv, b_pe, c_q, c_k, c_v, g_br) = split_cols(z, IN_SIZES)

    q_a = jax.nn.silu(a_q).reshape(B, T, H_A, DK_A)
    v_a = a_v.reshape(B, T, H_A, DV_A)
    logf_f = log_forget(a_ff, lb[0]).reshape(B, T, H_A, DK_A)
    logf_b = log_forget(a_fb, lb[1]).reshape(B, T, H_A, DK_A)
    k_f = -jnp.expm1(logf_f)
    k_b = -jnp.expm1(logf_b)
    if latent:
        s0_f, s0_b = ctx[0][:, 0], ctx[0][:, 1]
    else:
        s0_f = jnp.zeros((B, H_A, DK_A, DV_A), jnp.float32)
        s0_b = s0_f
    o_fwd, s_fwd = hgrn_chunk_scan(q_a, k_f, v_a, logf_f, s0_f)
    flip = lambda a: jnp.flip(a, axis=1)
    o_bwd, s_bwd = hgrn_chunk_scan(flip(q_a), flip(k_b), flip(v_a), flip(logf_b), s0_b)
    o_rec = rms_norm(o_fwd + flip(o_bwd), lp['hgrn_onorm']) * jax.nn.silu(a_g.reshape(B, T, H_A, DV_A))
    out_a = o_rec.reshape(B, T, H_A * DV_A)

    cq = (rms_norm(b_q, lp['mla_qa_norm']) @ lp['mla_wuq']).reshape(B, T, H_B, NOPE_B + ROPE_B)
    q_nope = rms_norm(cq[..., :NOPE_B], lp['mla_nope_norm'][0])
    q_pe = rms_norm(cq[..., NOPE_B:], lp['mla_rope_norm'][0])
    ckv = rms_norm(b_kv, lp['mla_kva_norm'])
    kpe = rms_norm(b_pe, lp['mla_rope_norm'][1])
    if latent:
        q_pe = axial_rope(q_pe)
        kpe_lat = axial_rope(kpe[:, :, None, :])[:, :, 0]
        ckv_all = jnp.concatenate([ctx[1], ckv], axis=1)
        kpe_all = jnp.concatenate([ctx[2], kpe_lat], axis=1)
    else:
        ckv_all, kpe_all = ckv, kpe
    S = ckv_all.shape[1]
    kv = (ckv_all @ lp['mla_wukv']).reshape(B, S, H_B, NOPE_B + V_B)
    k_nope = rms_norm(kv[..., :NOPE_B], lp['mla_nope_norm'][1])
    v_mla = kv[..., NOPE_B:]
    q_mla = jnp.concatenate([q_nope, q_pe], axis=-1)
    k_mla = jnp.concatenate([k_nope, jnp.broadcast_to(kpe_all[:, :, None, :], (B, S, H_B, ROPE_B))], axis=-1)
    out_b = block_attention(q_mla, k_mla, v_mla, (NOPE_B + ROPE_B) ** -0.5).reshape(B, T, H_B * V_B)

    q_g = rms_norm(c_q.reshape(B, T, H_C, HD_C), lp['gqa_qk_norm'][0])
    k_g = rms_norm(c_k.reshape(B, T, KVH_C, HD_C), lp['gqa_qk_norm'][1])
    v_g = c_v.reshape(B, T, KVH_C, HD_C)
    if latent:
        q_g_pos = axial_rope(q_g)
        k_all = jnp.concatenate([ctx[3], axial_rope(k_g)], axis=1)
        v_all = jnp.concatenate([ctx[4], v_g], axis=1)
    else:
        q_g_pos, k_all, v_all = q_g, k_g, v_g
    out_c = block_attention(q_g_pos, k_all, v_all, HD_C ** -0.5).reshape(B, T, H_C * HD_C)

    br = jnp.einsum('btnc,ncd->btnd', jnp.stack([out_a, out_b, out_c], axis=2), lp['w_branch'])
    gates = jax.nn.sigmoid(g_br.reshape(B, T, N_BRANCH, D_MODEL))
    out = jnp.sum(gates * br, axis=2) @ lp['w_out']
    ctx_tensors = None if latent else (jnp.stack([s_fwd, s_bwd], axis=1), ckv, kpe, k_g, v_g)
    return out, ctx_tensors


def trunk_layer(x, m, lp, lb, ctx):
    x = half_step_ffn(x, m[:, 0], m[:, 1], m[:, 2], lp['norm_w'][0], lp['ffn_w13'][0], lp['ffn_w2'][0])
    h = rms_norm(x, lp['norm_w'][1]) * (1 + m[:, 4]) + m[:, 3]
    mixed, ctx_tensors = token_mixing(h, lp, lb, ctx)
    x = x + m[:, 5] * mixed
    x = half_step_ffn(x, m[:, 6], m[:, 7], m[:, 8], lp['norm_w'][2], lp['ffn_w13'][1], lp['ffn_w2'][1])
    return x, ctx_tensors


def setup_inputs(seed: int = 0) -> dict:
    key = jax.random.key(seed)
    ks = jax.random.split(key, 26)
    f32 = jnp.float32

    def nrm(k, shape, scale=1.0):
        return scale * jax.random.normal(k, shape, f32)

    def gain(k, shape):
        return 1.0 + 0.02 * jax.random.normal(k, shape, f32)

    return {
        'x_prompt': nrm(ks[0], (BATCH, SEQ, D_MODEL)),
        'x_sample': nrm(ks[1], (DEC_BATCH, DEC_SEQ, D_MODEL)),
        'state_hgrn': nrm(ks[2], (DEC_BATCH, DEPTH, 2, H_A, DK_A, DV_A), 0.5),
        'cache_mla_ckv': nrm(ks[3], (DEC_BATCH, DEPTH, PAST_LEN, KV_LORA)),
        'cache_mla_kpe': nrm(ks[4], (DEC_BATCH, DEPTH, PAST_LEN, ROPE_B)),
        'cache_gqa_k': nrm(ks[5], (DEC_BATCH, DEPTH, PAST_LEN, KVH_C, HD_C)),
        'cache_gqa_v': nrm(ks[6], (DEC_BATCH, DEPTH, PAST_LEN, KVH_C, HD_C)),
        'c': nrm(ks[7], (DEC_BATCH, D_MODEL)),
        'c_ctx': nrm(ks[8], (D_MODEL,)),
        'w_mod': nrm(ks[9], (DEPTH, D_MODEL, 9 * D_MODEL), 0.5 * D_MODEL ** -0.5),
        'b_mod': nrm(ks[10], (DEPTH, 9 * D_MODEL), 0.02),
        'norm_w': gain(ks[11], (DEPTH, 3, D_MODEL)),
        'ffn_w13': nrm(ks[12], (DEPTH, 2, D_MODEL, 2 * D_FF), D_MODEL ** -0.5),
        'ffn_w2': nrm(ks[13], (DEPTH, 2, D_FF, D_MODEL), D_FF ** -0.5),
        'w_in': nrm(ks[14], (DEPTH, D_MODEL, N_IN), D_MODEL ** -0.5),
        'hgrn_lb': nrm(ks[15], (DEPTH, 2, HK_A)),
        'hgrn_onorm': gain(ks[16], (DEPTH, DV_A)),
        'mla_qa_norm': gain(ks[17], (DEPTH, Q_LORA)),
        'mla_kva_norm': gain(ks[18], (DEPTH, KV_LORA)),
        'mla_wuq': nrm(ks[19], (DEPTH, Q_LORA, H_B * (NOPE_B + ROPE_B)), Q_LORA ** -0.5),
        'mla_wukv': nrm(ks[20], (DEPTH, KV_LORA, H_B * (NOPE_B + V_B)), KV_LORA ** -0.5),
        'mla_nope_norm': gain(ks[21], (DEPTH, 2, NOPE_B)),
        'mla_rope_norm': gain(ks[22], (DEPTH, 2, ROPE_B)),
        'gqa_qk_norm': gain(ks[23], (DEPTH, 2, HD_C)),
        'w_branch': nrm(ks[24], (DEPTH, N_BRANCH, BR_W, D_MODEL), BR_W ** -0.5),
        'w_out': nrm(ks[25], (DEPTH, D_MODEL, D_MODEL), D_MODEL ** -0.5),
    }


def reference(x_prompt, x_sample, state_hgrn, cache_mla_ckv, cache_mla_kpe, cache_gqa_k, cache_gqa_v,
              c, c_ctx, w_mod, b_mod, norm_w, ffn_w13, ffn_w2, w_in, hgrn_lb, hgrn_onorm,
              mla_qa_norm, mla_kva_norm, mla_wuq, mla_wukv, mla_nope_norm, mla_rope_norm,
              gqa_qk_norm, w_branch, w_out):
    lb_all = layer_lower_bounds(hgrn_lb)
    y_p, y_s = x_prompt, x_sample
    st_h, st_ckv, st_kpe, st_k, st_v = [], [], [], [], []
    for l in range(DEPTH):
        lp = {
            'norm_w': norm_w[l], 'ffn_w13': ffn_w13[l], 'ffn_w2': ffn_w2[l], 'w_in': w_in[l],
            'hgrn_onorm': hgrn_onorm[l], 'mla_qa_norm': mla_qa_norm[l], 'mla_kva_norm': mla_kva_norm[l],
            'mla_wuq': mla_wuq[l], 'mla_wukv': mla_wukv[l], 'mla_nope_norm': mla_nope_norm[l],
            'mla_rope_norm': mla_rope_norm[l], 'gqa_qk_norm': gqa_qk_norm[l],
            'w_branch': w_branch[l], 'w_out': w_out[l],
        }
        m_ctx = adaln_params(c_ctx[None, :], w_mod[l], b_mod[l])
        m_lat = adaln_params(c, w_mod[l], b_mod[l])
        y_p, (s_h, s_ckv, s_kpe, s_k, s_v) = trunk_layer(y_p, m_ctx, lp, lb_all[l], None)
        st_h.append(s_h)
        st_ckv.append(s_ckv)
        st_kpe.append(s_kpe)
        st_k.append(s_k)
        st_v.append(s_v)
        cached = (state_hgrn[:, l], cache_mla_ckv[:, l], cache_mla_kpe[:, l], cache_gqa_k[:, l], cache_gqa_v[:, l])
        y_s, _ = trunk_layer(y_s, m_lat, lp, lb_all[l], cached)
    new_state_hgrn = jnp.stack(st_h, axis=1)
    new_mla_ckv = jnp.stack(st_ckv, axis=1)
    new_mla_kpe = jnp.stack(st_kpe, axis=1)
    new_gqa_k = jnp.stack(st_k, axis=1)
    new_gqa_v = jnp.stack(st_v, axis=1)
    return (y_p, y_s, new_state_hgrn, new_mla_ckv, new_mla_kpe, new_gqa_k, new_gqa_v)
```

```python
import functools

import numpy as np
import jax
import jax.numpy as jnp
from jax import lax
from jax.experimental import pallas as pl
from jax.experimental.pallas import tpu as pltpu

f32 = jnp.float32
bf16 = jnp.bfloat16

D_MODEL = 2048
BATCH = 16
SEQ = 256
DEPTH = 4
DEC_BATCH = 4
DEC_SEQ = 1024
PAST_LEN = 256
GRID_W = 64
ROPE_THETA = 10000.0
EPS = 1e-6
D_FF = 5632
N_BRANCH = 3
BR_W = 1024
H_A = 8
DK_A = 128
DV_A = 128
H_B = 8
Q_LORA = 512
KV_LORA = 256
NOPE_B = 128
ROPE_B = 64
V_B = 128
H_C = 8
KVH_C = 4
HD_C = 128

LANE = 128
MC = BATCH * SEQ
ML = DEC_BATCH * DEC_SEQ
M_ROWS = MC + ML
KV_SEQ = PAST_LEN + DEC_SEQ
KV_LAT_ROWS = DEC_BATCH * KV_SEQ
KV_ROWS = KV_LAT_ROWS + MC
N_MOD_ROWS = 16

COL_AQ, COL_AV, COL_AFF, COL_AFB, COL_AG = 0, 1024, 2048, 3072, 4096
COL_CQ = 5120
COL_GBR = 6144
COL_BQ = 12288
COL_CK = 12800
COL_CV = 13312
COL_BKV = 13824
COL_BPE = 14080
N_IN_PAD = 14336

ROW_TILE = 256
HGRN_C = 64
HGRN_SUB = 16
VMEM_LIMIT = 56 * 1024 * 1024


def _cparams(sem):
    return pltpu.CompilerParams(dimension_semantics=sem, vmem_limit_bytes=VMEM_LIMIT)


def _sigmoid(x):
    return 1.0 / (1.0 + jnp.exp(-x))


def _rms(x, w, n=None):
    n = x.shape[-1] if n is None else n
    ms = jnp.sum(x * x, axis=-1, keepdims=True) * (1.0 / n)
    return x * lax.rsqrt(ms + EPS) * w


def _mod_row(i, tm):
    n_ctx = MC // tm
    per = DEC_SEQ // tm
    return jnp.where(i < n_ctx, 0, 1 + jnp.maximum(i - n_ctx, 0) // per)


def _mm_kernel(*refs, pro, epi, n_w, shift_id, scale_id, gate_id, gate_scale):
    it = iter(refs)
    x_ref = next(it)
    nw_ref = next(it) if pro in ("norm", "normmod") else None
    modk_ref = next(it) if pro == "normmod" else None
    w_refs = [next(it) for _ in range(n_w)]
    res_ref = next(it) if epi == "residual" else None
    modn_ref = next(it) if epi == "residual" else None
    bias_ref = next(it) if epi == "bias" else None
    o_ref = next(it)
    h_ref = next(it) if pro is not None else None

    if pro is not None:
        @pl.when(pl.program_id(1) == 0)
        def _():
            x = x_ref[...].astype(f32)
            if pro == "silu":
                y = x * _sigmoid(x)
            else:
                y = _rms(x, nw_ref[...])
                if pro == "normmod":
                    y = y * (1.0 + modk_ref[0, scale_id:scale_id + 1, :]) + modk_ref[0, shift_id:shift_id + 1, :]
            h_ref[...] = y.astype(bf16)
        lhs = h_ref[...]
    else:
        lhs = x_ref[...]

    accs = [jnp.dot(lhs, w[...].astype(bf16), preferred_element_type=f32) for w in w_refs]
    if epi == "swiglu":
        a, u = accs
        out = a * _sigmoid(a) * u
    elif epi == "residual":
        out = res_ref[...] + (gate_scale * modn_ref[0, gate_id:gate_id + 1, :]) * accs[0]
    elif epi == "bias":
        out = accs[0] + bias_ref[...]
    else:
        out = accs[0]
    o_ref[...] = out.astype(o_ref.dtype)


def _mm(x, ws, *, tm, tn, out_dtype, name, pro=None, epi="store", norm_w=None, mod=None,
        shift_id=0, scale_id=0, gate_id=0, gate_scale=1.0, res=None, bias=None):
    m, k = x.shape
    n = ws[0].shape[1]
    grid = (m // tm, n // tn)
    in_specs = [pl.BlockSpec((tm, k), lambda i, j: (i, 0))]
    args = [x]
    if pro in ("norm", "normmod"):
        in_specs.append(pl.BlockSpec((1, k), lambda i, j: (0, 0)))
        args.append(norm_w)
    if pro == "normmod":
        in_specs.append(pl.BlockSpec((1, 9, k), lambda i, j: (_mod_row(i, tm), 0, 0)))
        args.append(mod)
    for w in ws:
        in_specs.append(pl.BlockSpec((k, tn), lambda i, j: (0, j)))
        args.append(w)
    if epi == "residual":
        in_specs.append(pl.BlockSpec((tm, tn), lambda i, j: (i, j)))
        args.append(res)
        in_specs.append(pl.BlockSpec((1, 9, tn), lambda i, j: (_mod_row(i, tm), 0, j)))
        args.append(mod)
    if epi == "bias":
        in_specs.append(pl.BlockSpec((1, tn), lambda i, j: (0, j)))
        args.append(bias)
    scratch = [pltpu.VMEM((tm, k), bf16)] if pro is not None else []
    kern = functools.partial(_mm_kernel, pro=pro, epi=epi, n_w=len(ws), shift_id=shift_id,
                             scale_id=scale_id, gate_id=gate_id, gate_scale=gate_scale)
    return pl.pallas_call(
        kern,
        grid=grid,
        in_specs=in_specs,
        out_specs=pl.BlockSpec((tm, tn), lambda i, j: (i, j)),
        out_shape=jax.ShapeDtypeStruct((m, n), out_dtype),
        scratch_shapes=scratch,
        compiler_params=_cparams(("parallel", "arbitrary")),
        name=name,
    )(*args)


def _merge_kernel(oa_ref, ob_ref, oc_ref, wb_ref, ga_ref, gb_ref, gc_ref, o_ref):
    acc = None
    for n, (o, g) in enumerate(((oa_ref, ga_ref), (ob_ref, gb_ref), (oc_ref, gc_ref))):
        br = jnp.dot(o[...], wb_ref[n], preferred_element_type=f32)
        term = _sigmoid(g[...]) * br
        acc = term if acc is None else acc + term
    o_ref[...] = acc.astype(o_ref.dtype)


def _merge(out_a, out_b, out_c, wb, z, *, tm=1024, tn=512):
    grid = (M_ROWS // tm, D_MODEL // tn)
    o_spec = pl.BlockSpec((tm, BR_W), lambda i, j: (i, 0))

    def g_spec(n):
        base = (COL_GBR + n * D_MODEL) // tn
        return pl.BlockSpec((tm, tn), lambda i, j: (i, base + j))

    return pl.pallas_call(
        _merge_kernel,
        grid=grid,
        in_specs=[o_spec, o_spec, o_spec,
                  pl.BlockSpec((N_BRANCH, BR_W, tn), lambda i, j: (0, 0, j)),
                  g_spec(0), g_spec(1), g_spec(2)],
        out_specs=pl.BlockSpec((tm, tn), lambda i, j: (i, j)),
        out_shape=jax.ShapeDtypeStruct((M_ROWS, D_MODEL), bf16),
        compiler_params=_cparams(("parallel", "arbitrary")),
        name="merge",
    )(out_a, out_b, out_c, wb, z, z, z)


def _rope(x, cos, sin, quarter):
    lane = lax.broadcasted_iota(jnp.int32, x.shape, 1)
    first = (lane % (2 * quarter)) < quarter
    partner = jnp.where(first, -pltpu.roll(x, LANE - quarter, axis=1), pltpu.roll(x, quarter, axis=1))
    return x * cos + partner * sin


def _rope_tables(width):
    half, quarter = width // 2, width // 4
    t = np.arange(DEC_SEQ)
    inv = ROPE_THETA ** (-np.arange(quarter, dtype=np.float32) / quarter)
    ang_r = (t // GRID_W).astype(np.float32)[:, None] * inv[None, :]
    ang_c = (t % GRID_W).astype(np.float32)[:, None] * inv[None, :]
    ang = np.concatenate([ang_r, ang_r, ang_c, ang_c], axis=1).astype(np.float32)
    cos = np.ones((KV_SEQ, LANE), np.float32)
    sin = np.zeros((KV_SEQ, LANE), np.float32)
    cos[PAST_LEN:, :width] = np.cos(ang)
    sin[PAST_LEN:, :width] = np.sin(ang)
    return jnp.asarray(cos), jnp.asarray(sin)


def _q_table_block(i):
    n_ctx = MC // ROW_TILE
    per = DEC_SEQ // ROW_TILE
    return jnp.where(i < n_ctx, 0, 1 + jnp.maximum(i - n_ctx, 0) % per)


def _prep_in_kernel(bq_ref, bkv_ref, bpe_ref, cq_ref, ck_ref,
                    qa_ref, kva_ref, rpe_ref, qkn_ref,
                    cg_ref, sg_ref, cm_ref, sm_ref,
                    bqn_ref, ckv_ref, kpe_ref, qg_ref, kg_ref):
    bqn_ref[...] = _rms(bq_ref[...], qa_ref[...]).astype(bf16)
    ckv_ref[...] = _rms(bkv_ref[...], kva_ref[...])
    kpe = _rms(bpe_ref[...], rpe_ref[...], n=ROPE_B)
    kpe_ref[...] = _rope(kpe, cm_ref[...], sm_ref[...], ROPE_B // 4)
    cg, sg = cg_ref[...], sg_ref[...]
    for h in range(H_C):
        sl = slice(h * HD_C, (h + 1) * HD_C)
        q = _rms(cq_ref[:, sl], qkn_ref[0:1, :])
        qg_ref[:, sl] = _rope(q, cg, sg, HD_C // 4).astype(bf16)
    for g in range(KVH_C):
        sl = slice(g * HD_C, (g + 1) * HD_C)
        k = _rms(ck_ref[:, sl], qkn_ref[1:2, :])
        kg_ref[:, sl] = _rope(k, cg, sg, HD_C // 4)


def _prep_in(z, qa_norm, kva_norm, rope_norm_k, qk_norm, tabs_g, tabs_m):
    tr = ROW_TILE
    grid = (M_ROWS // tr,)

    def zspec(col, width):
        return pl.BlockSpec((tr, width), lambda i: (i, col // width))

    def wspec(shape):
        return pl.BlockSpec(shape, lambda i: (0, 0))

    tspec = pl.BlockSpec((tr, LANE), lambda i: (_q_table_block(i), 0))

    def ospec(width):
        return pl.BlockSpec((tr, width), lambda i: (i, 0))

    return pl.pallas_call(
        _prep_in_kernel,
        grid=grid,
        in_specs=[zspec(COL_BQ, Q_LORA), zspec(COL_BKV, KV_LORA), zspec(COL_BPE, LANE),
                  zspec(COL_CQ, H_C * HD_C), zspec(COL_CK, KVH_C * HD_C),
                  wspec((1, Q_LORA)), wspec((1, KV_LORA)), wspec((1, LANE)), wspec((2, HD_C)),
                  tspec, tspec, tspec, tspec],
        out_specs=[ospec(Q_LORA), ospec(KV_LORA), ospec(LANE), ospec(H_C * HD_C), ospec(KVH_C * HD_C)],
        out_shape=[jax.ShapeDtypeStruct((M_ROWS, Q_LORA), bf16),
                   jax.ShapeDtypeStruct((M_ROWS, KV_LORA), f32),
                   jax.ShapeDtypeStruct((M_ROWS, LANE), f32),
                   jax.ShapeDtypeStruct((M_ROWS, H_C * HD_C), bf16),
                   jax.ShapeDtypeStruct((M_ROWS, KVH_C * HD_C), f32)],
        compiler_params=_cparams(("parallel",)),
        name="prep_in",
    )(z, z, z, z, z, qa_norm, kva_norm, rope_norm_k, qk_norm, *tabs_g, *tabs_m)


def _prep_mla_q_kernel(cq_ref, nn_ref, rn_ref, cm_ref, sm_ref, o_ref):
    cm, sm = cm_ref[...], sm_ref[...]
    for h in range(H_B):
        nope = _rms(cq_ref[:, h * LANE:(h + 1) * LANE], nn_ref[...])
        pe = _rms(cq_ref[:, (H_B + h) * LANE:(H_B + h + 1) * LANE], rn_ref[...], n=ROPE_B)
        o_ref[:, 2 * h * LANE:(2 * h + 1) * LANE] = nope.astype(bf16)
        o_ref[:, (2 * h + 1) * LANE:(2 * h + 2) * LANE] = _rope(pe, cm, sm, ROPE_B // 4).astype(bf16)


def _prep_mla_q(cq, nope_norm_q, rope_norm_q, tabs_m):
    tr = ROW_TILE
    width = 2 * H_B * LANE
    tspec = pl.BlockSpec((tr, LANE), lambda i: (_q_table_block(i), 0))
    return pl.pallas_call(
        _prep_mla_q_kernel,
        grid=(M_ROWS // tr,),
        in_specs=[pl.BlockSpec((tr, width), lambda i: (i, 0)),
                  pl.BlockSpec((1, LANE), lambda i: (0, 0)), pl.BlockSpec((1, LANE), lambda i: (0, 0)),
                  tspec, tspec],
        out_specs=pl.BlockSpec((tr, width), lambda i: (i, 0)),
        out_shape=jax.ShapeDtypeStruct((M_ROWS, width), bf16),
        compiler_params=_cparams(("parallel",)),
        name="prep_mla_q",
    )(cq, nope_norm_q, rope_norm_q, *tabs_m)


def _prep_mla_k_kernel(kv_ref, kpe_ref, nn_ref, o_ref):
    kpe = kpe_ref[...].astype(bf16)
    for h in range(H_B):
        nope = _rms(kv_ref[:, 2 * h * LANE:(2 * h + 1) * LANE], nn_ref[...])
        o_ref[:, 2 * h * LANE:(2 * h + 1) * LANE] = nope.astype(bf16)
        o_ref[:, (2 * h + 1) * LANE:(2 * h + 2) * LANE] = kpe


def _prep_mla_k(kv, kpe_all, nope_norm_k):
    tr = ROW_TILE
    width = 2 * H_B * LANE
    return pl.pallas_call(
        _prep_mla_k_kernel,
        grid=(KV_ROWS // tr,),
        in_specs=[pl.BlockSpec((tr, width), lambda i: (i, 0)),
                  pl.BlockSpec((tr, LANE), lambda i: (i, 0)),
                  pl.BlockSpec((1, LANE), lambda i: (0, 0))],
        out_specs=pl.BlockSpec((tr, width), lambda i: (i, 0)),
        out_shape=jax.ShapeDtypeStruct((KV_ROWS, width), bf16),
        compiler_params=_cparams(("parallel",)),
        name="prep_mla_k",
    )(kv, kpe_all, nope_norm_k)


def _attn_kernel(q_ref, k_ref, v_ref, o_ref, *, scale):
    q = q_ref[...].astype(bf16)
    k = k_ref[...].astype(bf16)
    s = lax.dot_general(q, k, (((1,), (1,)), ((), ())), preferred_element_type=f32) * scale
    m = jnp.max(s, axis=-1, keepdims=True)
    p = jnp.exp(s - m)
    l = jnp.sum(p, axis=-1, keepdims=True)
    o = jnp.dot(p.astype(bf16), v_ref[...].astype(bf16), preferred_element_type=f32)
    o_ref[...] = (o / l).astype(o_ref.dtype)


def _attention(q, k, v, *, n_samples, t, s, tq, dq, dv, n_heads, q_row0, kv_row0, k_col, v_col,
               out_rows, scale, name):
    nq = t // tq
    q0 = q_row0 // tq
    kv0 = kv_row0 // s
    return pl.pallas_call(
        functools.partial(_attn_kernel, scale=scale),
        grid=(n_samples, n_heads, nq),
        in_specs=[pl.BlockSpec((tq, dq), lambda b, h, i: (q0 + b * nq + i, h)),
                  pl.BlockSpec((s, dq), lambda b, h, i: (kv0 + b, k_col(h))),
                  pl.BlockSpec((s, dv), lambda b, h, i: (kv0 + b, v_col(h)))],
        out_specs=pl.BlockSpec((tq, dv), lambda b, h, i: (b * nq + i, h)),
        out_shape=jax.ShapeDtypeStruct((out_rows, n_heads * dv), bf16),
        compiler_params=_cparams(("parallel", "parallel", "arbitrary")),
        name=name,
    )(q, k, v)


def _hgrn_kernel(*refs, layer, t_len, latent):
    if latent:
        lb_ref, on_ref, q_ref, v_ref, ff_ref, fb_ref, g_ref, s_ref, out_ref, oacc_ref = refs
        sout_ref = None
    else:
        lb_ref, on_ref, q_ref, v_ref, ff_ref, fb_ref, g_ref, out_ref, sout_ref, oacc_ref = refs
        s_ref = None
    c = HGRN_C
    n_chunks = t_len // c
    neg = -1e30

    row = lax.broadcasted_iota(jnp.int32, (c, LANE), 0)
    ti = lax.broadcasted_iota(jnp.int32, (c, c), 0)
    si = lax.broadcasted_iota(jnp.int32, (c, c), 1)
    ones_b = jnp.ones((LANE, LANE), bf16)

    if layer > 0:
        ps = [lb_ref[i] for i in range(DEPTH)]
        pmax = functools.reduce(jnp.maximum, ps)
        es = [jnp.exp(p - pmax) for p in ps]
        lb = functools.reduce(lambda a, b: a + b, es[1:layer + 1]) / functools.reduce(lambda a, b: a + b, es)
        log_lb = jnp.log(lb)
        log_1m = jnp.log1p(-lb)

    def log_forget(pre, d):
        ls = jnp.minimum(pre, 0.0) - jnp.log1p(jnp.exp(-jnp.abs(pre)))
        if layer == 0:
            return ls
        a = log_lb[d:d + 1, :]
        cc = log_1m[d:d + 1, :] + ls
        return jnp.maximum(a, cc) + jnp.log1p(jnp.exp(-jnp.abs(a - cc)))

    def run_dir(d, pre_ref):
        rev = d == 1
        tri = jnp.where((si >= ti) if rev else (si <= ti), 1.0, 0.0).astype(bf16)
        pos = (HGRN_SUB - 1 - row % HGRN_SUB) if rev else (row % HGRN_SUB)
        levels = []
        w = HGRN_SUB
        while w < c:
            later = ((row % (2 * w)) < w) if rev else ((row % (2 * w)) >= w)
            same_pair = (ti // (2 * w)) == (si // (2 * w))
            ref_rows = [(2 * p * w + w) if rev else (2 * p * w + w - 1) for p in range(c // (2 * w))]
            levels.append((w, later, same_pair, ref_rows))
            w *= 2

        def body(ci, st):
            cidx = (n_chunks - 1 - ci) if rev else ci
            sl = pl.ds(pl.multiple_of(cidx * c, c), c)
            qx = q_ref[sl, :]
            q = qx * _sigmoid(qx)
            v = v_ref[sl, :]
            logf = log_forget(pre_ref[sl, :], d)
            k = 1.0 - jnp.exp(logf)
            hi = logf.astype(bf16)
            r1 = logf - hi.astype(f32)
            mid = r1.astype(bf16)
            lo = (r1 - mid.astype(f32)).astype(bf16)
            b = (jnp.dot(tri, hi, preferred_element_type=f32)
                 + jnp.dot(tri, mid, preferred_element_type=f32)
                 + jnp.dot(tri, lo, preferred_element_type=f32))
            blast = b[0:1, :] if rev else b[c - 1:c, :]
            v_b = v.astype(bf16)

            qd = (q * jnp.exp(b)).astype(bf16)
            o = lax.dot_general(qd, st.astype(bf16), (((1,), (1,)), ((), ())), preferred_element_type=f32)

            attn = jnp.zeros((c, c), f32)
            for (w, later, same_pair, ref_rows) in levels:
                refb = jnp.concatenate(
                    [jnp.broadcast_to(b[r:r + 1, :], (2 * w, LANE)) for r in ref_rows], axis=0)
                qh = (q * jnp.exp(jnp.where(later, b - refb, neg))).astype(bf16)
                kh = (k * jnp.exp(jnp.where(later, neg, refb - b))).astype(bf16)
                pm = lax.dot_general(qh, kh, (((1,), (1,)), ((), ())), preferred_element_type=f32)
                attn = attn + jnp.where(same_pair, pm, 0.0)
            o = o + jnp.dot(attn.astype(bf16), v_b, preferred_element_type=f32)

            for dd in range(HGRN_SUB):
                if dd == 0:
                    k_s, b_s, v_s = k, b, v
                else:
                    sh = (c - dd) if rev else dd
                    k_s = pltpu.roll(k, sh, axis=0)
                    b_s = pltpu.roll(b, sh, axis=0)
                    v_s = pltpu.roll(v, sh, axis=0)
                e = jnp.where(pos >= dd, b - b_s, neg)
                pw = (q * k_s * jnp.exp(e)).astype(bf16)
                o = o + jnp.dot(pw, ones_b, preferred_element_type=f32) * v_s

            if rev:
                oacc_ref[sl, :] = oacc_ref[sl, :] + o
            else:
                oacc_ref[sl, :] = o

            kd = (k * jnp.exp(blast - b)).astype(bf16)
            st_new = st * jnp.exp(blast) + lax.dot_general(
                v_b, kd, (((0,), (0,)), ((), ())), preferred_element_type=f32)
            return st_new

        if latent:
            st0 = s_ref[0, 0, d, 0].T
        else:
            st0 = jnp.zeros((DV_A, DK_A), f32)
        return lax.fori_loop(0, n_chunks, body, st0)

    st_f = run_dir(0, ff_ref)
    st_b = run_dir(1, fb_ref)
    if not latent:
        sout_ref[0, 0, 0] = st_f.T
        sout_ref[0, 1, 0] = st_b.T

    o = oacc_ref[...]
    gx = g_ref[...]
    out_ref[...] = (_rms(o, on_ref[...]) * (gx * _sigmoid(gx))).astype(out_ref.dtype)


def _hgrn(z, hgrn_lb, onorm, state, *, layer, latent):
    t_len = DEC_SEQ if latent else SEQ
    n_samples = DEC_BATCH if latent else BATCH
    row0 = (MC // t_len) if latent else 0

    def zspec(col):
        base = col // LANE
        return pl.BlockSpec((t_len, LANE), lambda b, h: (row0 + b, base + h))

    in_specs = [pl.BlockSpec((DEPTH, 2, LANE), lambda b, h: (0, 0, h)),
                pl.BlockSpec((1, LANE), lambda b, h: (0, 0)),
                zspec(COL_AQ), zspec(COL_AV), zspec(COL_AFF), zspec(COL_AFB), zspec(COL_AG)]
    args = [hgrn_lb, onorm, z, z, z, z, z]
    o_spec = pl.BlockSpec((t_len, LANE), lambda b, h: (b, h))
    o_shape = jax.ShapeDtypeStruct((n_samples * t_len, H_A * DV_A), bf16)
    if latent:
        in_specs.append(pl.BlockSpec((1, 1, 2, 1, DK_A, DV_A), lambda b, h: (b, layer, 0, h, 0, 0)))
        args.append(state)
        out_specs, out_shape = o_spec, o_shape
    else:
        out_specs = [o_spec, pl.BlockSpec((1, 2, 1, DK_A, DV_A), lambda b, h: (b, 0, h, 0, 0))]
        out_shape = [o_shape, jax.ShapeDtypeStruct((BATCH, 2, H_A, DK_A, DV_A), f32)]
    return pl.pallas_call(
        functools.partial(_hgrn_kernel, layer=layer, t_len=t_len, latent=latent),
        grid=(n_samples, H_A),
        in_specs=in_specs,
        out_specs=out_specs,
        out_shape=out_shape,
        scratch_shapes=[pltpu.VMEM((t_len, DV_A), f32)],
        compiler_params=_cparams(("parallel", "parallel")),
        name="hgrn_lat" if latent else "hgrn_ctx",
    )(*args)


def _permute_w_in(w_in):
    sizes = (1024, 1024, 1024, 1024, 1024, Q_LORA, KV_LORA, ROPE_B, 1024, 512, 512, N_BRANCH * D_MODEL)
    offs = np.concatenate([[0], np.cumsum(sizes)])
    part = lambda i: w_in[..., offs[i]:offs[i + 1]]
    (a_q, a_v, a_ff, a_fb, a_g, b_q, b_kv, b_pe, c_q, c_k, c_v, g_br) = [part(i) for i in range(12)]
    pad = jnp.zeros(w_in.shape[:-1] + (N_IN_PAD - COL_BPE - ROPE_B,), w_in.dtype)
    return jnp.concatenate([a_q, a_v, a_ff, a_fb, a_g, c_q, g_br, b_q, c_k, c_v, b_kv, b_pe, pad],
                           axis=-1).astype(bf16)


def _permute_wuq(wuq):
    w = wuq.reshape(DEPTH, Q_LORA, H_B, NOPE_B + ROPE_B)
    nope = w[..., :NOPE_B].reshape(DEPTH, Q_LORA, H_B * NOPE_B)
    pe = jnp.pad(w[..., NOPE_B:], ((0, 0), (0, 0), (0, 0), (0, LANE - ROPE_B))).reshape(DEPTH, Q_LORA, H_B * LANE)
    return jnp.concatenate([nope, pe], axis=-1).astype(bf16)


def _pad_lane(v):
    return jnp.pad(v, ((0, 0),) * (v.ndim - 1) + ((0, LANE - v.shape[-1]),))


def kernel(x_prompt, x_sample, state_hgrn, cache_mla_ckv, cache_mla_kpe, cache_gqa_k, cache_gqa_v, c, c_ctx,
           w_mod, b_mod, norm_w, ffn_w13, ffn_w2, w_in, hgrn_lb, hgrn_onorm, mla_qa_norm, mla_kva_norm,
           mla_wuq, mla_wukv, mla_nope_norm, mla_rope_norm, gqa_qk_norm, w_branch, w_out):
    w1_b = ffn_w13[..., :D_FF].astype(bf16)
    w3_b = ffn_w13[..., D_FF:].astype(bf16)
    w2_b = ffn_w2.astype(bf16)
    w_in_b = _permute_w_in(w_in)
    wuq_b = _permute_wuq(mla_wuq)
    wukv_b = mla_wukv.astype(bf16)
    wb_b = w_branch.astype(bf16)
    wo_b = w_out.astype(bf16)
    rope_norm_p = _pad_lane(mla_rope_norm)
    tabs_g = _rope_tables(HD_C)
    tabs_m = _rope_tables(ROPE_B)

    cond = jnp.zeros((N_MOD_ROWS, D_MODEL), f32).at[0].set(c_ctx).at[1:1 + DEC_BATCH].set(c)
    x = jnp.concatenate([x_prompt.reshape(MC, D_MODEL), x_sample.reshape(ML, D_MODEL)], axis=0)

    st_h, st_ckv, st_kpe, st_k, st_v = [], [], [], [], []
    for l in range(DEPTH):
        mod = _mm(cond, [w_mod[l]], tm=N_MOD_ROWS, tn=1024, out_dtype=f32, name="adaln", pro="silu",
                  epi="bias", bias=b_mod[l][None, :]).reshape(N_MOD_ROWS, 9, D_MODEL)

        def ffn(xin, i, ids):
            g = _mm(xin, [w1_b[l, i], w3_b[l, i]], tm=1024, tn=512, out_dtype=bf16, name="ffn_up",
                    pro="normmod", epi="swiglu", norm_w=norm_w[l, 2 * i][None, :], mod=mod,
                    shift_id=ids[0], scale_id=ids[1])
            return _mm(g, [w2_b[l, i]], tm=1024, tn=512, out_dtype=f32, name="ffn_down", epi="residual",
                       res=xin, mod=mod, gate_id=ids[2], gate_scale=0.5)

        x = ffn(x, 0, (0, 1, 2))

        z = _mm(x, [w_in_b[l]], tm=1024, tn=1024, out_dtype=f32, name="in_proj", pro="normmod",
                norm_w=norm_w[l, 1][None, :], mod=mod, shift_id=3, scale_id=4)

        out_a_ctx, s_new = _hgrn(z, hgrn_lb, hgrn_onorm[l][None, :], None, layer=l, latent=False)
        out_a_lat = _hgrn(z, hgrn_lb, hgrn_onorm[l][None, :], state_hgrn, layer=l, latent=True)
        out_a = jnp.concatenate([out_a_ctx, out_a_lat], axis=0)
        st_h.append(s_new)

        bq_n, ckv, kpe, q_g, k_g = _prep_in(z, mla_qa_norm[l][None, :], mla_kva_norm[l][None, :],
                                            rope_norm_p[l, 1][None, :], gqa_qk_norm[l], tabs_g, tabs_m)
        v_g = z[:, COL_CV:COL_CV + KVH_C * HD_C]
        st_ckv.append(ckv[:MC].reshape(BATCH, SEQ, KV_LORA))
        st_kpe.append(kpe[:MC, :ROPE_B].reshape(BATCH, SEQ, ROPE_B))
        st_k.append(k_g[:MC].reshape(BATCH, SEQ, KVH_C, HD_C))
        st_v.append(v_g[:MC].reshape(BATCH, SEQ, KVH_C, HD_C))

        def with_cache(cache, new, width):
            lat = jnp.concatenate([cache.reshape(DEC_BATCH, PAST_LEN, width),
                                   new[MC:].reshape(DEC_BATCH, DEC_SEQ, width)], axis=1)
            return jnp.concatenate([lat.reshape(KV_LAT_ROWS, width), new[:MC]], axis=0)

        cq = _mm(bq_n, [wuq_b[l]], tm=1024, tn=1024, out_dtype=f32, name="mla_uq")
        q_mla = _prep_mla_q(cq, mla_nope_norm[l, 0][None, :], rope_norm_p[l, 0][None, :], tabs_m)
        ckv_all = with_cache(cache_mla_ckv[:, l], ckv, KV_LORA).astype(bf16)
        kpe_all = with_cache(_pad_lane(cache_mla_kpe[:, l]), kpe, LANE)
        kv = _mm(ckv_all, [wukv_b[l]], tm=1024, tn=1024, out_dtype=f32, name="mla_ukv")
        k_mla = _prep_mla_k(kv, kpe_all, mla_nope_norm[l, 1][None, :])
        mla_scale = (NOPE_B + ROPE_B) ** -0.5
        mla_args = dict(dq=2 * LANE, dv=V_B, n_heads=H_B, k_col=lambda h: h, v_col=lambda h: 2 * h + 1,
                        scale=mla_scale)
        out_b_ctx = _attention(q_mla, k_mla, kv, n_samples=BATCH, t=SEQ, s=SEQ, tq=SEQ, q_row0=0, kv_row0=KV_LAT_ROWS,
                               out_rows=MC, name="mla_attn_ctx", **mla_args)
        out_b_lat = _attention(q_mla, k_mla, kv, n_samples=DEC_BATCH, t=DEC_SEQ, s=KV_SEQ, tq=512, q_row0=MC,
                               kv_row0=0, out_rows=ML, name="mla_attn_lat", **mla_args)
        out_b = jnp.concatenate([out_b_ctx, out_b_lat], axis=0)

        k_all = with_cache(cache_gqa_k[:, l].reshape(DEC_BATCH, PAST_LEN, KVH_C * HD_C), k_g, KVH_C * HD_C)
        v_all = with_cache(cache_gqa_v[:, l].reshape(DEC_BATCH, PAST_LEN, KVH_C * HD_C), v_g, KVH_C * HD_C)
        rep = H_C // KVH_C
        gqa_args = dict(dq=HD_C, dv=HD_C, n_heads=H_C, k_col=lambda h: h // rep, v_col=lambda h: h // rep,
                        scale=HD_C ** -0.5)
        out_c_ctx = _attention(q_g, k_all, v_all, n_samples=BATCH, t=SEQ, s=SEQ, tq=SEQ, q_row0=0, kv_row0=KV_LAT_ROWS,
                               out_rows=MC, name="gqa_attn_ctx", **gqa_args)
        out_c_lat = _attention(q_g, k_all, v_all, n_samples=DEC_BATCH, t=DEC_SEQ, s=KV_SEQ, tq=512, q_row0=MC,
                               kv_row0=0, out_rows=ML, name="gqa_attn_lat", **gqa_args)
        out_c = jnp.concatenate([out_c_ctx, out_c_lat], axis=0)

        merged = _merge(out_a, out_b, out_c, wb_b[l], z)
        x = _mm(merged, [wo_b[l]], tm=1024, tn=1024, out_dtype=f32, name="out_proj", epi="residual",
                res=x, mod=mod, gate_id=5, gate_scale=1.0)

        x = ffn(x, 1, (6, 7, 8))

    y_p = x[:MC].reshape(BATCH, SEQ, D_MODEL)
    y_s = x[MC:].reshape(DEC_BATCH, DEC_SEQ, D_MODEL)
    return (y_p, y_s,
            jnp.stack(st_h, axis=1), jnp.stack(st_ckv, axis=1), jnp.stack(st_kpe, axis=1),
            jnp.stack(st_k, axis=1), jnp.stack(st_v, axis=1))
```

```python
import functools

import numpy as np
import jax
import jax.numpy as jnp
from jax import lax
from jax.experimental import pallas as pl
from jax.experimental.pallas import tpu as pltpu

f32 = jnp.float32
bf16 = jnp.bfloat16

D_MODEL = 2048
BATCH = 16
SEQ = 256
DEPTH = 4
DEC_BATCH = 4
DEC_SEQ = 1024
PAST_LEN = 256
GRID_W = 64
ROPE_THETA = 10000.0
EPS = 1e-6
D_FF = 5632
N_BRANCH = 3
BR_W = 1024
H_A = 8
DK_A = 128
DV_A = 128
H_B = 8
Q_LORA = 512
KV_LORA = 256
NOPE_B = 128
ROPE_B = 64
V_B = 128
H_C = 8
KVH_C = 4
HD_C = 128

LANE = 128
MC = BATCH * SEQ
ML = DEC_BATCH * DEC_SEQ
M_ROWS = MC + ML
KV_SEQ = PAST_LEN + DEC_SEQ
KV_LAT_ROWS = DEC_BATCH * KV_SEQ
KV_ROWS = KV_LAT_ROWS + MC
N_MOD_ROWS = 16

COL_AQ, COL_AV, COL_AFF, COL_AFB, COL_AG = 0, 1024, 2048, 3072, 4096
COL_CQ = 5120
COL_GBR = 6144
COL_BQ = 12288
COL_CK = 12800
COL_CV = 13312
COL_BKV = 13824
COL_BPE = 14080
N_IN_PAD = 14336

ROW_TILE = 256
HGRN_C = 128
HGRN_HP = 2
HGRN_SPLIT = 2
VMEM_LIMIT = 56 * 1024 * 1024


def _cparams(sem):
    return pltpu.CompilerParams(dimension_semantics=sem, vmem_limit_bytes=VMEM_LIMIT)


def _sigmoid(x):
    return 1.0 / (1.0 + jnp.exp(-x))


def _rms(x, w, n=None):
    n = x.shape[-1] if n is None else n
    ms = jnp.sum(x * x, axis=-1, keepdims=True) * (1.0 / n)
    return x * lax.rsqrt(ms + EPS) * w


def _mod_row(i, tm):
    n_ctx = MC // tm
    per = DEC_SEQ // tm
    return jnp.where(i < n_ctx, 0, 1 + jnp.maximum(i - n_ctx, 0) // per)


def _mm_kernel(*refs, pro, epi, n_w, shift_id, scale_id, gate_id, gate_scale):
    it = iter(refs)
    x_ref = next(it)
    nw_ref = next(it) if pro in ("norm", "normmod") else None
    modk_ref = next(it) if pro == "normmod" else None
    w_refs = [next(it) for _ in range(n_w)]
    res_ref = next(it) if epi == "residual" else None
    modn_ref = next(it) if epi == "residual" else None
    bias_ref = next(it) if epi == "bias" else None
    o_ref = next(it)
    h_ref = next(it) if pro is not None else None

    if pro is not None:
        @pl.when(pl.program_id(1) == 0)
        def _():
            x = x_ref[...].astype(f32)
            if pro == "silu":
                y = x * _sigmoid(x)
            else:
                y = _rms(x, nw_ref[...])
                if pro == "normmod":
                    y = y * (1.0 + modk_ref[0, scale_id:scale_id + 1, :]) + modk_ref[0, shift_id:shift_id + 1, :]
            h_ref[...] = y.astype(bf16)
        lhs = h_ref[...]
    else:
        lhs = x_ref[...]

    accs = [jnp.dot(lhs, w[...].astype(bf16), preferred_element_type=f32) for w in w_refs]
    if epi == "swiglu":
        a, u = accs
        out = a * _sigmoid(a) * u
    elif epi == "residual":
        out = res_ref[...] + (gate_scale * modn_ref[0, gate_id:gate_id + 1, :]) * accs[0]
    elif epi == "bias":
        out = accs[0] + bias_ref[...]
    else:
        out = accs[0]
    o_ref[...] = out.astype(o_ref.dtype)


def _w_spec(w, lead, col0, k, tn):
    base = col0 // tn
    return pl.BlockSpec((None,) * len(lead) + (k, tn), lambda i, j: tuple(lead) + (0, base + j))


def _mm(x, ws, *, n, tm, tn, out_dtype, name, pro=None, epi="store", norm_w=None, mod=None,
        shift_id=0, scale_id=0, gate_id=0, gate_scale=1.0, res=None, bias=None):
    m, k = x.shape
    grid = (m // tm, n // tn)
    in_specs = [pl.BlockSpec((tm, k), lambda i, j: (i, 0))]
    args = [x]
    if pro in ("norm", "normmod"):
        in_specs.append(pl.BlockSpec((1, k), lambda i, j: (0, 0)))
        args.append(norm_w)
    if pro == "normmod":
        in_specs.append(pl.BlockSpec((1, 9, k), lambda i, j: (_mod_row(i, tm), 0, 0)))
        args.append(mod)
    for w, lead, col0 in ws:
        in_specs.append(_w_spec(w, lead, col0, k, tn))
        args.append(w)
    if epi == "residual":
        in_specs.append(pl.BlockSpec((tm, tn), lambda i, j: (i, j)))
        args.append(res)
        in_specs.append(pl.BlockSpec((1, 9, tn), lambda i, j: (_mod_row(i, tm), 0, j)))
        args.append(mod)
    if epi == "bias":
        in_specs.append(pl.BlockSpec((1, tn), lambda i, j: (0, j)))
        args.append(bias)
    scratch = [pltpu.VMEM((tm, k), bf16)] if pro is not None else []
    kern = functools.partial(_mm_kernel, pro=pro, epi=epi, n_w=len(ws), shift_id=shift_id,
                             scale_id=scale_id, gate_id=gate_id, gate_scale=gate_scale)
    return pl.pallas_call(
        kern,
        grid=grid,
        in_specs=in_specs,
        out_specs=pl.BlockSpec((tm, tn), lambda i, j: (i, j)),
        out_shape=jax.ShapeDtypeStruct((m, n), out_dtype),
        scratch_shapes=scratch,
        compiler_params=_cparams(("parallel", "arbitrary")),
        name=name,
    )(*args)


def _merge_kernel(oa_ref, ob_ref, oc_ref, wb_ref, ga_ref, gb_ref, gc_ref, o_ref):
    acc = None
    for n, (o, g) in enumerate(((oa_ref, ga_ref), (ob_ref, gb_ref), (oc_ref, gc_ref))):
        br = jnp.dot(o[...], wb_ref[n].astype(bf16), preferred_element_type=f32)
        term = _sigmoid(g[...]) * br
        acc = term if acc is None else acc + term
    o_ref[...] = acc.astype(o_ref.dtype)


def _merge(out_a, out_b, out_c, wb, layer, z, *, tm=1024, tn=512):
    grid = (M_ROWS // tm, D_MODEL // tn)
    o_spec = pl.BlockSpec((tm, BR_W), lambda i, j: (i, 0))

    def g_spec(n):
        base = (COL_GBR + n * D_MODEL) // tn
        return pl.BlockSpec((tm, tn), lambda i, j: (i, base + j))

    return pl.pallas_call(
        _merge_kernel,
        grid=grid,
        in_specs=[o_spec, o_spec, o_spec,
                  pl.BlockSpec((None, N_BRANCH, BR_W, tn), lambda i, j: (layer, 0, 0, j)),
                  g_spec(0), g_spec(1), g_spec(2)],
        out_specs=pl.BlockSpec((tm, tn), lambda i, j: (i, j)),
        out_shape=jax.ShapeDtypeStruct((M_ROWS, D_MODEL), bf16),
        compiler_params=_cparams(("parallel", "arbitrary")),
        name="merge",
    )(out_a, out_b, out_c, wb, z, z, z)


def _rope(x, cos, sin, quarter):
    lane = lax.broadcasted_iota(jnp.int32, x.shape, 1)
    first = (lane % (2 * quarter)) < quarter
    partner = jnp.where(first, -pltpu.roll(x, LANE - quarter, axis=1), pltpu.roll(x, quarter, axis=1))
    return x * cos + partner * sin


def _rope_tables(width):
    quarter = width // 4
    t = np.arange(DEC_SEQ)
    inv = ROPE_THETA ** (-np.arange(quarter, dtype=np.float32) / quarter)
    ang_r = (t // GRID_W).astype(np.float32)[:, None] * inv[None, :]
    ang_c = (t % GRID_W).astype(np.float32)[:, None] * inv[None, :]
    ang = np.concatenate([ang_r, ang_r, ang_c, ang_c], axis=1).astype(np.float32)
    cos = np.ones((KV_SEQ, LANE), np.float32)
    sin = np.zeros((KV_SEQ, LANE), np.float32)
    cos[PAST_LEN:, :width] = np.cos(ang)
    sin[PAST_LEN:, :width] = np.sin(ang)
    return jnp.asarray(cos), jnp.asarray(sin)


def _q_table_block(i):
    n_ctx = MC // ROW_TILE
    per = DEC_SEQ // ROW_TILE
    return jnp.where(i < n_ctx, 0, 1 + jnp.maximum(i - n_ctx, 0) % per)


def _prep_in_kernel(bq_ref, bkv_ref, bpe_ref, cq_ref, ck_ref,
                    qa_ref, kva_ref, rpe_ref, qkn_ref,
                    cg_ref, sg_ref, cm_ref, sm_ref,
                    bqn_ref, ckv_ref, kpe_ref, qg_ref, kg_ref):
    bqn_ref[...] = _rms(bq_ref[...], qa_ref[...]).astype(bf16)
    ckv_ref[...] = _rms(bkv_ref[...], kva_ref[...])
    kpe = _rms(bpe_ref[...], rpe_ref[...], n=ROPE_B)
    kpe_ref[...] = _rope(kpe, cm_ref[...], sm_ref[...], ROPE_B // 4)
    cg, sg = cg_ref[...], sg_ref[...]
    for h in range(H_C):
        sl = slice(h * HD_C, (h + 1) * HD_C)
        q = _rms(cq_ref[:, sl], qkn_ref[0:1, :])
        qg_ref[:, sl] = _rope(q, cg, sg, HD_C // 4).astype(bf16)
    for g in range(KVH_C):
        sl = slice(g * HD_C, (g + 1) * HD_C)
        k = _rms(ck_ref[:, sl], qkn_ref[1:2, :])
        kg_ref[:, sl] = _rope(k, cg, sg, HD_C // 4)


def _prep_in(z, qa_norm, kva_norm, rope_norm_k, qk_norm, tabs_g, tabs_m):
    tr = ROW_TILE
    grid = (M_ROWS // tr,)

    def zspec(col, width):
        return pl.BlockSpec((tr, width), lambda i: (i, col // width))

    def wspec(shape):
        return pl.BlockSpec(shape, lambda i: (0, 0))

    tspec = pl.BlockSpec((tr, LANE), lambda i: (_q_table_block(i), 0))

    def ospec(width):
        return pl.BlockSpec((tr, width), lambda i: (i, 0))

    return pl.pallas_call(
        _prep_in_kernel,
        grid=grid,
        in_specs=[zspec(COL_BQ, Q_LORA), zspec(COL_BKV, KV_LORA), zspec(COL_BPE, LANE),
                  zspec(COL_CQ, H_C * HD_C), zspec(COL_CK, KVH_C * HD_C),
                  wspec((1, Q_LORA)), wspec((1, KV_LORA)), wspec((1, LANE)), wspec((2, HD_C)),
                  tspec, tspec, tspec, tspec],
        out_specs=[ospec(Q_LORA), ospec(KV_LORA), ospec(LANE), ospec(H_C * HD_C), ospec(KVH_C * HD_C)],
        out_shape=[jax.ShapeDtypeStruct((M_ROWS, Q_LORA), bf16),
                   jax.ShapeDtypeStruct((M_ROWS, KV_LORA), f32),
                   jax.ShapeDtypeStruct((M_ROWS, LANE), f32),
                   jax.ShapeDtypeStruct((M_ROWS, H_C * HD_C), bf16),
                   jax.ShapeDtypeStruct((M_ROWS, KVH_C * HD_C), f32)],
        compiler_params=_cparams(("parallel",)),
        name="prep_in",
    )(z, z, z, z, z, qa_norm, kva_norm, rope_norm_k, qk_norm, *tabs_g, *tabs_m)


def _prep_mla_q_kernel(cq_ref, nn_ref, rn_ref, cm_ref, sm_ref, o_ref):
    cm, sm = cm_ref[...], sm_ref[...]
    for h in range(H_B):
        nope = _rms(cq_ref[:, h * LANE:(h + 1) * LANE], nn_ref[...])
        pe = _rms(cq_ref[:, (H_B + h) * LANE:(H_B + h + 1) * LANE], rn_ref[...], n=ROPE_B)
        o_ref[:, 2 * h * LANE:(2 * h + 1) * LANE] = nope.astype(bf16)
        o_ref[:, (2 * h + 1) * LANE:(2 * h + 2) * LANE] = _rope(pe, cm, sm, ROPE_B // 4).astype(bf16)


def _prep_mla_q(cq, nope_norm_q, rope_norm_q, tabs_m):
    tr = ROW_TILE
    width = 2 * H_B * LANE
    tspec = pl.BlockSpec((tr, LANE), lambda i: (_q_table_block(i), 0))
    return pl.pallas_call(
        _prep_mla_q_kernel,
        grid=(M_ROWS // tr,),
        in_specs=[pl.BlockSpec((tr, width), lambda i: (i, 0)),
                  pl.BlockSpec((1, LANE), lambda i: (0, 0)), pl.BlockSpec((1, LANE), lambda i: (0, 0)),
                  tspec, tspec],
        out_specs=pl.BlockSpec((tr, width), lambda i: (i, 0)),
        out_shape=jax.ShapeDtypeStruct((M_ROWS, width), bf16),
        compiler_params=_cparams(("parallel",)),
        name="prep_mla_q",
    )(cq, nope_norm_q, rope_norm_q, *tabs_m)


def _prep_mla_k_kernel(kv_ref, kpe_ref, nn_ref, o_ref):
    kpe = kpe_ref[...].astype(bf16)
    for h in range(H_B):
        nope = _rms(kv_ref[:, 2 * h * LANE:(2 * h + 1) * LANE], nn_ref[...])
        o_ref[:, 2 * h * LANE:(2 * h + 1) * LANE] = nope.astype(bf16)
        o_ref[:, (2 * h + 1) * LANE:(2 * h + 2) * LANE] = kpe


def _prep_mla_k(kv, kpe_all, nope_norm_k):
    tr = ROW_TILE
    width = 2 * H_B * LANE
    return pl.pallas_call(
        _prep_mla_k_kernel,
        grid=(KV_ROWS // tr,),
        in_specs=[pl.BlockSpec((tr, width), lambda i: (i, 0)),
                  pl.BlockSpec((tr, LANE), lambda i: (i, 0)),
                  pl.BlockSpec((1, LANE), lambda i: (0, 0))],
        out_specs=pl.BlockSpec((tr, width), lambda i: (i, 0)),
        out_shape=jax.ShapeDtypeStruct((KV_ROWS, width), bf16),
        compiler_params=_cparams(("parallel",)),
        name="prep_mla_k",
    )(kv, kpe_all, nope_norm_k)


def _attn_kernel(q_ref, k_ref, v_ref, o_ref, *, scale):
    q = q_ref[...].astype(bf16)
    k = k_ref[...].astype(bf16)
    s = lax.dot_general(q, k, (((1,), (1,)), ((), ())), preferred_element_type=f32) * scale
    m = jnp.max(s, axis=-1, keepdims=True)
    p = jnp.exp(s - m)
    l = jnp.sum(p, axis=-1, keepdims=True)
    o = jnp.dot(p.astype(bf16), v_ref[...].astype(bf16), preferred_element_type=f32)
    o_ref[...] = (o / l).astype(o_ref.dtype)


def _attention(q, k, v, *, n_samples, t, s, tq, dq, dv, n_heads, q_row0, kv_row0, k_col, v_col,
               out_rows, scale, name):
    nq = t // tq
    q0 = q_row0 // tq
    kv0 = kv_row0 // s
    return pl.pallas_call(
        functools.partial(_attn_kernel, scale=scale),
        grid=(n_samples, n_heads, nq),
        in_specs=[pl.BlockSpec((tq, dq), lambda b, h, i: (q0 + b * nq + i, h)),
                  pl.BlockSpec((s, dq), lambda b, h, i: (kv0 + b, k_col(h))),
                  pl.BlockSpec((s, dv), lambda b, h, i: (kv0 + b, v_col(h)))],
        out_specs=pl.BlockSpec((tq, dv), lambda b, h, i: (b * nq + i, h)),
        out_shape=jax.ShapeDtypeStruct((out_rows, n_heads * dv), bf16),
        compiler_params=_cparams(("parallel", "parallel", "arbitrary")),
        name=name,
    )(q, k, v)


def _hgrn_levels(c):
    w, out = 1, []
    while w < c:
        out.append(w)
        w *= 2
    return out


def _hgrn_consts(c):
    p = np.arange(c)[:, None]
    r = np.arange(c)[None, :]
    sums = [r <= p]
    masks = [r == p]
    for w in _hgrn_levels(c):
        base = (p // (2 * w)) * (2 * w)
        ref = base + w - 1
        later = (p % (2 * w)) >= w
        sums.append(np.where(later, (r > ref) & (r <= p), (r > p) & (r <= ref)))
        masks.append(((r // (2 * w)) * (2 * w) == base) & later & ((r % (2 * w)) < w))
    sums.append(r > p)
    g = np.stack(sums).astype(np.float32)
    m = np.stack(masks).astype(np.float32)
    flip = lambda a: a[:, ::-1, ::-1]
    g = np.stack([g, flip(g)]).reshape(2, -1, c)
    m = np.stack([m, flip(m)]).reshape(2, -1, c)
    return jnp.asarray(np.concatenate([g] * HGRN_SPLIT, axis=-1), bf16), jnp.asarray(m, f32)


def _hgrn_kernel(*refs, layer, t_len, latent):
    if latent:
        (lb_ref, on_ref, gm_ref, mk_ref, q_ref, v_ref, ff_ref, fb_ref, g_ref, s_ref,
         out_ref, of_ref, ob_ref, qd_ref, u_ref, dec_ref, st_ref) = refs
        sout_ref = None
    else:
        (lb_ref, on_ref, gm_ref, mk_ref, q_ref, v_ref, ff_ref, fb_ref, g_ref,
         out_ref, sout_ref, of_ref, ob_ref, qd_ref, u_ref, dec_ref, st_ref) = refs
        s_ref = None
    c = HGRN_C
    hp = HGRN_HP
    n_chunks = t_len // c
    levels = _hgrn_levels(c)
    n_lv = len(levels)
    trans_b = (((1,), (1,)), ((), ()))
    heads = [slice(i * LANE, (i + 1) * LANE) for i in range(hp)]

    row = lax.broadcasted_iota(jnp.int32, (c, LANE), 0)

    if layer > 0:
        ps = [lb_ref[i] for i in range(DEPTH)]
        pmax = functools.reduce(jnp.maximum, ps)
        es = [jnp.exp(p - pmax) for p in ps]
        lb = functools.reduce(lambda a, b: a + b, es[1:layer + 1]) / functools.reduce(lambda a, b: a + b, es)
        log_lb = jnp.log(lb)
        log_1m = jnp.log1p(-lb)

    def log_forget(pre, d):
        ls = jnp.minimum(pre, 0.0) - jnp.log1p(jnp.exp(-jnp.abs(pre)))
        if layer == 0:
            return ls
        a = log_lb[d:d + 1, :]
        cc = log_1m[d:d + 1, :] + ls
        return jnp.maximum(a, cc) + jnp.log1p(jnp.exp(-jnp.abs(a - cc)))

    def intra(d, cidx, pre_ref, oacc_ref):
        sl = pl.ds(pl.multiple_of(cidx * c, c), c)
        qx = q_ref[sl, :]
        q = qx * _sigmoid(qx)
        v_b = v_ref[sl, :].astype(bf16)
        logf = log_forget(pre_ref[sl, :], d)
        k = 1.0 - jnp.exp(logf)
        pieces, rem = [], logf
        for _ in range(HGRN_SPLIT):
            pieces.append(rem.astype(bf16))
            rem = rem - pieces[-1].astype(f32)
        x = jnp.exp(jnp.dot(gm_ref[d], jnp.concatenate(pieces, axis=0),
                            preferred_element_type=f32))

        qd_ref[d, sl, :] = (q * x[0:c]).astype(bf16)
        q_b = q.astype(bf16)
        k_b = k.astype(bf16)
        kd = (k * x[(n_lv + 1) * c:(n_lv + 2) * c]).astype(bf16)

        def stack(a):
            return jnp.concatenate([a[:, hs] for hs in heads], axis=0)

        def head_blocks(pp):
            return [pp[i * c:(i + 1) * c, i * c:(i + 1) * c] for i in range(hp)]

        pp = lax.dot_general(stack(q_b), stack(k_b), trans_b, preferred_element_type=f32)
        attn = [mk_ref[d, 0:c, :] * blk for blk in head_blocks(pp)]
        for li, w in enumerate(levels):
            later = ((row % (2 * w)) < w) if d == 1 else ((row % (2 * w)) >= w)
            later = jnp.concatenate([later] * hp, axis=1)
            r = stack((jnp.where(later, q, k) * x[(1 + li) * c:(2 + li) * c]).astype(bf16))
            pp = lax.dot_general(r, r, trans_b, preferred_element_type=f32)
            mask = mk_ref[d, (1 + li) * c:(2 + li) * c, :]
            attn = [a + mask * blk for a, blk in zip(attn, head_blocks(pp))]
        for hh, hs in enumerate(heads):
            oacc_ref[sl, hs] = jnp.dot(attn[hh].astype(bf16), v_b[:, hs], preferred_element_type=f32)
            u_ref[d, cidx, hh] = lax.dot_general(v_b[:, hs], kd[:, hs], (((0,), (0,)), ((), ())),
                                                 preferred_element_type=f32)
        decay = x[0:1] if d == 1 else x[c - 1:c]
        dec_ref[d, cidx] = jnp.broadcast_to(decay, (8, hp * LANE))

    def intra_body(ci, carry):
        intra(0, ci, ff_ref, of_ref)
        intra(1, ci, fb_ref, ob_ref)
        return carry

    lax.fori_loop(0, n_chunks, intra_body, 0)

    for d in range(2):
        for hh in range(hp):
            if latent:
                st_ref[d, hh] = s_ref[0, 0, d, hh].T
            else:
                st_ref[d, hh] = jnp.zeros((DV_A, DK_A), f32)

    def scan(d, cidx, oacc_ref):
        sl = pl.ds(pl.multiple_of(cidx * c, c), c)
        dec = dec_ref[d, cidx]
        for hh, hs in enumerate(heads):
            st = st_ref[d, hh]
            oacc_ref[sl, hs] = oacc_ref[sl, hs] + lax.dot_general(
                qd_ref[d, sl, hs], st.astype(bf16), trans_b, preferred_element_type=f32)
            st_ref[d, hh] = st * dec[0:1, hs] + u_ref[d, cidx, hh]

    def scan_body(ci, carry):
        scan(0, ci, of_ref)
        scan(1, n_chunks - 1 - ci, ob_ref)
        return carry

    lax.fori_loop(0, n_chunks, scan_body, 0)

    if not latent:
        for d in range(2):
            for hh in range(hp):
                sout_ref[0, d, hh] = st_ref[d, hh].T

    gx = g_ref[...]
    gate = gx * _sigmoid(gx)
    for hs in heads:
        o = of_ref[:, hs] + ob_ref[:, hs]
        out_ref[:, hs] = (_rms(o, on_ref[...]) * gate[:, hs]).astype(out_ref.dtype)


def _hgrn(z, hgrn_lb, onorm, state, consts, *, layer, latent):
    t_len = DEC_SEQ if latent else SEQ
    n_samples = DEC_BATCH if latent else BATCH
    row0 = (MC // t_len) if latent else 0
    gmat, masks = consts
    hp = HGRN_HP
    wide = hp * LANE

    def zspec(col):
        base = col // wide
        return pl.BlockSpec((t_len, wide), lambda b, h: (row0 + b, base + h))

    in_specs = [pl.BlockSpec((DEPTH, 2, wide), lambda b, h: (0, 0, h)),
                pl.BlockSpec((1, LANE), lambda b, h: (0, 0)),
                pl.BlockSpec(gmat.shape, lambda b, h: (0, 0, 0)),
                pl.BlockSpec(masks.shape, lambda b, h: (0, 0, 0)),
                zspec(COL_AQ), zspec(COL_AV), zspec(COL_AFF), zspec(COL_AFB), zspec(COL_AG)]
    args = [hgrn_lb, onorm, gmat, masks, z, z, z, z, z]
    o_spec = pl.BlockSpec((t_len, wide), lambda b, h: (b, h))
    o_shape = jax.ShapeDtypeStruct((n_samples * t_len, H_A * DV_A), bf16)
    if latent:
        in_specs.append(pl.BlockSpec((1, 1, 2, hp, DK_A, DV_A), lambda b, h: (b, layer, 0, h, 0, 0)))
        args.append(state)
        out_specs, out_shape = o_spec, o_shape
    else:
        out_specs = [o_spec, pl.BlockSpec((1, 2, hp, DK_A, DV_A), lambda b, h: (b, 0, h, 0, 0))]
        out_shape = [o_shape, jax.ShapeDtypeStruct((BATCH, 2, H_A, DK_A, DV_A), f32)]
    n_chunks = t_len // HGRN_C
    return pl.pallas_call(
        functools.partial(_hgrn_kernel, layer=layer, t_len=t_len, latent=latent),
        grid=(n_samples, H_A // hp),
        in_specs=in_specs,
        out_specs=out_specs,
        out_shape=out_shape,
        scratch_shapes=[pltpu.VMEM((t_len, wide), f32), pltpu.VMEM((t_len, wide), f32),
                        pltpu.VMEM((2, t_len, wide), bf16),
                        pltpu.VMEM((2, n_chunks, hp, DV_A, DK_A), f32),
                        pltpu.VMEM((2, n_chunks, 8, wide), f32),
                        pltpu.VMEM((2, hp, DV_A, DK_A), f32)],
        compiler_params=_cparams(("parallel", "parallel")),
        name="hgrn_lat" if latent else "hgrn_ctx",
    )(*args)


def _permute_w_in(w_in):
    sizes = (1024, 1024, 1024, 1024, 1024, Q_LORA, KV_LORA, ROPE_B, 1024, 512, 512, N_BRANCH * D_MODEL)
    offs = np.concatenate([[0], np.cumsum(sizes)])
    part = lambda i: w_in[..., offs[i]:offs[i + 1]]
    (a_q, a_v, a_ff, a_fb, a_g, b_q, b_kv, b_pe, c_q, c_k, c_v, g_br) = [part(i) for i in range(12)]
    pad = jnp.zeros(w_in.shape[:-1] + (N_IN_PAD - COL_BPE - ROPE_B,), w_in.dtype)
    return jnp.concatenate([a_q, a_v, a_ff, a_fb, a_g, c_q, g_br, b_q, c_k, c_v, b_kv, b_pe, pad],
                           axis=-1).astype(bf16)


def _permute_wuq(wuq):
    w = wuq.reshape(DEPTH, Q_LORA, H_B, NOPE_B + ROPE_B)
    nope = w[..., :NOPE_B].reshape(DEPTH, Q_LORA, H_B * NOPE_B)
    pe = jnp.pad(w[..., NOPE_B:], ((0, 0), (0, 0), (0, 0), (0, LANE - ROPE_B))).reshape(DEPTH, Q_LORA, H_B * LANE)
    return jnp.concatenate([nope, pe], axis=-1).astype(bf16)


def _pad_lane(v):
    return jnp.pad(v, ((0, 0),) * (v.ndim - 1) + ((0, LANE - v.shape[-1]),))


def kernel(x_prompt, x_sample, state_hgrn, cache_mla_ckv, cache_mla_kpe, cache_gqa_k, cache_gqa_v, c, c_ctx,
           w_mod, b_mod, norm_w, ffn_w13, ffn_w2, w_in, hgrn_lb, hgrn_onorm, mla_qa_norm, mla_kva_norm,
           mla_wuq, mla_wukv, mla_nope_norm, mla_rope_norm, gqa_qk_norm, w_branch, w_out):
    w_in_b = _permute_w_in(w_in)
    wuq_b = _permute_wuq(mla_wuq)
    wukv_b = mla_wukv.astype(bf16)
    hgrn_consts = _hgrn_consts(HGRN_C)
    rope_norm_p = _pad_lane(mla_rope_norm)
    tabs_g = _rope_tables(HD_C)
    tabs_m = _rope_tables(ROPE_B)

    cond = jnp.zeros((N_MOD_ROWS, D_MODEL), f32).at[0].set(c_ctx).at[1:1 + DEC_BATCH].set(c)
    x = jnp.concatenate([x_prompt.reshape(MC, D_MODEL), x_sample.reshape(ML, D_MODEL)], axis=0)

    st_h, st_ckv, st_kpe, st_k, st_v = [], [], [], [], []
    for l in range(DEPTH):
        mod = _mm(cond, [(w_mod, (l,), 0)], n=9 * D_MODEL, tm=N_MOD_ROWS, tn=1024, out_dtype=f32, name="adaln",
                  pro="silu", epi="bias", bias=b_mod[l][None, :]).reshape(N_MOD_ROWS, 9, D_MODEL)

        def ffn(xin, i, ids):
            g = _mm(xin, [(ffn_w13, (l, i), 0), (ffn_w13, (l, i), D_FF)], n=D_FF, tm=1024, tn=512,
                    out_dtype=bf16, name="ffn_up", pro="normmod", epi="swiglu",
                    norm_w=norm_w[l, 2 * i][None, :], mod=mod, shift_id=ids[0], scale_id=ids[1])
            return _mm(g, [(ffn_w2, (l, i), 0)], n=D_MODEL, tm=1024, tn=256, out_dtype=f32, name="ffn_down",
                       epi="residual", res=xin, mod=mod, gate_id=ids[2], gate_scale=0.5)

        x = ffn(x, 0, (0, 1, 2))

        z = _mm(x, [(w_in_b, (l,), 0)], n=N_IN_PAD, tm=1024, tn=1024, out_dtype=f32, name="in_proj",
                pro="normmod", norm_w=norm_w[l, 1][None, :], mod=mod, shift_id=3, scale_id=4)

        out_a_ctx, s_new = _hgrn(z, hgrn_lb, hgrn_onorm[l][None, :], None, hgrn_consts, layer=l, latent=False)
        out_a_lat = _hgrn(z, hgrn_lb, hgrn_onorm[l][None, :], state_hgrn, hgrn_consts, layer=l, latent=True)
        out_a = jnp.concatenate([out_a_ctx, out_a_lat], axis=0)
        st_h.append(s_new)

        bq_n, ckv, kpe, q_g, k_g = _prep_in(z, mla_qa_norm[l][None, :], mla_kva_norm[l][None, :],
                                            rope_norm_p[l, 1][None, :], gqa_qk_norm[l], tabs_g, tabs_m)
        v_g = z[:, COL_CV:COL_CV + KVH_C * HD_C]
        st_ckv.append(ckv[:MC].reshape(BATCH, SEQ, KV_LORA))
        st_kpe.append(kpe[:MC, :ROPE_B].reshape(BATCH, SEQ, ROPE_B))
        st_k.append(k_g[:MC].reshape(BATCH, SEQ, KVH_C, HD_C))
        st_v.append(v_g[:MC].reshape(BATCH, SEQ, KVH_C, HD_C))

        def with_cache(cache, new, width):
            lat = jnp.concatenate([cache.reshape(DEC_BATCH, PAST_LEN, width),
                                   new[MC:].reshape(DEC_BATCH, DEC_SEQ, width)], axis=1)
            return jnp.concatenate([lat.reshape(KV_LAT_ROWS, width), new[:MC]], axis=0)

        cq = _mm(bq_n, [(wuq_b, (l,), 0)], n=2 * H_B * LANE, tm=1024, tn=1024, out_dtype=f32, name="mla_uq")
        q_mla = _prep_mla_q(cq, mla_nope_norm[l, 0][None, :], rope_norm_p[l, 0][None, :], tabs_m)
        ckv_all = with_cache(cache_mla_ckv[:, l], ckv, KV_LORA).astype(bf16)
        kpe_all = with_cache(_pad_lane(cache_mla_kpe[:, l]), kpe, LANE)
        kv = _mm(ckv_all, [(wukv_b, (l,), 0)], n=2 * H_B * LANE, tm=1024, tn=1024, out_dtype=f32,
                 name="mla_ukv")
        k_mla = _prep_mla_k(kv, kpe_all, mla_nope_norm[l, 1][None, :])
        mla_scale = (NOPE_B + ROPE_B) ** -0.5
        mla_args = dict(dq=2 * LANE, dv=V_B, n_heads=H_B, k_col=lambda h: h, v_col=lambda h: 2 * h + 1,
                        scale=mla_scale)
        out_b_ctx = _attention(q_mla, k_mla, kv, n_samples=BATCH, t=SEQ, s=SEQ, tq=SEQ, q_row0=0,
                               kv_row0=KV_LAT_ROWS, out_rows=MC, name="mla_attn_ctx", **mla_args)
        out_b_lat = _attention(q_mla, k_mla, kv, n_samples=DEC_BATCH, t=DEC_SEQ, s=KV_SEQ, tq=512, q_row0=MC,
                               kv_row0=0, out_rows=ML, name="mla_attn_lat", **mla_args)
        out_b = jnp.concatenate([out_b_ctx, out_b_lat], axis=0)

        k_all = with_cache(cache_gqa_k[:, l].reshape(DEC_BATCH, PAST_LEN, KVH_C * HD_C), k_g, KVH_C * HD_C)
        v_all = with_cache(cache_gqa_v[:, l].reshape(DEC_BATCH, PAST_LEN, KVH_C * HD_C), v_g, KVH_C * HD_C)
        rep = H_C // KVH_C
        gqa_args = dict(dq=HD_C, dv=HD_C, n_heads=H_C, k_col=lambda h: h // rep, v_col=lambda h: h // rep,
                        scale=HD_C ** -0.5)
        out_c_ctx = _attention(q_g, k_all, v_all, n_samples=BATCH, t=SEQ, s=SEQ, tq=SEQ, q_row0=0,
                               kv_row0=KV_LAT_ROWS, out_rows=MC, name="gqa_attn_ctx", **gqa_args)
        out_c_lat = _attention(q_g, k_all, v_all, n_samples=DEC_BATCH, t=DEC_SEQ, s=KV_SEQ, tq=512, q_row0=MC,
                               kv_row0=0, out_rows=ML, name="gqa_attn_lat", **gqa_args)
        out_c = jnp.concatenate([out_c_ctx, out_c_lat], axis=0)

        merged = _merge(out_a, out_b, out_c, w_branch, l, z)
        x = _mm(merged, [(w_out, (l,), 0)], n=D_MODEL, tm=1024, tn=1024, out_dtype=f32, name="out_proj",
                epi="residual", res=x, mod=mod, gate_id=5, gate_scale=1.0)

        x = ffn(x, 1, (6, 7, 8))

    y_p = x[:MC].reshape(BATCH, SEQ, D_MODEL)
    y_s = x[MC:].reshape(DEC_BATCH, DEC_SEQ, D_MODEL)
    return (y_p, y_s,
            jnp.stack(st_h, axis=1), jnp.stack(st_ckv, axis=1), jnp.stack(st_kpe, axis=1),
            jnp.stack(st_k, axis=1), jnp.stack(st_v, axis=1))
```

```python
import functools

import numpy as np
import jax
import jax.numpy as jnp
from jax import lax
from jax.experimental import pallas as pl
from jax.experimental.pallas import tpu as pltpu

f32 = jnp.float32
bf16 = jnp.bfloat16

D_MODEL = 2048
BATCH = 16
SEQ = 256
DEPTH = 4
DEC_BATCH = 4
DEC_SEQ = 1024
PAST_LEN = 256
GRID_W = 64
ROPE_THETA = 10000.0
EPS = 1e-6
D_FF = 5632
N_BRANCH = 3
BR_W = 1024
H_A = 8
DK_A = 128
DV_A = 128
H_B = 8
Q_LORA = 512
KV_LORA = 256
NOPE_B = 128
ROPE_B = 64
V_B = 128
H_C = 8
KVH_C = 4
HD_C = 128

LANE = 128
MC = BATCH * SEQ
ML = DEC_BATCH * DEC_SEQ
M_ROWS = MC + ML
KV_SEQ = PAST_LEN + DEC_SEQ
KV_LAT_ROWS = DEC_BATCH * KV_SEQ
KV_ROWS = KV_LAT_ROWS + MC
N_MOD_ROWS = 16

COL_AQ, COL_AV, COL_AFF, COL_AFB, COL_AG = 0, 1024, 2048, 3072, 4096
COL_CQ = 5120
COL_GBR = 6144
COL_BQ = 12288
COL_CK = 12800
COL_CV = 13312
COL_BKV = 13824
COL_BPE = 14080
N_IN_PAD = 14336

ROW_TILE = 256
HGRN_C = 128
HGRN_HP = 2
HGRN_SPLIT = 2
VMEM_LIMIT = 56 * 1024 * 1024


def _cparams(sem):
    return pltpu.CompilerParams(dimension_semantics=sem, vmem_limit_bytes=VMEM_LIMIT)


def _sigmoid(x):
    return 1.0 / (1.0 + jnp.exp(-x))


def _rms(x, w, n=None):
    n = x.shape[-1] if n is None else n
    ms = jnp.sum(x * x, axis=-1, keepdims=True) * (1.0 / n)
    return x * lax.rsqrt(ms + EPS) * w


def _mod_row(i, tm):
    n_ctx = MC // tm
    per = DEC_SEQ // tm
    return jnp.where(i < n_ctx, 0, 1 + jnp.maximum(i - n_ctx, 0) // per)


def _mm_kernel(*refs, pro, epi, n_w, shift_id, scale_id, gate_id, gate_scale):
    it = iter(refs)
    x_ref = next(it)
    nw_ref = next(it) if pro in ("norm", "normmod") else None
    modk_ref = next(it) if pro == "normmod" else None
    w_refs = [next(it) for _ in range(n_w)]
    res_ref = next(it) if epi == "residual" else None
    modn_ref = next(it) if epi == "residual" else None
    bias_ref = next(it) if epi == "bias" else None
    o_ref = next(it)
    h_ref = next(it) if pro is not None else None

    if pro is not None:
        @pl.when(pl.program_id(1) == 0)
        def _():
            x = x_ref[...].astype(f32)
            if pro == "silu":
                y = x * _sigmoid(x)
            else:
                y = _rms(x, nw_ref[...])
                if pro == "normmod":
                    y = y * (1.0 + modk_ref[0, scale_id:scale_id + 1, :]) + modk_ref[0, shift_id:shift_id + 1, :]
            h_ref[...] = y.astype(bf16)
        lhs = h_ref[...]
    else:
        lhs = x_ref[...]

    accs = [jnp.dot(lhs, w[...].astype(bf16), preferred_element_type=f32) for w in w_refs]
    if epi == "swiglu":
        a, u = accs
        out = a * _sigmoid(a) * u
    elif epi == "residual":
        out = res_ref[...] + (gate_scale * modn_ref[0, gate_id:gate_id + 1, :]) * accs[0]
    elif epi == "bias":
        out = accs[0] + bias_ref[...]
    else:
        out = accs[0]
    o_ref[...] = out.astype(o_ref.dtype)


def _w_spec(w, lead, col0, k, tn):
    base = col0 // tn
    return pl.BlockSpec((None,) * len(lead) + (k, tn), lambda i, j: tuple(lead) + (0, base + j))


def _mm(x, ws, *, n, tm, tn, out_dtype, name, pro=None, epi="store", norm_w=None, mod=None,
        shift_id=0, scale_id=0, gate_id=0, gate_scale=1.0, res=None, bias=None):
    m, k = x.shape
    grid = (m // tm, n // tn)
    in_specs = [pl.BlockSpec((tm, k), lambda i, j: (i, 0))]
    args = [x]
    if pro in ("norm", "normmod"):
        in_specs.append(pl.BlockSpec((1, k), lambda i, j: (0, 0)))
        args.append(norm_w)
    if pro == "normmod":
        in_specs.append(pl.BlockSpec((1, 9, k), lambda i, j: (_mod_row(i, tm), 0, 0)))
        args.append(mod)
    for w, lead, col0 in ws:
        in_specs.append(_w_spec(w, lead, col0, k, tn))
        args.append(w)
    if epi == "residual":
        in_specs.append(pl.BlockSpec((tm, tn), lambda i, j: (i, j)))
        args.append(res)
        in_specs.append(pl.BlockSpec((1, 9, tn), lambda i, j: (_mod_row(i, tm), 0, j)))
        args.append(mod)
    if epi == "bias":
        in_specs.append(pl.BlockSpec((1, tn), lambda i, j: (0, j)))
        args.append(bias)
    scratch = [pltpu.VMEM((tm, k), bf16)] if pro is not None else []
    kern = functools.partial(_mm_kernel, pro=pro, epi=epi, n_w=len(ws), shift_id=shift_id,
                             scale_id=scale_id, gate_id=gate_id, gate_scale=gate_scale)
    return pl.pallas_call(
        kern,
        grid=grid,
        in_specs=in_specs,
        out_specs=pl.BlockSpec((tm, tn), lambda i, j: (i, j)),
        out_shape=jax.ShapeDtypeStruct((m, n), out_dtype),
        scratch_shapes=scratch,
        compiler_params=_cparams(("parallel", "arbitrary")),
        name=name,
    )(*args)


def _merge_kernel(oa_ref, ob_ref, oc_ref, wb_ref, ga_ref, gb_ref, gc_ref, o_ref):
    acc = None
    for n, (o, g) in enumerate(((oa_ref, ga_ref), (ob_ref, gb_ref), (oc_ref, gc_ref))):
        br = jnp.dot(o[...], wb_ref[n].astype(bf16), preferred_element_type=f32)
        term = _sigmoid(g[...]) * br
        acc = term if acc is None else acc + term
    o_ref[...] = acc.astype(o_ref.dtype)


def _merge(out_a, out_b, out_c, wb, layer, z, *, tm=1024, tn=512):
    grid = (M_ROWS // tm, D_MODEL // tn)
    o_spec = pl.BlockSpec((tm, BR_W), lambda i, j: (i, 0))

    def g_spec(n):
        base = (COL_GBR + n * D_MODEL) // tn
        return pl.BlockSpec((tm, tn), lambda i, j: (i, base + j))

    return pl.pallas_call(
        _merge_kernel,
        grid=grid,
        in_specs=[o_spec, o_spec, o_spec,
                  pl.BlockSpec((None, N_BRANCH, BR_W, tn), lambda i, j: (layer, 0, 0, j)),
                  g_spec(0), g_spec(1), g_spec(2)],
        out_specs=pl.BlockSpec((tm, tn), lambda i, j: (i, j)),
        out_shape=jax.ShapeDtypeStruct((M_ROWS, D_MODEL), bf16),
        compiler_params=_cparams(("parallel", "arbitrary")),
        name="merge",
    )(out_a, out_b, out_c, wb, z, z, z)


def _rope(x, cos, sin, quarter):
    lane = lax.broadcasted_iota(jnp.int32, x.shape, 1)
    first = (lane % (2 * quarter)) < quarter
    partner = jnp.where(first, -pltpu.roll(x, LANE - quarter, axis=1), pltpu.roll(x, quarter, axis=1))
    return x * cos + partner * sin


def _rope_tables(width):
    quarter = width // 4
    t = np.arange(DEC_SEQ)
    inv = ROPE_THETA ** (-np.arange(quarter, dtype=np.float32) / quarter)
    ang_r = (t // GRID_W).astype(np.float32)[:, None] * inv[None, :]
    ang_c = (t % GRID_W).astype(np.float32)[:, None] * inv[None, :]
    ang = np.concatenate([ang_r, ang_r, ang_c, ang_c], axis=1).astype(np.float32)
    cos = np.ones((KV_SEQ, LANE), np.float32)
    sin = np.zeros((KV_SEQ, LANE), np.float32)
    cos[PAST_LEN:, :width] = np.cos(ang)
    sin[PAST_LEN:, :width] = np.sin(ang)
    return cos, sin


def _q_table_block(i):
    n_ctx = MC // ROW_TILE
    per = DEC_SEQ // ROW_TILE
    return jnp.where(i < n_ctx, 0, 1 + jnp.maximum(i - n_ctx, 0) % per)


def _softmax_pv(s, v_b):
    m = jnp.max(s, axis=-1, keepdims=True)
    p = jnp.exp(s - m)
    l = jnp.sum(p, axis=-1, keepdims=True)
    return jnp.dot(p.astype(bf16), v_b, preferred_element_type=f32) / l


_TRANS_B = (((1,), (1,)), ((), ()))


def _mla_prep_kernel(bq_ref, bkv_ref, bpe_ref, qa_ref, kva_ref, rn_ref, nn_ref, wuq_ref, cm_ref, sm_ref,
                     q_ref, ckv_ref, kpe_ref):
    cm, sm = cm_ref[...], sm_ref[...]
    ckv_ref[...] = _rms(bkv_ref[...], kva_ref[...])
    kpe_ref[...] = _rope(_rms(bpe_ref[...], rn_ref[1:2, :], n=ROPE_B), cm, sm, ROPE_B // 4)
    cq = jnp.dot(_rms(bq_ref[...], qa_ref[...]).astype(bf16), wuq_ref[...], preferred_element_type=f32)
    for h in range(H_B):
        nope = _rms(cq[:, h * LANE:(h + 1) * LANE], nn_ref[...])
        pe = _rms(cq[:, (H_B + h) * LANE:(H_B + h + 1) * LANE], rn_ref[0:1, :], n=ROPE_B)
        q_ref[:, 2 * h * LANE:(2 * h + 1) * LANE] = nope.astype(bf16)
        q_ref[:, (2 * h + 1) * LANE:(2 * h + 2) * LANE] = _rope(pe, cm, sm, ROPE_B // 4).astype(bf16)


def _mla_prep(z, qa_norm, kva_norm, rope_norm, nope_norm_q, wuq, layer, tabs_m):
    tr = ROW_TILE
    width = 2 * H_B * LANE

    def zspec(col, w):
        return pl.BlockSpec((tr, w), lambda i: (i, col // w))

    def wspec(shape):
        return pl.BlockSpec(shape, lambda i: (0, 0))

    tspec = pl.BlockSpec((tr, LANE), lambda i: (_q_table_block(i), 0))
    return pl.pallas_call(
        _mla_prep_kernel,
        grid=(M_ROWS // tr,),
        in_specs=[zspec(COL_BQ, Q_LORA), zspec(COL_BKV, KV_LORA), zspec(COL_BPE, LANE),
                  wspec((1, Q_LORA)), wspec((1, KV_LORA)), wspec((2, LANE)), wspec((1, LANE)),
                  pl.BlockSpec((None, Q_LORA, width), lambda i: (layer, 0, 0)), tspec, tspec],
        out_specs=[pl.BlockSpec((tr, width), lambda i: (i, 0)),
                   pl.BlockSpec((tr, KV_LORA), lambda i: (i, 0)),
                   pl.BlockSpec((tr, LANE), lambda i: (i, 0))],
        out_shape=[jax.ShapeDtypeStruct((M_ROWS, width), bf16),
                   jax.ShapeDtypeStruct((M_ROWS, KV_LORA), f32),
                   jax.ShapeDtypeStruct((M_ROWS, LANE), f32)],
        compiler_params=_cparams(("parallel",)),
        name="mla_prep",
    )(z, z, z, qa_norm, kva_norm, rope_norm, nope_norm_q, wuq, *tabs_m)


def _mla_kv(ckv, kpe, w, nn):
    kv = jnp.dot(ckv.astype(bf16), w, preferred_element_type=f32)
    n_h = w.shape[1] // (2 * LANE)
    kpe_b = kpe.astype(bf16)
    ks, vs = [], []
    for h in range(n_h):
        nope = _rms(kv[:, 2 * h * LANE:(2 * h + 1) * LANE], nn)
        ks.append(jnp.concatenate([nope.astype(bf16), kpe_b], axis=1))
        vs.append(kv[:, (2 * h + 1) * LANE:(2 * h + 2) * LANE].astype(bf16))
    return ks, vs


def _mla_ctx_kernel(q_ref, ckv_ref, kpe_ref, w_ref, nn_ref, o_ref, *, scale):
    ks, vs = _mla_kv(ckv_ref[...], kpe_ref[...], w_ref[...], nn_ref[...])
    for h in range(H_B):
        s = lax.dot_general(q_ref[:, 2 * h * LANE:(2 * h + 2) * LANE], ks[h], _TRANS_B,
                            preferred_element_type=f32) * scale
        o_ref[:, h * V_B:(h + 1) * V_B] = _softmax_pv(s, vs[h]).astype(o_ref.dtype)


def _mla_ctx(q_mla, ckv, kpe, wukv, layer, nope_norm_k, scale):
    width = 2 * H_B * LANE
    return pl.pallas_call(
        functools.partial(_mla_ctx_kernel, scale=scale),
        grid=(BATCH,),
        in_specs=[pl.BlockSpec((SEQ, width), lambda b: (b, 0)),
                  pl.BlockSpec((SEQ, KV_LORA), lambda b: (b, 0)),
                  pl.BlockSpec((SEQ, LANE), lambda b: (b, 0)),
                  pl.BlockSpec((None, KV_LORA, width), lambda b: (layer, 0, 0)),
                  pl.BlockSpec((1, LANE), lambda b: (0, 0))],
        out_specs=pl.BlockSpec((SEQ, H_B * V_B), lambda b: (b, 0)),
        out_shape=jax.ShapeDtypeStruct((M_ROWS, H_B * V_B), bf16),
        compiler_params=_cparams(("parallel",)),
        name="mla_attn_ctx",
    )(q_mla, ckv, kpe, wukv, nope_norm_k)


def _mla_lat_kernel(prev_ref, q_ref, ckvc_ref, kpec_ref, ckvn_ref, kpen_ref, w_ref, nn_ref, o_ref,
                    k_scr, v_scr, *, scale):
    del prev_ref

    @pl.when(pl.program_id(2) == 0)
    def _():
        ks, vs = _mla_kv(ckvc_ref[...], kpec_ref[...], w_ref[...], nn_ref[...])
        k_scr[0:PAST_LEN, :] = ks[0]
        v_scr[0:PAST_LEN, :] = vs[0]
        ks, vs = _mla_kv(ckvn_ref[...], kpen_ref[...], w_ref[...], nn_ref[...])
        k_scr[PAST_LEN:KV_SEQ, :] = ks[0]
        v_scr[PAST_LEN:KV_SEQ, :] = vs[0]

    s = lax.dot_general(q_ref[...], k_scr[...], _TRANS_B, preferred_element_type=f32) * scale
    o_ref[...] = _softmax_pv(s, v_scr[...]).astype(o_ref.dtype)


def _mla_lat(prev, q_mla, ckv, kpe, cache_ckv, cache_kpe, wukv, layer, nope_norm_k, scale, *, tq=512):
    nq = DEC_SEQ // tq
    lat0 = MC // DEC_SEQ
    return pl.pallas_call(
        functools.partial(_mla_lat_kernel, scale=scale),
        grid=(DEC_BATCH, H_B, nq),
        in_specs=[pl.BlockSpec(memory_space=pl.ANY),
                  pl.BlockSpec((tq, 2 * LANE), lambda b, h, i: (MC // tq + b * nq + i, h)),
                  pl.BlockSpec((None, None, PAST_LEN, KV_LORA), lambda b, h, i: (b, layer, 0, 0)),
                  pl.BlockSpec((None, None, PAST_LEN, LANE), lambda b, h, i: (b, layer, 0, 0)),
                  pl.BlockSpec((DEC_SEQ, KV_LORA), lambda b, h, i: (lat0 + b, 0)),
                  pl.BlockSpec((DEC_SEQ, LANE), lambda b, h, i: (lat0 + b, 0)),
                  pl.BlockSpec((None, KV_LORA, 2 * LANE), lambda b, h, i: (layer, 0, h)),
                  pl.BlockSpec((1, LANE), lambda b, h, i: (0, 0))],
        out_specs=pl.BlockSpec((tq, V_B), lambda b, h, i: (MC // tq + b * nq + i, h)),
        out_shape=jax.ShapeDtypeStruct((M_ROWS, H_B * V_B), bf16),
        scratch_shapes=[pltpu.VMEM((KV_SEQ, 2 * LANE), bf16), pltpu.VMEM((KV_SEQ, V_B), bf16)],
        input_output_aliases={0: 0},
        compiler_params=_cparams(("parallel", "parallel", "arbitrary")),
        name="mla_attn_lat",
    )(prev, q_mla, cache_ckv, cache_kpe, ckv, kpe, wukv, nope_norm_k)


def _gqa_ctx_kernel(cq_ref, ck_ref, cv_ref, qkn_ref, o_ref, kg_ref, *, scale):
    rep = H_C // KVH_C
    for g in range(KVH_C):
        gs = slice(g * HD_C, (g + 1) * HD_C)
        k = _rms(ck_ref[:, gs], qkn_ref[1:2, :])
        kg_ref[:, gs] = k
        k_b = k.astype(bf16)
        v_b = cv_ref[:, gs].astype(bf16)
        for h in range(g * rep, (g + 1) * rep):
            hs = slice(h * HD_C, (h + 1) * HD_C)
            q = _rms(cq_ref[:, hs], qkn_ref[0:1, :]).astype(bf16)
            s = lax.dot_general(q, k_b, _TRANS_B, preferred_element_type=f32) * scale
            o_ref[:, hs] = _softmax_pv(s, v_b).astype(o_ref.dtype)


def _gqa_ctx(z, qk_norm, scale):
    def zspec(col, w):
        return pl.BlockSpec((SEQ, w), lambda b: (b, col // w))

    return pl.pallas_call(
        functools.partial(_gqa_ctx_kernel, scale=scale),
        grid=(BATCH,),
        in_specs=[zspec(COL_CQ, H_C * HD_C), zspec(COL_CK, KVH_C * HD_C), zspec(COL_CV, KVH_C * HD_C),
                  pl.BlockSpec((2, HD_C), lambda b: (0, 0))],
        out_specs=[pl.BlockSpec((SEQ, H_C * HD_C), lambda b: (b, 0)),
                   pl.BlockSpec((SEQ, KVH_C * HD_C), lambda b: (b, 0))],
        out_shape=[jax.ShapeDtypeStruct((M_ROWS, H_C * HD_C), bf16),
                   jax.ShapeDtypeStruct((MC, KVH_C * HD_C), f32)],
        compiler_params=_cparams(("parallel",)),
        name="gqa_attn_ctx",
    )(z, z, z, qk_norm)


def _gqa_lat_kernel(prev_ref, cq_ref, ckn_ref, cvn_ref, kc_ref, vc_ref, qkn_ref, ck_ref, sk_ref, cqt_ref, sqt_ref,
                    o_ref, k_scr, v_scr, *, scale):
    del prev_ref
    rep = H_C // KVH_C

    @pl.when(pl.program_id(2) == 0)
    def _():
        k_scr[0:PAST_LEN, :] = kc_ref[...].astype(bf16)
        v_scr[0:PAST_LEN, :] = vc_ref[...].astype(bf16)
        k = _rope(_rms(ckn_ref[...], qkn_ref[1:2, :]), ck_ref[...], sk_ref[...], HD_C // 4)
        k_scr[PAST_LEN:KV_SEQ, :] = k.astype(bf16)
        v_scr[PAST_LEN:KV_SEQ, :] = cvn_ref[...].astype(bf16)

    cq, sq = cqt_ref[...], sqt_ref[...]
    k_b, v_b = k_scr[...], v_scr[...]
    for r in range(rep):
        rs = slice(r * HD_C, (r + 1) * HD_C)
        q = _rope(_rms(cq_ref[:, rs], qkn_ref[0:1, :]), cq, sq, HD_C // 4).astype(bf16)
        s = lax.dot_general(q, k_b, _TRANS_B, preferred_element_type=f32) * scale
        o_ref[:, rs] = _softmax_pv(s, v_b).astype(o_ref.dtype)


def _gqa_lat(prev, z, cache_k, cache_v, layer, qk_norm, tabs_lat, scale, *, tq=512):
    nq = DEC_SEQ // tq
    lat0 = MC // DEC_SEQ
    rep = H_C // KVH_C
    gw = rep * HD_C
    cos, sin = tabs_lat
    return pl.pallas_call(
        functools.partial(_gqa_lat_kernel, scale=scale),
        grid=(DEC_BATCH, KVH_C, nq),
        in_specs=[pl.BlockSpec(memory_space=pl.ANY),
                  pl.BlockSpec((tq, gw), lambda b, g, i: (MC // tq + b * nq + i, COL_CQ // gw + g)),
                  pl.BlockSpec((DEC_SEQ, HD_C), lambda b, g, i: (lat0 + b, COL_CK // HD_C + g)),
                  pl.BlockSpec((DEC_SEQ, HD_C), lambda b, g, i: (lat0 + b, COL_CV // HD_C + g)),
                  pl.BlockSpec((None, None, PAST_LEN, HD_C), lambda b, g, i: (b, layer, 0, g)),
                  pl.BlockSpec((None, None, PAST_LEN, HD_C), lambda b, g, i: (b, layer, 0, g)),
                  pl.BlockSpec((2, HD_C), lambda b, g, i: (0, 0)),
                  pl.BlockSpec((DEC_SEQ, LANE), lambda b, g, i: (0, 0)),
                  pl.BlockSpec((DEC_SEQ, LANE), lambda b, g, i: (0, 0)),
                  pl.BlockSpec((tq, LANE), lambda b, g, i: (i, 0)),
                  pl.BlockSpec((tq, LANE), lambda b, g, i: (i, 0))],
        out_specs=pl.BlockSpec((tq, gw), lambda b, g, i: (MC // tq + b * nq + i, g)),
        out_shape=jax.ShapeDtypeStruct((M_ROWS, H_C * HD_C), bf16),
        scratch_shapes=[pltpu.VMEM((KV_SEQ, HD_C), bf16), pltpu.VMEM((KV_SEQ, HD_C), bf16)],
        input_output_aliases={0: 0},
        compiler_params=_cparams(("parallel", "parallel", "arbitrary")),
        name="gqa_attn_lat",
    )(prev, z, z, z, cache_k, cache_v, qk_norm, cos, sin, cos, sin)


def _hgrn_levels(c):
    w, out = 1, []
    while w < c:
        out.append(w)
        w *= 2
    return out


def _hgrn_consts(c):
    p = np.arange(c)[:, None]
    r = np.arange(c)[None, :]
    sums = [r <= p]
    masks = [r == p]
    for w in _hgrn_levels(c):
        base = (p // (2 * w)) * (2 * w)
        ref = base + w - 1
        later = (p % (2 * w)) >= w
        sums.append(np.where(later, (r > ref) & (r <= p), (r > p) & (r <= ref)))
        masks.append(((r // (2 * w)) * (2 * w) == base) & later & ((r % (2 * w)) < w))
    sums.append(r > p)
    g = np.stack(sums).astype(np.float32)
    m = np.stack(masks).astype(np.float32)
    flip = lambda a: a[:, ::-1, ::-1]
    g = np.stack([g, flip(g)]).reshape(2, -1, c)
    m = np.stack([m, flip(m)]).reshape(2, -1, c)
    return jnp.asarray(np.concatenate([g] * HGRN_SPLIT, axis=-1), bf16), jnp.asarray(m, f32)


def _hgrn_kernel(*refs, layer, t_len, latent):
    if latent:
        refs = refs[1:]
        (lb_ref, on_ref, gm_ref, mk_ref, q_ref, v_ref, ff_ref, fb_ref, g_ref, s_ref,
         out_ref, of_ref, ob_ref, qd_ref, u_ref, dec_ref, st_ref) = refs
        sout_ref = None
    else:
        (lb_ref, on_ref, gm_ref, mk_ref, q_ref, v_ref, ff_ref, fb_ref, g_ref,
         out_ref, sout_ref, of_ref, ob_ref, qd_ref, u_ref, dec_ref, st_ref) = refs
        s_ref = None
    c = HGRN_C
    hp = HGRN_HP
    n_chunks = t_len // c
    levels = _hgrn_levels(c)
    n_lv = len(levels)
    trans_b = (((1,), (1,)), ((), ()))
    heads = [slice(i * LANE, (i + 1) * LANE) for i in range(hp)]

    row = lax.broadcasted_iota(jnp.int32, (c, LANE), 0)

    if layer > 0:
        ps = [lb_ref[i] for i in range(DEPTH)]
        pmax = functools.reduce(jnp.maximum, ps)
        es = [jnp.exp(p - pmax) for p in ps]
        lb = functools.reduce(lambda a, b: a + b, es[1:layer + 1]) / functools.reduce(lambda a, b: a + b, es)
        log_lb = jnp.log(lb)
        log_1m = jnp.log1p(-lb)

    def log_forget(pre, d):
        ls = jnp.minimum(pre, 0.0) - jnp.log1p(jnp.exp(-jnp.abs(pre)))
        if layer == 0:
            return ls
        a = log_lb[d:d + 1, :]
        cc = log_1m[d:d + 1, :] + ls
        return jnp.maximum(a, cc) + jnp.log1p(jnp.exp(-jnp.abs(a - cc)))

    def intra(d, cidx, pre_ref, oacc_ref):
        sl = pl.ds(pl.multiple_of(cidx * c, c), c)
        qx = q_ref[sl, :]
        q = qx * _sigmoid(qx)
        v_b = v_ref[sl, :].astype(bf16)
        logf = log_forget(pre_ref[sl, :], d)
        k = 1.0 - jnp.exp(logf)
        pieces, rem = [], logf
        for _ in range(HGRN_SPLIT):
            pieces.append(rem.astype(bf16))
            rem = rem - pieces[-1].astype(f32)
        x = jnp.exp(jnp.dot(gm_ref[d], jnp.concatenate(pieces, axis=0),
                            preferred_element_type=f32))

        qd_ref[d, sl, :] = (q * x[0:c]).astype(bf16)
        q_b = q.astype(bf16)
        k_b = k.astype(bf16)
        kd = (k * x[(n_lv + 1) * c:(n_lv + 2) * c]).astype(bf16)

        def stack(a):
            return jnp.concatenate([a[:, hs] for hs in heads], axis=0)

        def head_blocks(pp):
            return [pp[i * c:(i + 1) * c, i * c:(i + 1) * c] for i in range(hp)]

        pp = lax.dot_general(stack(q_b), stack(k_b), trans_b, preferred_element_type=f32)
        attn = [mk_ref[d, 0:c, :] * blk for blk in head_blocks(pp)]
        for li, w in enumerate(levels):
            later = ((row % (2 * w)) < w) if d == 1 else ((row % (2 * w)) >= w)
            later = jnp.concatenate([later] * hp, axis=1)
            r = stack((jnp.where(later, q, k) * x[(1 + li) * c:(2 + li) * c]).astype(bf16))
            pp = lax.dot_general(r, r, trans_b, preferred_element_type=f32)
            mask = mk_ref[d, (1 + li) * c:(2 + li) * c, :]
            attn = [a + mask * blk for a, blk in zip(attn, head_blocks(pp))]
        for hh, hs in enumerate(heads):
            oacc_ref[sl, hs] = jnp.dot(attn[hh].astype(bf16), v_b[:, hs], preferred_element_type=f32)
            u_ref[d, cidx, hh] = lax.dot_general(v_b[:, hs], kd[:, hs], (((0,), (0,)), ((), ())),
                                                 preferred_element_type=f32)
        decay = x[0:1] if d == 1 else x[c - 1:c]
        dec_ref[d, cidx] = jnp.broadcast_to(decay, (8, hp * LANE))

    def intra_body(ci, carry):
        intra(0, ci, ff_ref, of_ref)
        intra(1, ci, fb_ref, ob_ref)
        return carry

    lax.fori_loop(0, n_chunks, intra_body, 0)

    for d in range(2):
        for hh in range(hp):
            if latent:
                st_ref[d, hh] = s_ref[0, 0, d, hh].T
            else:
                st_ref[d, hh] = jnp.zeros((DV_A, DK_A), f32)

    def scan(d, cidx, oacc_ref):
        sl = pl.ds(pl.multiple_of(cidx * c, c), c)
        dec = dec_ref[d, cidx]
        for hh, hs in enumerate(heads):
            st = st_ref[d, hh]
            oacc_ref[sl, hs] = oacc_ref[sl, hs] + lax.dot_general(
                qd_ref[d, sl, hs], st.astype(bf16), trans_b, preferred_element_type=f32)
            st_ref[d, hh] = st * dec[0:1, hs] + u_ref[d, cidx, hh]

    def scan_body(ci, carry):
        scan(0, ci, of_ref)
        scan(1, n_chunks - 1 - ci, ob_ref)
        return carry

    lax.fori_loop(0, n_chunks, scan_body, 0)

    if not latent:
        for d in range(2):
            for hh in range(hp):
                sout_ref[0, d, hh] = st_ref[d, hh].T

    gx = g_ref[...]
    gate = gx * _sigmoid(gx)
    for hs in heads:
        o = of_ref[:, hs] + ob_ref[:, hs]
        out_ref[:, hs] = (_rms(o, on_ref[...]) * gate[:, hs]).astype(out_ref.dtype)


def _hgrn(z, hgrn_lb, onorm, state, consts, *, layer, latent, prev=None):
    t_len = DEC_SEQ if latent else SEQ
    n_samples = DEC_BATCH if latent else BATCH
    row0 = (MC // t_len) if latent else 0
    gmat, masks = consts
    hp = HGRN_HP
    wide = hp * LANE

    def zspec(col):
        base = col // wide
        return pl.BlockSpec((t_len, wide), lambda b, h: (row0 + b, base + h))

    in_specs = [pl.BlockSpec((DEPTH, 2, wide), lambda b, h: (0, 0, h)),
                pl.BlockSpec((1, LANE), lambda b, h: (0, 0)),
                pl.BlockSpec(gmat.shape, lambda b, h: (0, 0, 0)),
                pl.BlockSpec(masks.shape, lambda b, h: (0, 0, 0)),
                zspec(COL_AQ), zspec(COL_AV), zspec(COL_AFF), zspec(COL_AFB), zspec(COL_AG)]
    args = [hgrn_lb, onorm, gmat, masks, z, z, z, z, z]
    o_spec = pl.BlockSpec((t_len, wide), lambda b, h: (row0 + b, h))
    o_shape = jax.ShapeDtypeStruct((M_ROWS, H_A * DV_A), bf16)
    aliases = {}
    if latent:
        in_specs.insert(0, pl.BlockSpec(memory_space=pl.ANY))
        args.insert(0, prev)
        aliases = {0: 0}
        in_specs.append(pl.BlockSpec((1, 1, 2, hp, DK_A, DV_A), lambda b, h: (b, layer, 0, h, 0, 0)))
        args.append(state)
        out_specs, out_shape = o_spec, o_shape
    else:
        out_specs = [o_spec, pl.BlockSpec((1, 2, hp, DK_A, DV_A), lambda b, h: (b, 0, h, 0, 0))]
        out_shape = [o_shape, jax.ShapeDtypeStruct((BATCH, 2, H_A, DK_A, DV_A), f32)]
    n_chunks = t_len // HGRN_C
    return pl.pallas_call(
        functools.partial(_hgrn_kernel, layer=layer, t_len=t_len, latent=latent),
        grid=(n_samples, H_A // hp),
        in_specs=in_specs,
        out_specs=out_specs,
        out_shape=out_shape,
        scratch_shapes=[pltpu.VMEM((t_len, wide), f32), pltpu.VMEM((t_len, wide), f32),
                        pltpu.VMEM((2, t_len, wide), bf16),
                        pltpu.VMEM((2, n_chunks, hp, DV_A, DK_A), f32),
                        pltpu.VMEM((2, n_chunks, 8, wide), f32),
                        pltpu.VMEM((2, hp, DV_A, DK_A), f32)],
        input_output_aliases=aliases,
        compiler_params=_cparams(("parallel", "parallel")),
        name="hgrn_lat" if latent else "hgrn_ctx",
    )(*args)


def _permute_w_in(w_in):
    sizes = (1024, 1024, 1024, 1024, 1024, Q_LORA, KV_LORA, ROPE_B, 1024, 512, 512, N_BRANCH * D_MODEL)
    offs = np.concatenate([[0], np.cumsum(sizes)])
    part = lambda i: w_in[..., offs[i]:offs[i + 1]]
    (a_q, a_v, a_ff, a_fb, a_g, b_q, b_kv, b_pe, c_q, c_k, c_v, g_br) = [part(i) for i in range(12)]
    pad = jnp.zeros(w_in.shape[:-1] + (N_IN_PAD - COL_BPE - ROPE_B,), w_in.dtype)
    return jnp.concatenate([a_q, a_v, a_ff, a_fb, a_g, c_q, g_br, b_q, c_k, c_v, b_kv, b_pe, pad],
                           axis=-1).astype(bf16)


def _permute_wuq(wuq):
    w = wuq.reshape(DEPTH, Q_LORA, H_B, NOPE_B + ROPE_B)
    nope = w[..., :NOPE_B].reshape(DEPTH, Q_LORA, H_B * NOPE_B)
    pe = jnp.pad(w[..., NOPE_B:], ((0, 0), (0, 0), (0, 0), (0, LANE - ROPE_B))).reshape(DEPTH, Q_LORA, H_B * LANE)
    return jnp.concatenate([nope, pe], axis=-1).astype(bf16)


def _pad_lane(v):
    return jnp.pad(v, ((0, 0),) * (v.ndim - 1) + ((0, LANE - v.shape[-1]),))


def kernel(x_prompt, x_sample, state_hgrn, cache_mla_ckv, cache_mla_kpe, cache_gqa_k, cache_gqa_v, c, c_ctx,
           w_mod, b_mod, norm_w, ffn_w13, ffn_w2, w_in, hgrn_lb, hgrn_onorm, mla_qa_norm, mla_kva_norm,
           mla_wuq, mla_wukv, mla_nope_norm, mla_rope_norm, gqa_qk_norm, w_branch, w_out):
    w_in_b = _permute_w_in(w_in)
    wuq_b = _permute_wuq(mla_wuq)
    wukv_b = mla_wukv.astype(bf16)
    hgrn_consts = _hgrn_consts(HGRN_C)
    rope_norm_p = _pad_lane(mla_rope_norm)
    cache_kpe_p = _pad_lane(cache_mla_kpe)
    cache_k = cache_gqa_k.reshape(DEC_BATCH, DEPTH, PAST_LEN, KVH_C * HD_C)
    cache_v = cache_gqa_v.reshape(DEC_BATCH, DEPTH, PAST_LEN, KVH_C * HD_C)
    tabs_m = tuple(jnp.asarray(t) for t in _rope_tables(ROPE_B))
    tabs_g_lat = tuple(jnp.asarray(t[PAST_LEN:]) for t in _rope_tables(HD_C))
    mla_scale = (NOPE_B + ROPE_B) ** -0.5
    gqa_scale = HD_C ** -0.5

    cond = jnp.zeros((N_MOD_ROWS, D_MODEL), f32).at[0].set(c_ctx).at[1:1 + DEC_BATCH].set(c)
    x = jnp.concatenate([x_prompt.reshape(MC, D_MODEL), x_sample.reshape(ML, D_MODEL)], axis=0)

    st_h, st_ckv, st_kpe, st_k, st_v = [], [], [], [], []
    for l in range(DEPTH):
        mod = _mm(cond, [(w_mod, (l,), 0)], n=9 * D_MODEL, tm=N_MOD_ROWS, tn=1024, out_dtype=f32, name="adaln",
                  pro="silu", epi="bias", bias=b_mod[l][None, :]).reshape(N_MOD_ROWS, 9, D_MODEL)

        def ffn(xin, i, ids):
            g = _mm(xin, [(ffn_w13, (l, i), 0), (ffn_w13, (l, i), D_FF)], n=D_FF, tm=1024, tn=512,
                    out_dtype=bf16, name="ffn_up", pro="normmod", epi="swiglu",
                    norm_w=norm_w[l, 2 * i][None, :], mod=mod, shift_id=ids[0], scale_id=ids[1])
            return _mm(g, [(ffn_w2, (l, i), 0)], n=D_MODEL, tm=1024, tn=256, out_dtype=f32, name="ffn_down",
                       epi="residual", res=xin, mod=mod, gate_id=ids[2], gate_scale=0.5)

        x = ffn(x, 0, (0, 1, 2))

        z = _mm(x, [(w_in_b, (l,), 0)], n=N_IN_PAD, tm=1024, tn=1024, out_dtype=f32, name="in_proj",
                pro="normmod", norm_w=norm_w[l, 1][None, :], mod=mod, shift_id=3, scale_id=4)

        out_a, s_new = _hgrn(z, hgrn_lb, hgrn_onorm[l][None, :], None, hgrn_consts, layer=l, latent=False)
        out_a = _hgrn(z, hgrn_lb, hgrn_onorm[l][None, :], state_hgrn, hgrn_consts, layer=l, latent=True,
                      prev=out_a)
        st_h.append(s_new)

        q_mla, ckv, kpe = _mla_prep(z, mla_qa_norm[l][None, :], mla_kva_norm[l][None, :], rope_norm_p[l],
                                    mla_nope_norm[l, 0][None, :], wuq_b, l, tabs_m)
        nn_k = mla_nope_norm[l, 1][None, :]
        out_b = _mla_ctx(q_mla, ckv, kpe, wukv_b, l, nn_k, mla_scale)
        out_b = _mla_lat(out_b, q_mla, ckv, kpe, cache_mla_ckv, cache_kpe_p, wukv_b, l, nn_k, mla_scale)
        st_ckv.append(ckv[:MC].reshape(BATCH, SEQ, KV_LORA))
        st_kpe.append(kpe[:MC, :ROPE_B].reshape(BATCH, SEQ, ROPE_B))

        out_c, k_g = _gqa_ctx(z, gqa_qk_norm[l], gqa_scale)
        out_c = _gqa_lat(out_c, z, cache_k, cache_v, l, gqa_qk_norm[l], tabs_g_lat, gqa_scale)
        st_k.append(k_g.reshape(BATCH, SEQ, KVH_C, HD_C))
        st_v.append(z[:MC, COL_CV:COL_CV + KVH_C * HD_C].reshape(BATCH, SEQ, KVH_C, HD_C))

        merged = _merge(out_a, out_b, out_c, w_branch, l, z)
        x = _mm(merged, [(w_out, (l,), 0)], n=D_MODEL, tm=1024, tn=1024, out_dtype=f32, name="out_proj",
                epi="residual", res=x, mod=mod, gate_id=5, gate_scale=1.0)

        x = ffn(x, 1, (6, 7, 8))

    y_p = x[:MC].reshape(BATCH, SEQ, D_MODEL)
    y_s = x[MC:].reshape(DEC_BATCH, DEC_SEQ, D_MODEL)
    return (y_p, y_s,
            jnp.stack(st_h, axis=1), jnp.stack(st_ckv, axis=1), jnp.stack(st_kpe, axis=1),
            jnp.stack(st_k, axis=1), jnp.stack(st_v, axis=1))
```

```python
import functools

import numpy as np
import jax
import jax.numpy as jnp
from jax import lax
from jax.experimental import pallas as pl
from jax.experimental.pallas import tpu as pltpu

f32 = jnp.float32
bf16 = jnp.bfloat16

D_MODEL = 2048
BATCH = 16
SEQ = 256
DEPTH = 4
DEC_BATCH = 4
DEC_SEQ = 1024
PAST_LEN = 256
GRID_W = 64
ROPE_THETA = 10000.0
EPS = 1e-6
D_FF = 5632
N_BRANCH = 3
BR_W = 1024
H_A = 8
DK_A = 128
DV_A = 128
HK_A = H_A * DK_A
H_B = 8
Q_LORA = 512
KV_LORA = 256
NOPE_B = 128
ROPE_B = 64
V_B = 128
H_C = 8
KVH_C = 4
HD_C = 128

LANE = 128
MC = BATCH * SEQ
ML = DEC_BATCH * DEC_SEQ
M_ROWS = MC + ML
KV_SEQ = PAST_LEN + DEC_SEQ
KV_LAT_ROWS = DEC_BATCH * KV_SEQ
KV_ROWS = KV_LAT_ROWS + MC
N_MOD_ROWS = 16

COL_AQ, COL_AV, COL_AFF, COL_AFB, COL_AG = 0, 1024, 2048, 3072, 4096
COL_CQ = 5120
COL_GBR = 6144
COL_BQ = 12288
COL_CK = 12800
COL_CV = 13312
COL_BKV = 13824
COL_BPE = 14080
N_IN_PAD = 14336

ROW_TILE = 256
HGRN_C = 128
HGRN_HP = 2
HGRN_SPLIT = 2
VMEM_LIMIT = 56 * 1024 * 1024


def _cparams(sem):
    return pltpu.CompilerParams(dimension_semantics=sem, vmem_limit_bytes=VMEM_LIMIT)


def _sigmoid(x):
    return 1.0 / (1.0 + jnp.exp(-x))


def _rms(x, w, n=None):
    n = x.shape[-1] if n is None else n
    ms = jnp.sum(x * x, axis=-1, keepdims=True) * (1.0 / n)
    return x * lax.rsqrt(ms + EPS) * w


def _mod_row(i, tm):
    n_ctx = MC // tm
    per = DEC_SEQ // tm
    return jnp.where(i < n_ctx, 0, 1 + jnp.maximum(i - n_ctx, 0) // per)


def _mm_kernel(*refs, pro, epi, n_w, shift_id, scale_id, gate_id, gate_scale):
    it = iter(refs)
    x_ref = next(it)
    nw_ref = next(it) if pro in ("norm", "normmod") else None
    modk_ref = next(it) if pro == "normmod" else None
    w_refs = [next(it) for _ in range(n_w)]
    res_ref = next(it) if epi == "residual" else None
    modn_ref = next(it) if epi == "residual" else None
    bias_ref = next(it) if epi == "bias" else None
    o_ref = next(it)
    h_ref = next(it) if pro is not None else None

    if pro is not None:
        @pl.when(pl.program_id(1) == 0)
        def _():
            x = x_ref[...].astype(f32)
            if pro == "silu":
                y = x * _sigmoid(x)
            else:
                y = _rms(x, nw_ref[...])
                if pro == "normmod":
                    y = y * (1.0 + modk_ref[0, scale_id:scale_id + 1, :]) + modk_ref[0, shift_id:shift_id + 1, :]
            h_ref[...] = y.astype(bf16)
        lhs = h_ref[...]
    else:
        lhs = x_ref[...]

    accs = [jnp.dot(lhs, w[...].astype(bf16), preferred_element_type=f32) for w in w_refs]
    if epi == "swiglu":
        a, u = accs
        out = a * _sigmoid(a) * u
    elif epi == "residual":
        out = res_ref[...] + (gate_scale * modn_ref[0, gate_id:gate_id + 1, :]) * accs[0]
    elif epi == "bias":
        out = accs[0] + bias_ref[...]
    else:
        out = accs[0]
    o_ref[...] = out.astype(o_ref.dtype)


def _w_spec(w, lead, col0, k, tn):
    base = col0 // tn
    return pl.BlockSpec((None,) * len(lead) + (k, tn), lambda i, j: tuple(lead) + (0, base + j))


def _mm(x, ws, *, n, tm, tn, out_dtype, name, pro=None, epi="store", norm_w=None, mod=None,
        shift_id=0, scale_id=0, gate_id=0, gate_scale=1.0, res=None, bias=None):
    m, k = x.shape
    grid = (m // tm, n // tn)
    in_specs = [pl.BlockSpec((tm, k), lambda i, j: (i, 0))]
    args = [x]
    if pro in ("norm", "normmod"):
        in_specs.append(pl.BlockSpec((1, k), lambda i, j: (0, 0)))
        args.append(norm_w)
    if pro == "normmod":
        in_specs.append(pl.BlockSpec((1, 9, k), lambda i, j: (_mod_row(i, tm), 0, 0)))
        args.append(mod)
    for w, lead, col0 in ws:
        in_specs.append(_w_spec(w, lead, col0, k, tn))
        args.append(w)
    if epi == "residual":
        in_specs.append(pl.BlockSpec((tm, tn), lambda i, j: (i, j)))
        args.append(res)
        in_specs.append(pl.BlockSpec((1, 9, tn), lambda i, j: (_mod_row(i, tm), 0, j)))
        args.append(mod)
    if epi == "bias":
        in_specs.append(pl.BlockSpec((1, tn), lambda i, j: (0, j)))
        args.append(bias)
    scratch = [pltpu.VMEM((tm, k), bf16)] if pro is not None else []
    kern = functools.partial(_mm_kernel, pro=pro, epi=epi, n_w=len(ws), shift_id=shift_id,
                             scale_id=scale_id, gate_id=gate_id, gate_scale=gate_scale)
    return pl.pallas_call(
        kern,
        grid=grid,
        in_specs=in_specs,
        out_specs=pl.BlockSpec((tm, tn), lambda i, j: (i, j)),
        out_shape=jax.ShapeDtypeStruct((m, n), out_dtype),
        scratch_shapes=scratch,
        compiler_params=_cparams(("parallel", "arbitrary")),
        name=name,
    )(*args)


def _merge_kernel(oa_ref, ob_ref, oc_ref, wb_ref, ga_ref, gb_ref, gc_ref, o_ref):
    acc = None
    for n, (o, g) in enumerate(((oa_ref, ga_ref), (ob_ref, gb_ref), (oc_ref, gc_ref))):
        br = jnp.dot(o[...], wb_ref[n].astype(bf16), preferred_element_type=f32)
        term = _sigmoid(g[...]) * br
        acc = term if acc is None else acc + term
    o_ref[...] = acc.astype(o_ref.dtype)


def _merge(out_a, out_b, out_c, wb, layer, z, *, tm=1024, tn=512):
    grid = (M_ROWS // tm, D_MODEL // tn)
    o_spec = pl.BlockSpec((tm, BR_W), lambda i, j: (i, 0))

    def g_spec(n):
        base = (COL_GBR + n * D_MODEL) // tn
        return pl.BlockSpec((tm, tn), lambda i, j: (i, base + j))

    return pl.pallas_call(
        _merge_kernel,
        grid=grid,
        in_specs=[o_spec, o_spec, o_spec,
                  pl.BlockSpec((None, N_BRANCH, BR_W, tn), lambda i, j: (layer, 0, 0, j)),
                  g_spec(0), g_spec(1), g_spec(2)],
        out_specs=pl.BlockSpec((tm, tn), lambda i, j: (i, j)),
        out_shape=jax.ShapeDtypeStruct((M_ROWS, D_MODEL), bf16),
        compiler_params=_cparams(("parallel", "arbitrary")),
        name="merge",
    )(out_a, out_b, out_c, wb, z, z, z)


def _rope(x, cos, sin, quarter):
    lane = lax.broadcasted_iota(jnp.int32, x.shape, 1)
    first = (lane % (2 * quarter)) < quarter
    partner = jnp.where(first, -pltpu.roll(x, LANE - quarter, axis=1), pltpu.roll(x, quarter, axis=1))
    return x * cos + partner * sin


def _rope_tables(width):
    quarter = width // 4
    t = np.arange(DEC_SEQ)
    inv = ROPE_THETA ** (-np.arange(quarter, dtype=np.float32) / quarter)
    ang_r = (t // GRID_W).astype(np.float32)[:, None] * inv[None, :]
    ang_c = (t % GRID_W).astype(np.float32)[:, None] * inv[None, :]
    ang = np.concatenate([ang_r, ang_r, ang_c, ang_c], axis=1).astype(np.float32)
    cos = np.ones((KV_SEQ, LANE), np.float32)
    sin = np.zeros((KV_SEQ, LANE), np.float32)
    cos[PAST_LEN:, :width] = np.cos(ang)
    sin[PAST_LEN:, :width] = np.sin(ang)
    return cos, sin


def _q_table_block(i):
    n_ctx = MC // ROW_TILE
    per = DEC_SEQ // ROW_TILE
    return jnp.where(i < n_ctx, 0, 1 + jnp.maximum(i - n_ctx, 0) % per)


def _softmax_pv(s, v_b):
    m = jnp.max(s, axis=-1, keepdims=True)
    p = jnp.exp(s - m)
    l = jnp.sum(p, axis=-1, keepdims=True)
    return jnp.dot(p.astype(bf16), v_b, preferred_element_type=f32) / l


_TRANS_B = (((1,), (1,)), ((), ()))


def _attend(qs, ks, vs, scale):
    outs = []
    for i in range(0, len(qs), 2):
        ss = [lax.dot_general(q, k, _TRANS_B, preferred_element_type=f32) * scale
              for q, k in zip(qs[i:i + 2], ks[i:i + 2])]
        outs += [_softmax_pv(s, v) for s, v in zip(ss, vs[i:i + 2])]
    return outs


def _mla_prep_kernel(bq_ref, bkv_ref, bpe_ref, qa_ref, kva_ref, rn_ref, nn_ref, wuq_ref, cm_ref, sm_ref,
                     q_ref, ckv_ref, kpe_ref):
    cm, sm = cm_ref[...], sm_ref[...]
    ckv_ref[...] = _rms(bkv_ref[...], kva_ref[...])
    kpe_ref[...] = _rope(_rms(bpe_ref[...], rn_ref[1:2, :], n=ROPE_B), cm, sm, ROPE_B // 4)
    cq = jnp.dot(_rms(bq_ref[...], qa_ref[...]).astype(bf16), wuq_ref[...], preferred_element_type=f32)
    for h in range(H_B):
        nope = _rms(cq[:, h * LANE:(h + 1) * LANE], nn_ref[...])
        pe = _rms(cq[:, (H_B + h) * LANE:(H_B + h + 1) * LANE], rn_ref[0:1, :], n=ROPE_B)
        q_ref[:, 2 * h * LANE:(2 * h + 1) * LANE] = nope.astype(bf16)
        q_ref[:, (2 * h + 1) * LANE:(2 * h + 2) * LANE] = _rope(pe, cm, sm, ROPE_B // 4).astype(bf16)


def _mla_prep(z, qa_norm, kva_norm, rope_norm, nope_norm_q, wuq, layer, tabs_m):
    tr = ROW_TILE
    width = 2 * H_B * LANE

    def zspec(col, w):
        return pl.BlockSpec((tr, w), lambda i: (i, col // w))

    def wspec(shape):
        return pl.BlockSpec(shape, lambda i: (0, 0))

    tspec = pl.BlockSpec((tr, LANE), lambda i: (_q_table_block(i), 0))
    return pl.pallas_call(
        _mla_prep_kernel,
        grid=(M_ROWS // tr,),
        in_specs=[zspec(COL_BQ, Q_LORA), zspec(COL_BKV, KV_LORA), zspec(COL_BPE, LANE),
                  wspec((1, Q_LORA)), wspec((1, KV_LORA)), wspec((2, LANE)), wspec((1, LANE)),
                  pl.BlockSpec((None, Q_LORA, width), lambda i: (layer, 0, 0)), tspec, tspec],
        out_specs=[pl.BlockSpec((tr, width), lambda i: (i, 0)),
                   pl.BlockSpec((tr, KV_LORA), lambda i: (i, 0)),
                   pl.BlockSpec((tr, LANE), lambda i: (i, 0))],
        out_shape=[jax.ShapeDtypeStruct((M_ROWS, width), bf16),
                   jax.ShapeDtypeStruct((M_ROWS, KV_LORA), f32),
                   jax.ShapeDtypeStruct((M_ROWS, LANE), f32)],
        compiler_params=_cparams(("parallel",)),
        name="mla_prep",
    )(z, z, z, qa_norm, kva_norm, rope_norm, nope_norm_q, wuq, *tabs_m)


def _mla_kv(ckv, kpe, w, nn):
    kv = jnp.dot(ckv.astype(bf16), w, preferred_element_type=f32)
    n_h = w.shape[1] // (2 * LANE)
    kpe_b = kpe.astype(bf16)
    ks, vs = [], []
    for h in range(n_h):
        nope = _rms(kv[:, 2 * h * LANE:(2 * h + 1) * LANE], nn)
        ks.append(jnp.concatenate([nope.astype(bf16), kpe_b], axis=1))
        vs.append(kv[:, (2 * h + 1) * LANE:(2 * h + 2) * LANE].astype(bf16))
    return ks, vs


def _mla_ctx_kernel(q_ref, ckv_ref, kpe_ref, w_ref, nn_ref, o_ref, *, scale):
    ks, vs = _mla_kv(ckv_ref[...], kpe_ref[...], w_ref[...], nn_ref[...])
    for h in range(H_B):
        s = lax.dot_general(q_ref[:, 2 * h * LANE:(2 * h + 2) * LANE], ks[h], _TRANS_B,
                            preferred_element_type=f32) * scale
        o_ref[:, h * V_B:(h + 1) * V_B] = _softmax_pv(s, vs[h]).astype(o_ref.dtype)


def _mla_ctx(q_mla, ckv, kpe, wukv, layer, nope_norm_k, scale):
    width = 2 * H_B * LANE
    return pl.pallas_call(
        functools.partial(_mla_ctx_kernel, scale=scale),
        grid=(BATCH,),
        in_specs=[pl.BlockSpec((SEQ, width), lambda b: (b, 0)),
                  pl.BlockSpec((SEQ, KV_LORA), lambda b: (b, 0)),
                  pl.BlockSpec((SEQ, LANE), lambda b: (b, 0)),
                  pl.BlockSpec((None, KV_LORA, width), lambda b: (layer, 0, 0)),
                  pl.BlockSpec((1, LANE), lambda b: (0, 0))],
        out_specs=pl.BlockSpec((SEQ, H_B * V_B), lambda b: (b, 0)),
        out_shape=jax.ShapeDtypeStruct((M_ROWS, H_B * V_B), bf16),
        compiler_params=_cparams(("parallel",)),
        name="mla_attn_ctx",
    )(q_mla, ckv, kpe, wukv, nope_norm_k)


def _mla_lat_kernel(prev_ref, q_ref, ckvc_ref, kpec_ref, ckvn_ref, kpen_ref, w_ref, nn_ref, o_ref,
                    k_scr, v_scr, *, scale):
    del prev_ref

    @pl.when(pl.program_id(2) == 0)
    def _():
        ks, vs = _mla_kv(ckvc_ref[...], kpec_ref[...], w_ref[...], nn_ref[...])
        k_scr[0:PAST_LEN, :] = ks[0]
        v_scr[0:PAST_LEN, :] = vs[0]
        ks, vs = _mla_kv(ckvn_ref[...], kpen_ref[...], w_ref[...], nn_ref[...])
        k_scr[PAST_LEN:KV_SEQ, :] = ks[0]
        v_scr[PAST_LEN:KV_SEQ, :] = vs[0]

    k_b, v_b = k_scr[...], v_scr[...]
    half = q_ref.shape[0] // 2
    qs = [q_ref[0:half, :], q_ref[half:2 * half, :]]
    for i, o in enumerate(_attend(qs, [k_b, k_b], [v_b, v_b], scale)):
        o_ref[i * half:(i + 1) * half, :] = o.astype(o_ref.dtype)


def _mla_lat(prev, q_mla, ckv, kpe, cache_ckv, cache_kpe, wukv, layer, nope_norm_k, scale, *, tq=512):
    nq = DEC_SEQ // tq
    lat0 = MC // DEC_SEQ
    return pl.pallas_call(
        functools.partial(_mla_lat_kernel, scale=scale),
        grid=(DEC_BATCH, H_B, nq),
        in_specs=[pl.BlockSpec(memory_space=pl.ANY),
                  pl.BlockSpec((tq, 2 * LANE), lambda b, h, i: (MC // tq + b * nq + i, h)),
                  pl.BlockSpec((None, None, PAST_LEN, KV_LORA), lambda b, h, i: (b, layer, 0, 0)),
                  pl.BlockSpec((None, None, PAST_LEN, LANE), lambda b, h, i: (b, layer, 0, 0)),
                  pl.BlockSpec((DEC_SEQ, KV_LORA), lambda b, h, i: (lat0 + b, 0)),
                  pl.BlockSpec((DEC_SEQ, LANE), lambda b, h, i: (lat0 + b, 0)),
                  pl.BlockSpec((None, KV_LORA, 2 * LANE), lambda b, h, i: (layer, 0, h)),
                  pl.BlockSpec((1, LANE), lambda b, h, i: (0, 0))],
        out_specs=pl.BlockSpec((tq, V_B), lambda b, h, i: (MC // tq + b * nq + i, h)),
        out_shape=jax.ShapeDtypeStruct((M_ROWS, H_B * V_B), bf16),
        scratch_shapes=[pltpu.VMEM((KV_SEQ, 2 * LANE), bf16), pltpu.VMEM((KV_SEQ, V_B), bf16)],
        input_output_aliases={0: 0},
        compiler_params=_cparams(("parallel", "parallel", "arbitrary")),
        name="mla_attn_lat",
    )(prev, q_mla, cache_ckv, cache_kpe, ckv, kpe, wukv, nope_norm_k)


def _gqa_ctx_kernel(cq_ref, ck_ref, cv_ref, qkn_ref, o_ref, kg_ref, *, scale):
    rep = H_C // KVH_C
    for g in range(KVH_C):
        gs = slice(g * HD_C, (g + 1) * HD_C)
        k = _rms(ck_ref[:, gs], qkn_ref[1:2, :])
        kg_ref[:, gs] = k
        k_b = k.astype(bf16)
        v_b = cv_ref[:, gs].astype(bf16)
        for h in range(g * rep, (g + 1) * rep):
            hs = slice(h * HD_C, (h + 1) * HD_C)
            q = _rms(cq_ref[:, hs], qkn_ref[0:1, :]).astype(bf16)
            s = lax.dot_general(q, k_b, _TRANS_B, preferred_element_type=f32) * scale
            o_ref[:, hs] = _softmax_pv(s, v_b).astype(o_ref.dtype)


def _gqa_ctx(z, qk_norm, scale):
    def zspec(col, w):
        return pl.BlockSpec((SEQ, w), lambda b: (b, col // w))

    return pl.pallas_call(
        functools.partial(_gqa_ctx_kernel, scale=scale),
        grid=(BATCH,),
        in_specs=[zspec(COL_CQ, H_C * HD_C), zspec(COL_CK, KVH_C * HD_C), zspec(COL_CV, KVH_C * HD_C),
                  pl.BlockSpec((2, HD_C), lambda b: (0, 0))],
        out_specs=[pl.BlockSpec((SEQ, H_C * HD_C), lambda b: (b, 0)),
                   pl.BlockSpec((SEQ, KVH_C * HD_C), lambda b: (b, 0))],
        out_shape=[jax.ShapeDtypeStruct((M_ROWS, H_C * HD_C), bf16),
                   jax.ShapeDtypeStruct((MC, KVH_C * HD_C), f32)],
        compiler_params=_cparams(("parallel",)),
        name="gqa_attn_ctx",
    )(z, z, z, qk_norm)


def _gqa_lat_kernel(prev_ref, cq_ref, ckn_ref, cvn_ref, kc_ref, vc_ref, qkn_ref, ck_ref, sk_ref, cqt_ref, sqt_ref,
                    o_ref, k_scr, v_scr, *, scale):
    del prev_ref
    rep = H_C // KVH_C

    @pl.when(pl.program_id(2) == 0)
    def _():
        k_scr[0:PAST_LEN, :] = kc_ref[...].astype(bf16)
        v_scr[0:PAST_LEN, :] = vc_ref[...].astype(bf16)
        k = _rope(_rms(ckn_ref[...], qkn_ref[1:2, :]), ck_ref[...], sk_ref[...], HD_C // 4)
        k_scr[PAST_LEN:KV_SEQ, :] = k.astype(bf16)
        v_scr[PAST_LEN:KV_SEQ, :] = cvn_ref[...].astype(bf16)

    cq, sq = cqt_ref[...], sqt_ref[...]
    k_b, v_b = k_scr[...], v_scr[...]
    half = cq_ref.shape[0] // 2
    for r in range(rep):
        rs = slice(r * HD_C, (r + 1) * HD_C)
        q = _rope(_rms(cq_ref[:, rs], qkn_ref[0:1, :]), cq, sq, HD_C // 4).astype(bf16)
        for i, o in enumerate(_attend([q[0:half], q[half:2 * half]], [k_b, k_b], [v_b, v_b], scale)):
            o_ref[i * half:(i + 1) * half, rs] = o.astype(o_ref.dtype)


def _gqa_lat(prev, z, cache_k, cache_v, layer, qk_norm, tabs_lat, scale, *, tq=512):
    nq = DEC_SEQ // tq
    lat0 = MC // DEC_SEQ
    rep = H_C // KVH_C
    gw = rep * HD_C
    cos, sin = tabs_lat
    return pl.pallas_call(
        functools.partial(_gqa_lat_kernel, scale=scale),
        grid=(DEC_BATCH, KVH_C, nq),
        in_specs=[pl.BlockSpec(memory_space=pl.ANY),
                  pl.BlockSpec((tq, gw), lambda b, g, i: (MC // tq + b * nq + i, COL_CQ // gw + g)),
                  pl.BlockSpec((DEC_SEQ, HD_C), lambda b, g, i: (lat0 + b, COL_CK // HD_C + g)),
                  pl.BlockSpec((DEC_SEQ, HD_C), lambda b, g, i: (lat0 + b, COL_CV // HD_C + g)),
                  pl.BlockSpec((None, None, PAST_LEN, HD_C), lambda b, g, i: (b, layer, 0, g)),
                  pl.BlockSpec((None, None, PAST_LEN, HD_C), lambda b, g, i: (b, layer, 0, g)),
                  pl.BlockSpec((2, HD_C), lambda b, g, i: (0, 0)),
                  pl.BlockSpec((DEC_SEQ, LANE), lambda b, g, i: (0, 0)),
                  pl.BlockSpec((DEC_SEQ, LANE), lambda b, g, i: (0, 0)),
                  pl.BlockSpec((tq, LANE), lambda b, g, i: (i, 0)),
                  pl.BlockSpec((tq, LANE), lambda b, g, i: (i, 0))],
        out_specs=pl.BlockSpec((tq, gw), lambda b, g, i: (MC // tq + b * nq + i, g)),
        out_shape=jax.ShapeDtypeStruct((M_ROWS, H_C * HD_C), bf16),
        scratch_shapes=[pltpu.VMEM((KV_SEQ, HD_C), bf16), pltpu.VMEM((KV_SEQ, HD_C), bf16)],
        input_output_aliases={0: 0},
        compiler_params=_cparams(("parallel", "parallel", "arbitrary")),
        name="gqa_attn_lat",
    )(prev, z, z, z, cache_k, cache_v, qk_norm, cos, sin, cos, sin)


def _hgrn_levels(c):
    w, out = 1, []
    while w < c:
        out.append(w)
        w *= 2
    return out


def _hgrn_consts(c):
    p = np.arange(c)[:, None]
    r = np.arange(c)[None, :]
    sums = [r <= p]
    masks = [r == p]
    for w in _hgrn_levels(c):
        base = (p // (2 * w)) * (2 * w)
        ref = base + w - 1
        later = (p % (2 * w)) >= w
        sums.append(np.where(later, (r > ref) & (r <= p), (r > p) & (r <= ref)))
        masks.append(((r // (2 * w)) * (2 * w) == base) & later & ((r % (2 * w)) < w))
    sums.append(r > p)
    g = np.stack(sums).astype(np.float32)
    m = np.stack(masks).astype(np.float32)
    flip = lambda a: a[:, ::-1, ::-1]
    g = np.stack([g, flip(g)]).reshape(2, -1, c)
    m = np.stack([m, flip(m)]).reshape(2, -1, c)
    return jnp.asarray(np.concatenate([g] * HGRN_SPLIT, axis=-1), bf16), jnp.asarray(m, f32)


def _hgrn_kernel(*refs, layer, t_len, latent):
    if latent:
        refs = refs[1:]
        (lb_ref, on_ref, gm_ref, mk_ref, q_ref, v_ref, ff_ref, fb_ref, g_ref, s_ref,
         out_ref, of_ref, ob_ref, qd_ref, u_ref, dec_ref, st_ref) = refs
        sout_ref = None
    else:
        (lb_ref, on_ref, gm_ref, mk_ref, q_ref, v_ref, ff_ref, fb_ref, g_ref,
         out_ref, sout_ref, of_ref, ob_ref, qd_ref, u_ref, dec_ref, st_ref) = refs
        s_ref = None
    c = HGRN_C
    hp = HGRN_HP
    n_chunks = t_len // c
    levels = _hgrn_levels(c)
    n_lv = len(levels)
    trans_b = (((1,), (1,)), ((), ()))
    heads = [slice(i * LANE, (i + 1) * LANE) for i in range(hp)]

    row = lax.broadcasted_iota(jnp.int32, (c, LANE), 0)

    if layer > 0:
        ps = [lb_ref[i] for i in range(DEPTH)]
        pmax = functools.reduce(jnp.maximum, ps)
        es = [jnp.exp(p - pmax) for p in ps]
        lb = functools.reduce(lambda a, b: a + b, es[1:layer + 1]) / functools.reduce(lambda a, b: a + b, es)
        log_lb = jnp.log(lb)
        log_1m = jnp.log1p(-lb)

    def log_forget(pre, d):
        ls = jnp.minimum(pre, 0.0) - jnp.log1p(jnp.exp(-jnp.abs(pre)))
        if layer == 0:
            return ls
        a = log_lb[d:d + 1, :]
        cc = log_1m[d:d + 1, :] + ls
        return jnp.maximum(a, cc) + jnp.log1p(jnp.exp(-jnp.abs(a - cc)))

    dirs = (0, 1)
    pre_refs = (ff_ref, fb_ref)
    oacc_refs = (of_ref, ob_ref)

    def stack(a):
        return jnp.concatenate([a[:, hs] for hs in heads], axis=0)

    def head_blocks(pp):
        return [pp[i * c:(i + 1) * c, i * c:(i + 1) * c] for i in range(hp)]

    def split(a):
        pieces, rem = [], a
        for _ in range(HGRN_SPLIT):
            pieces.append(rem.astype(bf16))
            rem = rem - pieces[-1].astype(f32)
        return jnp.concatenate(pieces, axis=0)

    def intra_body(ci, carry):
        sl = pl.ds(pl.multiple_of(ci * c, c), c)
        qx = q_ref[sl, :]
        q = qx * _sigmoid(qx)
        q_b = q.astype(bf16)
        v_b = v_ref[sl, :].astype(bf16)
        logf = [log_forget(pre_refs[d][sl, :], d) for d in dirs]
        k = [1.0 - jnp.exp(lf) for lf in logf]
        x = [jnp.exp(jnp.dot(gm_ref[d], split(logf[d]), preferred_element_type=f32)) for d in dirs]
        for d in dirs:
            qd_ref[d, sl, :] = (q * x[d][0:c]).astype(bf16)
        kd = [(k[d] * x[d][(n_lv + 1) * c:(n_lv + 2) * c]).astype(bf16) for d in dirs]

        qs = stack(q_b)
        pps = [lax.dot_general(qs, stack(k[d].astype(bf16)), trans_b, preferred_element_type=f32) for d in dirs]
        attn = [[mk_ref[d, 0:c, :] * blk for blk in head_blocks(pps[d])] for d in dirs]
        for li, w in enumerate(levels):
            rs = []
            for d in dirs:
                later = ((row % (2 * w)) < w) if d == 1 else ((row % (2 * w)) >= w)
                later = jnp.concatenate([later] * hp, axis=1)
                rs.append(stack((jnp.where(later, q, k[d]) * x[d][(1 + li) * c:(2 + li) * c]).astype(bf16)))
            pps = [lax.dot_general(r, r, trans_b, preferred_element_type=f32) for r in rs]
            for d in dirs:
                mask = mk_ref[d, (1 + li) * c:(2 + li) * c, :]
                attn[d] = [a + mask * blk for a, blk in zip(attn[d], head_blocks(pps[d]))]
        for hh, hs in enumerate(heads):
            outs = [jnp.dot(attn[d][hh].astype(bf16), v_b[:, hs], preferred_element_type=f32) for d in dirs]
            us = [lax.dot_general(v_b[:, hs], kd[d][:, hs], (((0,), (0,)), ((), ())),
                                  preferred_element_type=f32) for d in dirs]
            for d in dirs:
                oacc_refs[d][sl, hs] = outs[d]
                u_ref[d, ci, hh] = us[d]
        for d in dirs:
            decay = x[d][0:1] if d == 1 else x[d][c - 1:c]
            dec_ref[d, ci] = jnp.broadcast_to(decay, (8, hp * LANE))
        return carry

    lax.fori_loop(0, n_chunks, intra_body, 0)

    for d in range(2):
        for hh in range(hp):
            if latent:
                st_ref[d, hh] = s_ref[0, 0, d, hh].T
            else:
                st_ref[d, hh] = jnp.zeros((DV_A, DK_A), f32)

    def scan(d, cidx, oacc_ref):
        sl = pl.ds(pl.multiple_of(cidx * c, c), c)
        dec = dec_ref[d, cidx]
        for hh, hs in enumerate(heads):
            st = st_ref[d, hh]
            oacc_ref[sl, hs] = oacc_ref[sl, hs] + lax.dot_general(
                qd_ref[d, sl, hs], st.astype(bf16), trans_b, preferred_element_type=f32)
            st_ref[d, hh] = st * dec[0:1, hs] + u_ref[d, cidx, hh]

    def scan_body(ci, carry):
        scan(0, ci, of_ref)
        scan(1, n_chunks - 1 - ci, ob_ref)
        return carry

    lax.fori_loop(0, n_chunks, scan_body, 0)

    if not latent:
        for d in range(2):
            for hh in range(hp):
                sout_ref[0, d, hh] = st_ref[d, hh].T

    gx = g_ref[...]
    gate = gx * _sigmoid(gx)
    for hs in heads:
        o = of_ref[:, hs] + ob_ref[:, hs]
        out_ref[:, hs] = (_rms(o, on_ref[...]) * gate[:, hs]).astype(out_ref.dtype)


def _hgrn(z, hgrn_lb, onorm, state, consts, *, layer, latent, prev=None):
    t_len = DEC_SEQ if latent else SEQ
    n_samples = DEC_BATCH if latent else BATCH
    row0 = (MC // t_len) if latent else 0
    gmat, masks = consts
    hp = HGRN_HP
    wide = hp * LANE

    def zspec(col):
        base = col // wide
        return pl.BlockSpec((t_len, wide), lambda b, h: (row0 + b, base + h))

    in_specs = [pl.BlockSpec((DEPTH, 2, wide), lambda b, h: (0, 0, h)),
                pl.BlockSpec((1, LANE), lambda b, h: (0, 0)),
                pl.BlockSpec(gmat.shape, lambda b, h: (0, 0, 0)),
                pl.BlockSpec(masks.shape, lambda b, h: (0, 0, 0)),
                zspec(COL_AQ), zspec(COL_AV), zspec(COL_AFF), zspec(COL_AFB), zspec(COL_AG)]
    args = [hgrn_lb, onorm, gmat, masks, z, z, z, z, z]
    o_spec = pl.BlockSpec((t_len, wide), lambda b, h: (row0 + b, h))
    o_shape = jax.ShapeDtypeStruct((M_ROWS, H_A * DV_A), bf16)
    aliases = {}
    if latent:
        in_specs.insert(0, pl.BlockSpec(memory_space=pl.ANY))
        args.insert(0, prev)
        aliases = {0: 0}
        in_specs.append(pl.BlockSpec((1, 1, 2, hp, DK_A, DV_A), lambda b, h: (b, layer, 0, h, 0, 0)))
        args.append(state)
        out_specs, out_shape = o_spec, o_shape
    else:
        out_specs = [o_spec, pl.BlockSpec((1, 2, hp, DK_A, DV_A), lambda b, h: (b, 0, h, 0, 0))]
        out_shape = [o_shape, jax.ShapeDtypeStruct((BATCH, 2, H_A, DK_A, DV_A), f32)]
    n_chunks = t_len // HGRN_C
    return pl.pallas_call(
        functools.partial(_hgrn_kernel, layer=layer, t_len=t_len, latent=latent),
        grid=(n_samples, H_A // hp),
        in_specs=in_specs,
        out_specs=out_specs,
        out_shape=out_shape,
        scratch_shapes=[pltpu.VMEM((t_len, wide), f32), pltpu.VMEM((t_len, wide), f32),
                        pltpu.VMEM((2, t_len, wide), bf16),
                        pltpu.VMEM((2, n_chunks, hp, DV_A, DK_A), f32),
                        pltpu.VMEM((2, n_chunks, 8, wide), f32),
                        pltpu.VMEM((2, hp, DV_A, DK_A), f32)],
        input_output_aliases=aliases,
        compiler_params=_cparams(("parallel", "parallel")),
        name="hgrn_lat" if latent else "hgrn_ctx",
    )(*args)


_W_IN_SEGMENTS = (
    (0, 40, 0, 0),
    (40, 48, 46, 64),
    (48, 96, 62, 64),
    (96, 100, 40, 0),
    (100, 108, 54, 64),
    (108, 110, 44, 0),
    (110, 111, 46, 0),
)
_W_IN_BLOCKS = N_IN_PAD // LANE
_W_IN_SRC_BLOCKS = -(-(3 * HK_A + 2 * H_A * DV_A + Q_LORA + KV_LORA + ROPE_B + H_C * HD_C + 2 * KVH_C * HD_C
                       + N_BRANCH * D_MODEL) // LANE)


def _w_in_src_block(j):
    src = jnp.zeros_like(j)
    for lo, hi, s0, _ in _W_IN_SEGMENTS:
        src = jnp.where((j >= lo) & (j < hi), s0 + (j - lo), src)
    return src


def _permute_w_in_kernel(a_ref, b_ref, o_ref):
    j = pl.program_id(1)
    for lo, hi, _, skip in _W_IN_SEGMENTS:
        @pl.when((j >= lo) & (j < hi))
        def _(lo=lo, skip=skip):
            if lo == 110:
                o_ref[...] = jnp.concatenate(
                    [a_ref[:, :ROPE_B], jnp.zeros((D_MODEL, LANE - ROPE_B), f32)], axis=1).astype(bf16)
            elif skip:
                o_ref[...] = jnp.concatenate([a_ref[:, skip:], b_ref[:, :skip]], axis=1).astype(bf16)
            else:
                o_ref[...] = a_ref[...].astype(bf16)

    @pl.when(j >= _W_IN_SEGMENTS[-1][1])
    def _():
        o_ref[...] = jnp.zeros(o_ref.shape, bf16)


def _permute_w_in(w_in):
    last = _W_IN_SRC_BLOCKS - 1
    return pl.pallas_call(
        _permute_w_in_kernel,
        grid=(DEPTH, _W_IN_BLOCKS),
        in_specs=[pl.BlockSpec((None, D_MODEL, LANE), lambda l, j: (l, 0, _w_in_src_block(j))),
                  pl.BlockSpec((None, D_MODEL, LANE), lambda l, j: (l, 0, jnp.minimum(_w_in_src_block(j) + 1, last)))],
        out_specs=pl.BlockSpec((None, D_MODEL, LANE), lambda l, j: (l, 0, j)),
        out_shape=jax.ShapeDtypeStruct((DEPTH, D_MODEL, N_IN_PAD), bf16),
        compiler_params=_cparams(("parallel", "parallel")),
        name="permute_w_in",
    )(w_in, w_in)


def _permute_wuq(wuq):
    w = wuq.reshape(DEPTH, Q_LORA, H_B, NOPE_B + ROPE_B)
    nope = w[..., :NOPE_B].reshape(DEPTH, Q_LORA, H_B * NOPE_B)
    pe = jnp.pad(w[..., NOPE_B:], ((0, 0), (0, 0), (0, 0), (0, LANE - ROPE_B))).reshape(DEPTH, Q_LORA, H_B * LANE)
    return jnp.concatenate([nope, pe], axis=-1).astype(bf16)


def _pad_lane(v):
    return jnp.pad(v, ((0, 0),) * (v.ndim - 1) + ((0, LANE - v.shape[-1]),))


def kernel(x_prompt, x_sample, state_hgrn, cache_mla_ckv, cache_mla_kpe, cache_gqa_k, cache_gqa_v, c, c_ctx,
           w_mod, b_mod, norm_w, ffn_w13, ffn_w2, w_in, hgrn_lb, hgrn_onorm, mla_qa_norm, mla_kva_norm,
           mla_wuq, mla_wukv, mla_nope_norm, mla_rope_norm, gqa_qk_norm, w_branch, w_out):
    w_in_b = _permute_w_in(w_in)
    wuq_b = _permute_wuq(mla_wuq)
    wukv_b = mla_wukv.astype(bf16)
    hgrn_consts = _hgrn_consts(HGRN_C)
    rope_norm_p = _pad_lane(mla_rope_norm)
    cache_kpe_p = _pad_lane(cache_mla_kpe)
    cache_k = cache_gqa_k.reshape(DEC_BATCH, DEPTH, PAST_LEN, KVH_C * HD_C)
    cache_v = cache_gqa_v.reshape(DEC_BATCH, DEPTH, PAST_LEN, KVH_C * HD_C)
    tabs_m = tuple(jnp.asarray(t) for t in _rope_tables(ROPE_B))
    tabs_g_lat = tuple(jnp.asarray(t[PAST_LEN:]) for t in _rope_tables(HD_C))
    mla_scale = (NOPE_B + ROPE_B) ** -0.5
    gqa_scale = HD_C ** -0.5

    cond = jnp.zeros((N_MOD_ROWS, D_MODEL), f32).at[0].set(c_ctx).at[1:1 + DEC_BATCH].set(c)
    x = jnp.concatenate([x_prompt.reshape(MC, D_MODEL), x_sample.reshape(ML, D_MODEL)], axis=0)

    st_h, st_ckv, st_kpe, st_k, st_v = [], [], [], [], []
    for l in range(DEPTH):
        mod = _mm(cond, [(w_mod, (l,), 0)], n=9 * D_MODEL, tm=N_MOD_ROWS, tn=1024, out_dtype=f32, name="adaln",
                  pro="silu", epi="bias", bias=b_mod[l][None, :]).reshape(N_MOD_ROWS, 9, D_MODEL)

        def ffn(xin, i, ids):
            g = _mm(xin, [(ffn_w13, (l, i), 0), (ffn_w13, (l, i), D_FF)], n=D_FF, tm=1024, tn=512,
                    out_dtype=bf16, name="ffn_up", pro="normmod", epi="swiglu",
                    norm_w=norm_w[l, 2 * i][None, :], mod=mod, shift_id=ids[0], scale_id=ids[1])
            return _mm(g, [(ffn_w2, (l, i), 0)], n=D_MODEL, tm=1024, tn=256, out_dtype=f32, name="ffn_down",
                       epi="residual", res=xin, mod=mod, gate_id=ids[2], gate_scale=0.5)

        x = ffn(x, 0, (0, 1, 2))

        z = _mm(x, [(w_in_b, (l,), 0)], n=N_IN_PAD, tm=1024, tn=1024, out_dtype=f32, name="in_proj",
                pro="normmod", norm_w=norm_w[l, 1][None, :], mod=mod, shift_id=3, scale_id=4)

        out_a, s_new = _hgrn(z, hgrn_lb, hgrn_onorm[l][None, :], None, hgrn_consts, layer=l, latent=False)
        out_a = _hgrn(z, hgrn_lb, hgrn_onorm[l][None, :], state_hgrn, hgrn_consts, layer=l, latent=True,
                      prev=out_a)
        st_h.append(s_new)

        q_mla, ckv, kpe = _mla_prep(z, mla_qa_norm[l][None, :], mla_kva_norm[l][None, :], rope_norm_p[l],
                                    mla_nope_norm[l, 0][None, :], wuq_b, l, tabs_m)
        nn_k = mla_nope_norm[l, 1][None, :]
        out_b = _mla_ctx(q_mla, ckv, kpe, wukv_b, l, nn_k, mla_scale)
        out_b = _mla_lat(out_b, q_mla, ckv, kpe, cache_mla_ckv, cache_kpe_p, wukv_b, l, nn_k, mla_scale)
        st_ckv.append(ckv[:MC].reshape(BATCH, SEQ, KV_LORA))
        st_kpe.append(kpe[:MC, :ROPE_B].reshape(BATCH, SEQ, ROPE_B))

        out_c, k_g = _gqa_ctx(z, gqa_qk_norm[l], gqa_scale)
        out_c = _gqa_lat(out_c, z, cache_k, cache_v, l, gqa_qk_norm[l], tabs_g_lat, gqa_scale)
        st_k.append(k_g.reshape(BATCH, SEQ, KVH_C, HD_C))
        st_v.append(z[:MC, COL_CV:COL_CV + KVH_C * HD_C].reshape(BATCH, SEQ, KVH_C, HD_C))

        merged = _merge(out_a, out_b, out_c, w_branch, l, z)
        x = _mm(merged, [(w_out, (l,), 0)], n=D_MODEL, tm=1024, tn=1024, out_dtype=f32, name="out_proj",
                epi="residual", res=x, mod=mod, gate_id=5, gate_scale=1.0)

        x = ffn(x, 1, (6, 7, 8))

    y_p = x[:MC].reshape(BATCH, SEQ, D_MODEL)
    y_s = x[MC:].reshape(DEC_BATCH, DEC_SEQ, D_MODEL)
    return (y_p, y_s,
            jnp.stack(st_h, axis=1), jnp.stack(st_ckv, axis=1), jnp.stack(st_kpe, axis=1),
            jnp.stack(st_k, axis=1), jnp.stack(st_v, axis=1))
```

```python
import functools

import numpy as np
import jax
import jax.numpy as jnp
from jax import lax
from jax.experimental import pallas as pl
from jax.experimental.pallas import tpu as pltpu

f32 = jnp.float32
bf16 = jnp.bfloat16

D_MODEL = 2048
BATCH = 16
SEQ = 256
DEPTH = 4
DEC_BATCH = 4
DEC_SEQ = 1024
PAST_LEN = 256
GRID_W = 64
ROPE_THETA = 10000.0
EPS = 1e-6
D_FF = 5632
N_BRANCH = 3
BR_W = 1024
H_A = 8
DK_A = 128
DV_A = 128
HK_A = H_A * DK_A
H_B = 8
Q_LORA = 512
KV_LORA = 256
NOPE_B = 128
ROPE_B = 64
V_B = 128
H_C = 8
KVH_C = 4
HD_C = 128

LANE = 128
MC = BATCH * SEQ
ML = DEC_BATCH * DEC_SEQ
M_ROWS = MC + ML
KV_SEQ = PAST_LEN + DEC_SEQ
KV_LAT_ROWS = DEC_BATCH * KV_SEQ
KV_ROWS = KV_LAT_ROWS + MC
N_MOD_ROWS = 16

COL_AQ, COL_AV, COL_AFF, COL_AFB, COL_AG = 0, 1024, 2048, 3072, 4096
COL_BQ = 5120
COL_BKV = 5632
COL_BPE = 5888
COL_CQ = 6144
COL_CK = 7168
COL_CV = 7680
COL_GBR = 8192
N_IN_PAD = 14336
IN_TILE = 1024
IN_SHIFT_FROM = COL_CQ // IN_TILE
IN_SHIFT = COL_CQ - (COL_BPE + ROPE_B)

ROW_TILE = 256
HGRN_C = 128
HGRN_HP = 2
HGRN_SPLIT = 2
VMEM_LIMIT = 56 * 1024 * 1024


def _cparams(sem):
    return pltpu.CompilerParams(dimension_semantics=sem, vmem_limit_bytes=VMEM_LIMIT)


def _sigmoid(x):
    return 1.0 / (1.0 + jnp.exp(-x))


def _rms(x, w, n=None):
    n = x.shape[-1] if n is None else n
    ms = jnp.sum(x * x, axis=-1, keepdims=True) * (1.0 / n)
    return x * lax.rsqrt(ms + EPS) * w


def _mod_row(i, tm):
    n_ctx = MC // tm
    per = DEC_SEQ // tm
    return jnp.where(i < n_ctx, 0, 1 + jnp.maximum(i - n_ctx, 0) // per)


def _mm_kernel(*refs, pro, epi, n_w, shift_id, scale_id, gate_id, gate_scale, w_rows):
    it = iter(refs)
    x_ref = next(it)
    nw_ref = next(it) if pro in ("norm", "normmod") else None
    modk_ref = next(it) if pro == "normmod" else None
    w_refs = [next(it) for _ in range(n_w)]
    res_ref = next(it) if epi == "residual" else None
    modn_ref = next(it) if epi == "residual" else None
    bias_ref = next(it) if epi == "bias" else None
    o_ref = next(it)
    h_ref = next(it) if pro is not None else None

    if pro is not None:
        @pl.when(pl.program_id(1) == 0)
        def _():
            x = x_ref[...].astype(f32)
            if pro == "silu":
                y = x * _sigmoid(x)
            else:
                y = _rms(x, nw_ref[...])
                if pro == "normmod":
                    y = y * (1.0 + modk_ref[0, scale_id:scale_id + 1, :]) + modk_ref[0, shift_id:shift_id + 1, :]
            h_ref[...] = y.astype(bf16)
        lhs = h_ref[...]
    else:
        lhs = x_ref[...]

    if w_rows is None:
        accs = [jnp.dot(lhs, w[...].astype(bf16), preferred_element_type=f32) for w in w_refs]
    else:
        accs = [lax.dot_general(lhs, w[(0,) * (len(w.shape) - 2)].astype(bf16), (((1,), (1,)), ((), ())),
                                preferred_element_type=f32) for w in w_refs]
    if epi == "swiglu":
        a, u = accs
        out = a * _sigmoid(a) * u
    elif epi == "residual":
        out = res_ref[...] + (gate_scale * modn_ref[0, gate_id:gate_id + 1, :]) * accs[0]
    elif epi == "bias":
        out = accs[0] + bias_ref[...]
    else:
        out = accs[0]
    o_ref[...] = out.astype(o_ref.dtype)


def _w_spec(w, lead, col0, k, tn):
    base = col0 // tn
    return pl.BlockSpec((None,) * len(lead) + (k, tn), lambda i, j: tuple(lead) + (0, base + j))


def _mm(x, ws, *, n, tm, tn, out_dtype, name, pro=None, epi="store", norm_w=None, mod=None,
        shift_id=0, scale_id=0, gate_id=0, gate_scale=1.0, res=None, bias=None, w_rows=None):
    m, k = x.shape
    grid = (m // tm, n // tn)
    in_specs = [pl.BlockSpec((tm, k), lambda i, j: (i, 0))]
    args = [x]
    if pro in ("norm", "normmod"):
        in_specs.append(pl.BlockSpec((1, k), lambda i, j: (0, 0)))
        args.append(norm_w)
    if pro == "normmod":
        in_specs.append(pl.BlockSpec((1, 9, k), lambda i, j: (_mod_row(i, tm), 0, 0)))
        args.append(mod)
    for w, lead, col0 in ws:
        if w_rows is None:
            in_specs.append(_w_spec(w, lead, col0, k, tn))
        else:
            in_specs.append(pl.BlockSpec((pl.Element(1),) * len(lead) + (pl.Element(tn), pl.Element(k)),
                                         lambda i, j, lead=lead: tuple(lead) + (w_rows(j), 0)))
        args.append(w)
    if epi == "residual":
        in_specs.append(pl.BlockSpec((tm, tn), lambda i, j: (i, j)))
        args.append(res)
        in_specs.append(pl.BlockSpec((1, 9, tn), lambda i, j: (_mod_row(i, tm), 0, j)))
        args.append(mod)
    if epi == "bias":
        in_specs.append(pl.BlockSpec((1, tn), lambda i, j: (0, j)))
        args.append(bias)
    scratch = [pltpu.VMEM((tm, k), bf16)] if pro is not None else []
    kern = functools.partial(_mm_kernel, pro=pro, epi=epi, n_w=len(ws), shift_id=shift_id,
                             scale_id=scale_id, gate_id=gate_id, gate_scale=gate_scale, w_rows=w_rows)
    return pl.pallas_call(
        kern,
        grid=grid,
        in_specs=in_specs,
        out_specs=pl.BlockSpec((tm, tn), lambda i, j: (i, j)),
        out_shape=jax.ShapeDtypeStruct((m, n), out_dtype),
        scratch_shapes=scratch,
        compiler_params=_cparams(("parallel", "arbitrary")),
        name=name,
    )(*args)


def _merge_kernel(oa_ref, ob_ref, oc_ref, wb_ref, ga_ref, gb_ref, gc_ref, o_ref):
    acc = None
    for n, (o, g) in enumerate(((oa_ref, ga_ref), (ob_ref, gb_ref), (oc_ref, gc_ref))):
        br = jnp.dot(o[...], wb_ref[n].astype(bf16), preferred_element_type=f32)
        term = _sigmoid(g[...]) * br
        acc = term if acc is None else acc + term
    o_ref[...] = acc.astype(o_ref.dtype)


def _merge(out_a, out_b, out_c, wb, layer, z, *, tm=1024, tn=512):
    grid = (M_ROWS // tm, D_MODEL // tn)
    o_spec = pl.BlockSpec((tm, BR_W), lambda i, j: (i, 0))

    def g_spec(n):
        base = (COL_GBR + n * D_MODEL) // tn
        return pl.BlockSpec((tm, tn), lambda i, j: (i, base + j))

    return pl.pallas_call(
        _merge_kernel,
        grid=grid,
        in_specs=[o_spec, o_spec, o_spec,
                  pl.BlockSpec((None, N_BRANCH, BR_W, tn), lambda i, j: (layer, 0, 0, j)),
                  g_spec(0), g_spec(1), g_spec(2)],
        out_specs=pl.BlockSpec((tm, tn), lambda i, j: (i, j)),
        out_shape=jax.ShapeDtypeStruct((M_ROWS, D_MODEL), bf16),
        compiler_params=_cparams(("parallel", "arbitrary")),
        name="merge",
    )(out_a, out_b, out_c, wb, z, z, z)


def _rope(x, cos, sin, quarter):
    lane = lax.broadcasted_iota(jnp.int32, x.shape, 1)
    first = (lane % (2 * quarter)) < quarter
    partner = jnp.where(first, -pltpu.roll(x, LANE - quarter, axis=1), pltpu.roll(x, quarter, axis=1))
    return x * cos + partner * sin


def _rope_tables(width):
    quarter = width // 4
    t = np.arange(DEC_SEQ)
    inv = ROPE_THETA ** (-np.arange(quarter, dtype=np.float32) / quarter)
    ang_r = (t // GRID_W).astype(np.float32)[:, None] * inv[None, :]
    ang_c = (t % GRID_W).astype(np.float32)[:, None] * inv[None, :]
    ang = np.concatenate([ang_r, ang_r, ang_c, ang_c], axis=1).astype(np.float32)
    cos = np.ones((KV_SEQ, LANE), np.float32)
    sin = np.zeros((KV_SEQ, LANE), np.float32)
    cos[PAST_LEN:, :width] = np.cos(ang)
    sin[PAST_LEN:, :width] = np.sin(ang)
    return cos, sin


def _q_table_block(i):
    n_ctx = MC // ROW_TILE
    per = DEC_SEQ // ROW_TILE
    return jnp.where(i < n_ctx, 0, 1 + jnp.maximum(i - n_ctx, 0) % per)


def _softmax_pv(s, v_b):
    m = jnp.max(s, axis=-1, keepdims=True)
    p = jnp.exp(s - m)
    l = jnp.sum(p, axis=-1, keepdims=True)
    return jnp.dot(p.astype(bf16), v_b, preferred_element_type=f32) / l


_TRANS_B = (((1,), (1,)), ((), ()))


def _attend(qs, ks, vs, scale):
    outs = []
    for i in range(0, len(qs), 2):
        ss = [lax.dot_general(q, k, _TRANS_B, preferred_element_type=f32) * scale
              for q, k in zip(qs[i:i + 2], ks[i:i + 2])]
        outs += [_softmax_pv(s, v) for s, v in zip(ss, vs[i:i + 2])]
    return outs


def _mla_prep_kernel(bq_ref, bkv_ref, bpe_ref, qa_ref, kva_ref, rn_ref, nn_ref, wuq_ref, cm_ref, sm_ref,
                     q_ref, ckv_ref, kpe_ref):
    cm, sm = cm_ref[...], sm_ref[...]
    ckv_ref[...] = _rms(bkv_ref[...], kva_ref[...])
    bpe = bpe_ref[...]
    bpe = jnp.where(lax.broadcasted_iota(jnp.int32, bpe.shape, 1) < ROPE_B, bpe, 0.0)
    kpe_ref[...] = _rope(_rms(bpe, rn_ref[1:2, :], n=ROPE_B), cm, sm, ROPE_B // 4)
    cq = jnp.dot(_rms(bq_ref[...], qa_ref[...]).astype(bf16), wuq_ref[...], preferred_element_type=f32)
    for h in range(H_B):
        nope = _rms(cq[:, h * LANE:(h + 1) * LANE], nn_ref[...])
        pe = _rms(cq[:, (H_B + h) * LANE:(H_B + h + 1) * LANE], rn_ref[0:1, :], n=ROPE_B)
        q_ref[:, 2 * h * LANE:(2 * h + 1) * LANE] = nope.astype(bf16)
        q_ref[:, (2 * h + 1) * LANE:(2 * h + 2) * LANE] = _rope(pe, cm, sm, ROPE_B // 4).astype(bf16)


def _mla_prep(z, qa_norm, kva_norm, rope_norm, nope_norm_q, wuq, layer, tabs_m):
    tr = ROW_TILE
    width = 2 * H_B * LANE

    def zspec(col, w):
        return pl.BlockSpec((tr, w), lambda i: (i, col // w))

    def wspec(shape):
        return pl.BlockSpec(shape, lambda i: (0, 0))

    tspec = pl.BlockSpec((tr, LANE), lambda i: (_q_table_block(i), 0))
    return pl.pallas_call(
        _mla_prep_kernel,
        grid=(M_ROWS // tr,),
        in_specs=[zspec(COL_BQ, Q_LORA), zspec(COL_BKV, KV_LORA), zspec(COL_BPE, LANE),
                  wspec((1, Q_LORA)), wspec((1, KV_LORA)), wspec((2, LANE)), wspec((1, LANE)),
                  pl.BlockSpec((None, Q_LORA, width), lambda i: (layer, 0, 0)), tspec, tspec],
        out_specs=[pl.BlockSpec((tr, width), lambda i: (i, 0)),
                   pl.BlockSpec((tr, KV_LORA), lambda i: (i, 0)),
                   pl.BlockSpec((tr, LANE), lambda i: (i, 0))],
        out_shape=[jax.ShapeDtypeStruct((M_ROWS, width), bf16),
                   jax.ShapeDtypeStruct((M_ROWS, KV_LORA), f32),
                   jax.ShapeDtypeStruct((M_ROWS, LANE), f32)],
        compiler_params=_cparams(("parallel",)),
        name="mla_prep",
    )(z, z, z, qa_norm, kva_norm, rope_norm, nope_norm_q, wuq, *tabs_m)


def _mla_kv(ckv, kpe, w, nn):
    kv = jnp.dot(ckv.astype(bf16), w, preferred_element_type=f32)
    n_h = w.shape[1] // (2 * LANE)
    kpe_b = kpe.astype(bf16)
    ks, vs = [], []
    for h in range(n_h):
        nope = _rms(kv[:, 2 * h * LANE:(2 * h + 1) * LANE], nn)
        ks.append(jnp.concatenate([nope.astype(bf16), kpe_b], axis=1))
        vs.append(kv[:, (2 * h + 1) * LANE:(2 * h + 2) * LANE].astype(bf16))
    return ks, vs


def _mla_ctx_kernel(q_ref, ckv_ref, kpe_ref, w_ref, nn_ref, o_ref, *, scale):
    ks, vs = _mla_kv(ckv_ref[...], kpe_ref[...], w_ref[...], nn_ref[...])
    for h in range(H_B):
        s = lax.dot_general(q_ref[:, 2 * h * LANE:(2 * h + 2) * LANE], ks[h], _TRANS_B,
                            preferred_element_type=f32) * scale
        o_ref[:, h * V_B:(h + 1) * V_B] = _softmax_pv(s, vs[h]).astype(o_ref.dtype)


def _mla_ctx(q_mla, ckv, kpe, wukv, layer, nope_norm_k, scale):
    width = 2 * H_B * LANE
    return pl.pallas_call(
        functools.partial(_mla_ctx_kernel, scale=scale),
        grid=(BATCH,),
        in_specs=[pl.BlockSpec((SEQ, width), lambda b: (b, 0)),
                  pl.BlockSpec((SEQ, KV_LORA), lambda b: (b, 0)),
                  pl.BlockSpec((SEQ, LANE), lambda b: (b, 0)),
                  pl.BlockSpec((None, KV_LORA, width), lambda b: (layer, 0, 0)),
                  pl.BlockSpec((1, LANE), lambda b: (0, 0))],
        out_specs=pl.BlockSpec((SEQ, H_B * V_B), lambda b: (b, 0)),
        out_shape=jax.ShapeDtypeStruct((M_ROWS, H_B * V_B), bf16),
        compiler_params=_cparams(("parallel",)),
        name="mla_attn_ctx",
    )(q_mla, ckv, kpe, wukv, nope_norm_k)


def _mla_lat_kernel(prev_ref, q_ref, ckvc_ref, kpec_ref, ckvn_ref, kpen_ref, w_ref, nn_ref, o_ref,
                    k_scr, v_scr, *, scale):
    del prev_ref

    @pl.when(pl.program_id(2) == 0)
    def _():
        ks, vs = _mla_kv(ckvc_ref[...], kpec_ref[...], w_ref[...], nn_ref[...])
        k_scr[0:PAST_LEN, :] = ks[0]
        v_scr[0:PAST_LEN, :] = vs[0]
        ks, vs = _mla_kv(ckvn_ref[...], kpen_ref[...], w_ref[...], nn_ref[...])
        k_scr[PAST_LEN:KV_SEQ, :] = ks[0]
        v_scr[PAST_LEN:KV_SEQ, :] = vs[0]

    k_b, v_b = k_scr[...], v_scr[...]
    half = q_ref.shape[0] // 2
    qs = [q_ref[0:half, :], q_ref[half:2 * half, :]]
    for i, o in enumerate(_attend(qs, [k_b, k_b], [v_b, v_b], scale)):
        o_ref[i * half:(i + 1) * half, :] = o.astype(o_ref.dtype)


def _mla_lat(prev, q_mla, ckv, kpe, cache_ckv, cache_kpe, wukv, layer, nope_norm_k, scale, *, tq=512):
    nq = DEC_SEQ // tq
    lat0 = MC // DEC_SEQ
    return pl.pallas_call(
        functools.partial(_mla_lat_kernel, scale=scale),
        grid=(DEC_BATCH, H_B, nq),
        in_specs=[pl.BlockSpec(memory_space=pl.ANY),
                  pl.BlockSpec((tq, 2 * LANE), lambda b, h, i: (MC // tq + b * nq + i, h)),
                  pl.BlockSpec((None, None, PAST_LEN, KV_LORA), lambda b, h, i: (b, layer, 0, 0)),
                  pl.BlockSpec((None, None, PAST_LEN, LANE), lambda b, h, i: (b, layer, 0, 0)),
                  pl.BlockSpec((DEC_SEQ, KV_LORA), lambda b, h, i: (lat0 + b, 0)),
                  pl.BlockSpec((DEC_SEQ, LANE), lambda b, h, i: (lat0 + b, 0)),
                  pl.BlockSpec((None, KV_LORA, 2 * LANE), lambda b, h, i: (layer, 0, h)),
                  pl.BlockSpec((1, LANE), lambda b, h, i: (0, 0))],
        out_specs=pl.BlockSpec((tq, V_B), lambda b, h, i: (MC // tq + b * nq + i, h)),
        out_shape=jax.ShapeDtypeStruct((M_ROWS, H_B * V_B), bf16),
        scratch_shapes=[pltpu.VMEM((KV_SEQ, 2 * LANE), bf16), pltpu.VMEM((KV_SEQ, V_B), bf16)],
        input_output_aliases={0: 0},
        compiler_params=_cparams(("parallel", "parallel", "arbitrary")),
        name="mla_attn_lat",
    )(prev, q_mla, cache_ckv, cache_kpe, ckv, kpe, wukv, nope_norm_k)


def _gqa_ctx_kernel(cq_ref, ck_ref, cv_ref, qkn_ref, o_ref, kg_ref, *, scale):
    rep = H_C // KVH_C
    for g in range(KVH_C):
        gs = slice(g * HD_C, (g + 1) * HD_C)
        k = _rms(ck_ref[:, gs], qkn_ref[1:2, :])
        kg_ref[:, gs] = k
        k_b = k.astype(bf16)
        v_b = cv_ref[:, gs].astype(bf16)
        for h in range(g * rep, (g + 1) * rep):
            hs = slice(h * HD_C, (h + 1) * HD_C)
            q = _rms(cq_ref[:, hs], qkn_ref[0:1, :]).astype(bf16)
            s = lax.dot_general(q, k_b, _TRANS_B, preferred_element_type=f32) * scale
            o_ref[:, hs] = _softmax_pv(s, v_b).astype(o_ref.dtype)


def _gqa_ctx(z, qk_norm, scale):
    def zspec(col, w):
        return pl.BlockSpec((SEQ, w), lambda b: (b, col // w))

    return pl.pallas_call(
        functools.partial(_gqa_ctx_kernel, scale=scale),
        grid=(BATCH,),
        in_specs=[zspec(COL_CQ, H_C * HD_C), zspec(COL_CK, KVH_C * HD_C), zspec(COL_CV, KVH_C * HD_C),
                  pl.BlockSpec((2, HD_C), lambda b: (0, 0))],
        out_specs=[pl.BlockSpec((SEQ, H_C * HD_C), lambda b: (b, 0)),
                   pl.BlockSpec((SEQ, KVH_C * HD_C), lambda b: (b, 0))],
        out_shape=[jax.ShapeDtypeStruct((M_ROWS, H_C * HD_C), bf16),
                   jax.ShapeDtypeStruct((MC, KVH_C * HD_C), f32)],
        compiler_params=_cparams(("parallel",)),
        name="gqa_attn_ctx",
    )(z, z, z, qk_norm)


def _gqa_lat_kernel(prev_ref, cq_ref, ckn_ref, cvn_ref, kc_ref, vc_ref, qkn_ref, ck_ref, sk_ref, cqt_ref, sqt_ref,
                    o_ref, k_scr, v_scr, *, scale):
    del prev_ref
    rep = H_C // KVH_C

    @pl.when(pl.program_id(2) == 0)
    def _():
        k_scr[0:PAST_LEN, :] = kc_ref[...].astype(bf16)
        v_scr[0:PAST_LEN, :] = vc_ref[...].astype(bf16)
        k = _rope(_rms(ckn_ref[...], qkn_ref[1:2, :]), ck_ref[...], sk_ref[...], HD_C // 4)
        k_scr[PAST_LEN:KV_SEQ, :] = k.astype(bf16)
        v_scr[PAST_LEN:KV_SEQ, :] = cvn_ref[...].astype(bf16)

    cq, sq = cqt_ref[...], sqt_ref[...]
    k_b, v_b = k_scr[...], v_scr[...]
    half = cq_ref.shape[0] // 2
    for r in range(rep):
        rs = slice(r * HD_C, (r + 1) * HD_C)
        q = _rope(_rms(cq_ref[:, rs], qkn_ref[0:1, :]), cq, sq, HD_C // 4).astype(bf16)
        for i, o in enumerate(_attend([q[0:half], q[half:2 * half]], [k_b, k_b], [v_b, v_b], scale)):
            o_ref[i * half:(i + 1) * half, rs] = o.astype(o_ref.dtype)


def _gqa_lat(prev, z, cache_k, cache_v, layer, qk_norm, tabs_lat, scale, *, tq=512):
    nq = DEC_SEQ // tq
    lat0 = MC // DEC_SEQ
    rep = H_C // KVH_C
    gw = rep * HD_C
    cos, sin = tabs_lat
    return pl.pallas_call(
        functools.partial(_gqa_lat_kernel, scale=scale),
        grid=(DEC_BATCH, KVH_C, nq),
        in_specs=[pl.BlockSpec(memory_space=pl.ANY),
                  pl.BlockSpec((tq, gw), lambda b, g, i: (MC // tq + b * nq + i, COL_CQ // gw + g)),
                  pl.BlockSpec((DEC_SEQ, HD_C), lambda b, g, i: (lat0 + b, COL_CK // HD_C + g)),
                  pl.BlockSpec((DEC_SEQ, HD_C), lambda b, g, i: (lat0 + b, COL_CV // HD_C + g)),
                  pl.BlockSpec((None, None, PAST_LEN, HD_C), lambda b, g, i: (b, layer, 0, g)),
                  pl.BlockSpec((None, None, PAST_LEN, HD_C), lambda b, g, i: (b, layer, 0, g)),
                  pl.BlockSpec((2, HD_C), lambda b, g, i: (0, 0)),
                  pl.BlockSpec((DEC_SEQ, LANE), lambda b, g, i: (0, 0)),
                  pl.BlockSpec((DEC_SEQ, LANE), lambda b, g, i: (0, 0)),
                  pl.BlockSpec((tq, LANE), lambda b, g, i: (i, 0)),
                  pl.BlockSpec((tq, LANE), lambda b, g, i: (i, 0))],
        out_specs=pl.BlockSpec((tq, gw), lambda b, g, i: (MC // tq + b * nq + i, g)),
        out_shape=jax.ShapeDtypeStruct((M_ROWS, H_C * HD_C), bf16),
        scratch_shapes=[pltpu.VMEM((KV_SEQ, HD_C), bf16), pltpu.VMEM((KV_SEQ, HD_C), bf16)],
        input_output_aliases={0: 0},
        compiler_params=_cparams(("parallel", "parallel", "arbitrary")),
        name="gqa_attn_lat",
    )(prev, z, z, z, cache_k, cache_v, qk_norm, cos, sin, cos, sin)


def _hgrn_levels(c):
    w, out = 1, []
    while w < c:
        out.append(w)
        w *= 2
    return out


def _hgrn_consts(c):
    p = np.arange(c)[:, None]
    r = np.arange(c)[None, :]
    sums = [r <= p]
    masks = [r == p]
    for w in _hgrn_levels(c):
        base = (p // (2 * w)) * (2 * w)
        ref = base + w - 1
        later = (p % (2 * w)) >= w
        sums.append(np.where(later, (r > ref) & (r <= p), (r > p) & (r <= ref)))
        masks.append(((r // (2 * w)) * (2 * w) == base) & later & ((r % (2 * w)) < w))
    sums.append(r > p)
    g = np.stack(sums).astype(np.float32)
    m = np.stack(masks).astype(np.float32)
    flip = lambda a: a[:, ::-1, ::-1]
    g = np.stack([g, flip(g)]).reshape(2, -1, c)
    m = np.stack([m, flip(m)]).reshape(2, -1, c)
    return jnp.asarray(np.concatenate([g] * HGRN_SPLIT, axis=-1), bf16), jnp.asarray(m, f32)


def _hgrn_kernel(*refs, layer, t_len, latent):
    if latent:
        refs = refs[1:]
        (lb_ref, on_ref, gm_ref, mk_ref, q_ref, v_ref, ff_ref, fb_ref, g_ref, s_ref,
         out_ref, of_ref, ob_ref, qd_ref, u_ref, dec_ref, st_ref) = refs
        sout_ref = None
    else:
        (lb_ref, on_ref, gm_ref, mk_ref, q_ref, v_ref, ff_ref, fb_ref, g_ref,
         out_ref, sout_ref, of_ref, ob_ref, qd_ref, u_ref, dec_ref, st_ref) = refs
        s_ref = None
    c = HGRN_C
    hp = HGRN_HP
    n_chunks = t_len // c
    levels = _hgrn_levels(c)
    n_lv = len(levels)
    trans_b = (((1,), (1,)), ((), ()))
    heads = [slice(i * LANE, (i + 1) * LANE) for i in range(hp)]

    row = lax.broadcasted_iota(jnp.int32, (c, LANE), 0)

    if layer > 0:
        ps = [lb_ref[i] for i in range(DEPTH)]
        pmax = functools.reduce(jnp.maximum, ps)
        es = [jnp.exp(p - pmax) for p in ps]
        lb = functools.reduce(lambda a, b: a + b, es[1:layer + 1]) / functools.reduce(lambda a, b: a + b, es)
        log_lb = jnp.log(lb)
        log_1m = jnp.log1p(-lb)

    def log_forget(pre, d):
        ls = jnp.minimum(pre, 0.0) - jnp.log1p(jnp.exp(-jnp.abs(pre)))
        if layer == 0:
            return ls
        a = log_lb[d:d + 1, :]
        cc = log_1m[d:d + 1, :] + ls
        return jnp.maximum(a, cc) + jnp.log1p(jnp.exp(-jnp.abs(a - cc)))

    dirs = (0, 1)
    pre_refs = (ff_ref, fb_ref)
    oacc_refs = (of_ref, ob_ref)

    def stack(a):
        return jnp.concatenate([a[:, hs] for hs in heads], axis=0)

    def head_blocks(pp):
        return [pp[i * c:(i + 1) * c, i * c:(i + 1) * c] for i in range(hp)]

    def split(a):
        pieces, rem = [], a
        for _ in range(HGRN_SPLIT):
            pieces.append(rem.astype(bf16))
            rem = rem - pieces[-1].astype(f32)
        return jnp.concatenate(pieces, axis=0)

    def intra_body(ci, carry):
        sl = pl.ds(pl.multiple_of(ci * c, c), c)
        qx = q_ref[sl, :]
        q = qx * _sigmoid(qx)
        q_b = q.astype(bf16)
        v_b = v_ref[sl, :].astype(bf16)
        logf = [log_forget(pre_refs[d][sl, :], d) for d in dirs]
        k = [1.0 - jnp.exp(lf) for lf in logf]
        x = [jnp.exp(jnp.dot(gm_ref[d], split(logf[d]), preferred_element_type=f32)) for d in dirs]
        for d in dirs:
            qd_ref[d, sl, :] = (q * x[d][0:c]).astype(bf16)
        kd = [(k[d] * x[d][(n_lv + 1) * c:(n_lv + 2) * c]).astype(bf16) for d in dirs]

        qs = stack(q_b)
        pps = [lax.dot_general(qs, stack(k[d].astype(bf16)), trans_b, preferred_element_type=f32) for d in dirs]
        attn = [[mk_ref[d, 0:c, :] * blk for blk in head_blocks(pps[d])] for d in dirs]
        for li, w in enumerate(levels):
            rs = []
            for d in dirs:
                later = ((row % (2 * w)) < w) if d == 1 else ((row % (2 * w)) >= w)
                later = jnp.concatenate([later] * hp, axis=1)
                rs.append(stack((jnp.where(later, q, k[d]) * x[d][(1 + li) * c:(2 + li) * c]).astype(bf16)))
            pps = [lax.dot_general(r, r, trans_b, preferred_element_type=f32) for r in rs]
            for d in dirs:
                mask = mk_ref[d, (1 + li) * c:(2 + li) * c, :]
                attn[d] = [a + mask * blk for a, blk in zip(attn[d], head_blocks(pps[d]))]
        for hh, hs in enumerate(heads):
            outs = [jnp.dot(attn[d][hh].astype(bf16), v_b[:, hs], preferred_element_type=f32) for d in dirs]
            us = [lax.dot_general(v_b[:, hs], kd[d][:, hs], (((0,), (0,)), ((), ())),
                                  preferred_element_type=f32) for d in dirs]
            for d in dirs:
                oacc_refs[d][sl, hs] = outs[d]
                u_ref[d, ci, hh] = us[d]
        for d in dirs:
            decay = x[d][0:1] if d == 1 else x[d][c - 1:c]
            dec_ref[d, ci] = jnp.broadcast_to(decay, (8, hp * LANE))
        return carry

    lax.fori_loop(0, n_chunks, intra_body, 0)

    for d in range(2):
        for hh in range(hp):
            if latent:
                st_ref[d, hh] = s_ref[0, 0, d, hh].T
            else:
                st_ref[d, hh] = jnp.zeros((DV_A, DK_A), f32)

    def scan(d, cidx, oacc_ref):
        sl = pl.ds(pl.multiple_of(cidx * c, c), c)
        dec = dec_ref[d, cidx]
        for hh, hs in enumerate(heads):
            st = st_ref[d, hh]
            oacc_ref[sl, hs] = oacc_ref[sl, hs] + lax.dot_general(
                qd_ref[d, sl, hs], st.astype(bf16), trans_b, preferred_element_type=f32)
            st_ref[d, hh] = st * dec[0:1, hs] + u_ref[d, cidx, hh]

    def scan_body(ci, carry):
        scan(0, ci, of_ref)
        scan(1, n_chunks - 1 - ci, ob_ref)
        return carry

    lax.fori_loop(0, n_chunks, scan_body, 0)

    if not latent:
        for d in range(2):
            for hh in range(hp):
                sout_ref[0, d, hh] = st_ref[d, hh].T

    gx = g_ref[...]
    gate = gx * _sigmoid(gx)
    for hs in heads:
        o = of_ref[:, hs] + ob_ref[:, hs]
        out_ref[:, hs] = (_rms(o, on_ref[...]) * gate[:, hs]).astype(out_ref.dtype)


def _hgrn(z, hgrn_lb, onorm, state, consts, *, layer, latent, prev=None):
    t_len = DEC_SEQ if latent else SEQ
    n_samples = DEC_BATCH if latent else BATCH
    row0 = (MC // t_len) if latent else 0
    gmat, masks = consts
    hp = HGRN_HP
    wide = hp * LANE

    def zspec(col):
        base = col // wide
        return pl.BlockSpec((t_len, wide), lambda b, h: (row0 + b, base + h))

    in_specs = [pl.BlockSpec((DEPTH, 2, wide), lambda b, h: (0, 0, h)),
                pl.BlockSpec((1, LANE), lambda b, h: (0, 0)),
                pl.BlockSpec(gmat.shape, lambda b, h: (0, 0, 0)),
                pl.BlockSpec(masks.shape, lambda b, h: (0, 0, 0)),
                zspec(COL_AQ), zspec(COL_AV), zspec(COL_AFF), zspec(COL_AFB), zspec(COL_AG)]
    args = [hgrn_lb, onorm, gmat, masks, z, z, z, z, z]
    o_spec = pl.BlockSpec((t_len, wide), lambda b, h: (row0 + b, h))
    o_shape = jax.ShapeDtypeStruct((M_ROWS, H_A * DV_A), bf16)
    aliases = {}
    if latent:
        in_specs.insert(0, pl.BlockSpec(memory_space=pl.ANY))
        args.insert(0, prev)
        aliases = {0: 0}
        in_specs.append(pl.BlockSpec((1, 1, 2, hp, DK_A, DV_A), lambda b, h: (b, layer, 0, h, 0, 0)))
        args.append(state)
        out_specs, out_shape = o_spec, o_shape
    else:
        out_specs = [o_spec, pl.BlockSpec((1, 2, hp, DK_A, DV_A), lambda b, h: (b, 0, h, 0, 0))]
        out_shape = [o_shape, jax.ShapeDtypeStruct((BATCH, 2, H_A, DK_A, DV_A), f32)]
    n_chunks = t_len // HGRN_C
    return pl.pallas_call(
        functools.partial(_hgrn_kernel, layer=layer, t_len=t_len, latent=latent),
        grid=(n_samples, H_A // hp),
        in_specs=in_specs,
        out_specs=out_specs,
        out_shape=out_shape,
        scratch_shapes=[pltpu.VMEM((t_len, wide), f32), pltpu.VMEM((t_len, wide), f32),
                        pltpu.VMEM((2, t_len, wide), bf16),
                        pltpu.VMEM((2, n_chunks, hp, DV_A, DK_A), f32),
                        pltpu.VMEM((2, n_chunks, 8, wide), f32),
                        pltpu.VMEM((2, hp, DV_A, DK_A), f32)],
        input_output_aliases=aliases,
        compiler_params=_cparams(("parallel", "parallel")),
        name="hgrn_lat" if latent else "hgrn_ctx",
    )(*args)


def _permute_wuq(wuq):
    w = wuq.reshape(DEPTH, Q_LORA, H_B, NOPE_B + ROPE_B)
    nope = w[..., :NOPE_B].reshape(DEPTH, Q_LORA, H_B * NOPE_B)
    pe = jnp.pad(w[..., NOPE_B:], ((0, 0), (0, 0), (0, 0), (0, LANE - ROPE_B))).reshape(DEPTH, Q_LORA, H_B * LANE)
    return jnp.concatenate([nope, pe], axis=-1).astype(bf16)


def _pad_lane(v):
    return jnp.pad(v, ((0, 0),) * (v.ndim - 1) + ((0, LANE - v.shape[-1]),))


def kernel(x_prompt, x_sample, state_hgrn, cache_mla_ckv, cache_mla_kpe, cache_gqa_k, cache_gqa_v, c, c_ctx,
           w_mod, b_mod, norm_w, ffn_w13, ffn_w2, w_in, hgrn_lb, hgrn_onorm, mla_qa_norm, mla_kva_norm,
           mla_wuq, mla_wukv, mla_nope_norm, mla_rope_norm, gqa_qk_norm, w_branch, w_out):
    w_in_t = jnp.swapaxes(w_in, 1, 2)
    wuq_b = _permute_wuq(mla_wuq)
    wukv_b = mla_wukv.astype(bf16)
    hgrn_consts = _hgrn_consts(HGRN_C)
    rope_norm_p = _pad_lane(mla_rope_norm)
    cache_kpe_p = _pad_lane(cache_mla_kpe)
    cache_k = cache_gqa_k.reshape(DEC_BATCH, DEPTH, PAST_LEN, KVH_C * HD_C)
    cache_v = cache_gqa_v.reshape(DEC_BATCH, DEPTH, PAST_LEN, KVH_C * HD_C)
    tabs_m = tuple(jnp.asarray(t) for t in _rope_tables(ROPE_B))
    tabs_g_lat = tuple(jnp.asarray(t[PAST_LEN:]) for t in _rope_tables(HD_C))
    mla_scale = (NOPE_B + ROPE_B) ** -0.5
    gqa_scale = HD_C ** -0.5

    cond = jnp.zeros((N_MOD_ROWS, D_MODEL), f32).at[0].set(c_ctx).at[1:1 + DEC_BATCH].set(c)
    x = jnp.concatenate([x_prompt.reshape(MC, D_MODEL), x_sample.reshape(ML, D_MODEL)], axis=0)

    st_h, st_ckv, st_kpe, st_k, st_v = [], [], [], [], []
    for l in range(DEPTH):
        mod = _mm(cond, [(w_mod, (l,), 0)], n=9 * D_MODEL, tm=N_MOD_ROWS, tn=1024, out_dtype=f32, name="adaln",
                  pro="silu", epi="bias", bias=b_mod[l][None, :]).reshape(N_MOD_ROWS, 9, D_MODEL)

        def ffn(xin, i, ids):
            g = _mm(xin, [(ffn_w13, (l, i), 0), (ffn_w13, (l, i), D_FF)], n=D_FF, tm=1024, tn=512,
                    out_dtype=bf16, name="ffn_up", pro="normmod", epi="swiglu",
                    norm_w=norm_w[l, 2 * i][None, :], mod=mod, shift_id=ids[0], scale_id=ids[1])
            return _mm(g, [(ffn_w2, (l, i), 0)], n=D_MODEL, tm=1024, tn=256, out_dtype=f32, name="ffn_down",
                       epi="residual", res=xin, mod=mod, gate_id=ids[2], gate_scale=0.5)

        x = ffn(x, 0, (0, 1, 2))

        z = _mm(x, [(w_in_t, (l,), 0)], n=N_IN_PAD, tm=1024, tn=IN_TILE, out_dtype=f32, name="in_proj",
                pro="normmod", norm_w=norm_w[l, 1][None, :], mod=mod, shift_id=3, scale_id=4,
                w_rows=lambda j: (j * (IN_TILE // ROPE_B)
                                  - jnp.where(j >= IN_SHIFT_FROM, IN_SHIFT // ROPE_B, 0)) * ROPE_B)

        out_a, s_new = _hgrn(z, hgrn_lb, hgrn_onorm[l][None, :], None, hgrn_consts, layer=l, latent=False)
        out_a = _hgrn(z, hgrn_lb, hgrn_onorm[l][None, :], state_hgrn, hgrn_consts, layer=l, latent=True,
                      prev=out_a)
        st_h.append(s_new)

        q_mla, ckv, kpe = _mla_prep(z, mla_qa_norm[l][None, :], mla_kva_norm[l][None, :], rope_norm_p[l],
                                    mla_nope_norm[l, 0][None, :], wuq_b, l, tabs_m)
        nn_k = mla_nope_norm[l, 1][None, :]
        out_b = _mla_ctx(q_mla, ckv, kpe, wukv_b, l, nn_k, mla_scale)
        out_b = _mla_lat(out_b, q_mla, ckv, kpe, cache_mla_ckv, cache_kpe_p, wukv_b, l, nn_k, mla_scale)
        st_ckv.append(ckv[:MC].reshape(BATCH, SEQ, KV_LORA))
        st_kpe.append(kpe[:MC, :ROPE_B].reshape(BATCH, SEQ, ROPE_B))

        out_c, k_g = _gqa_ctx(z, gqa_qk_norm[l], gqa_scale)
        out_c = _gqa_lat(out_c, z, cache_k, cache_v, l, gqa_qk_norm[l], tabs_g_lat, gqa_scale)
        st_k.append(k_g.reshape(BATCH, SEQ, KVH_C, HD_C))
        st_v.append(z[:MC, COL_CV:COL_CV + KVH_C * HD_C].reshape(BATCH, SEQ, KVH_C, HD_C))

        merged = _merge(out_a, out_b, out_c, w_branch, l, z)
        x = _mm(merged, [(w_out, (l,), 0)], n=D_MODEL, tm=1024, tn=1024, out_dtype=f32, name="out_proj",
                epi="residual", res=x, mod=mod, gate_id=5, gate_scale=1.0)

        x = ffn(x, 1, (6, 7, 8))

    y_p = x[:MC].reshape(BATCH, SEQ, D_MODEL)
    y_s = x[MC:].reshape(DEC_BATCH, DEC_SEQ, D_MODEL)
    return (y_p, y_s,
            jnp.stack(st_h, axis=1), jnp.stack(st_ckv, axis=1), jnp.stack(st_kpe, axis=1),
            jnp.stack(st_k, axis=1), jnp.stack(st_v, axis=1))
```

```python
import functools

import numpy as np
import jax
import jax.numpy as jnp
from jax import lax
from jax.experimental import pallas as pl
from jax.experimental.pallas import tpu as pltpu

f32 = jnp.float32
bf16 = jnp.bfloat16

D_MODEL = 2048
BATCH = 16
SEQ = 256
DEPTH = 4
DEC_BATCH = 4
DEC_SEQ = 1024
PAST_LEN = 256
GRID_W = 64
ROPE_THETA = 10000.0
EPS = 1e-6
D_FF = 5632
N_BRANCH = 3
BR_W = 1024
H_A = 8
DK_A = 128
DV_A = 128
HK_A = H_A * DK_A
H_B = 8
Q_LORA = 512
KV_LORA = 256
NOPE_B = 128
ROPE_B = 64
V_B = 128
H_C = 8
KVH_C = 4
HD_C = 128

LANE = 128
MC = BATCH * SEQ
ML = DEC_BATCH * DEC_SEQ
M_ROWS = MC + ML
KV_SEQ = PAST_LEN + DEC_SEQ
KV_LAT_ROWS = DEC_BATCH * KV_SEQ
KV_ROWS = KV_LAT_ROWS + MC
N_MOD_ROWS = 16

COL_AQ, COL_AV, COL_AFF, COL_AFB, COL_AG = 0, 1024, 2048, 3072, 4096
COL_BQ = 5120
COL_BKV = 5632
COL_BPE = 5888
COL_CQ = 6144
COL_CK = 7168
COL_CV = 7680
COL_GBR = 8192
N_IN_PAD = 14336
IN_TILE = 1024
IN_SHIFT_FROM = COL_CQ // IN_TILE
IN_SHIFT = COL_CQ - (COL_BPE + ROPE_B)

ROW_TILE = 256
HGRN_C = 128
HGRN_HP = 2
HGRN_SPLIT = 2
VMEM_LIMIT = 56 * 1024 * 1024


def _cparams(sem):
    return pltpu.CompilerParams(dimension_semantics=sem, vmem_limit_bytes=VMEM_LIMIT)


def _sigmoid(x):
    return 1.0 / (1.0 + jnp.exp(-x))


def _rms(x, w, n=None):
    n = x.shape[-1] if n is None else n
    ms = jnp.sum(x * x, axis=-1, keepdims=True) * (1.0 / n)
    return x * lax.rsqrt(ms + EPS) * w


def _mod_row(i, tm):
    n_ctx = MC // tm
    per = DEC_SEQ // tm
    return jnp.where(i < n_ctx, 0, 1 + jnp.maximum(i - n_ctx, 0) // per)


def _mm_kernel(*refs, pro, epi, n_w, shift_id, scale_id, gate_id, gate_scale, w_rows):
    it = iter(refs)
    x_ref = next(it)
    nw_ref = next(it) if pro in ("norm", "normmod") else None
    modk_ref = next(it) if pro == "normmod" else None
    w_refs = [next(it) for _ in range(n_w)]
    res_ref = next(it) if epi == "residual" else None
    modn_ref = next(it) if epi == "residual" else None
    bias_ref = next(it) if epi == "bias" else None
    o_ref = next(it)
    h_ref = next(it) if pro is not None else None

    if pro is not None:
        @pl.when(pl.program_id(1) == 0)
        def _():
            x = x_ref[...].astype(f32)
            if pro == "silu":
                y = x * _sigmoid(x)
            else:
                y = _rms(x, nw_ref[...])
                if pro == "normmod":
                    y = y * (1.0 + modk_ref[0, scale_id:scale_id + 1, :]) + modk_ref[0, shift_id:shift_id + 1, :]
            h_ref[...] = y.astype(bf16)
        lhs = h_ref[...]
    else:
        lhs = x_ref[...]

    if w_rows is None:
        accs = [jnp.dot(lhs, w[...].astype(bf16), preferred_element_type=f32) for w in w_refs]
    else:
        accs = [lax.dot_general(lhs, w[(0,) * (len(w.shape) - 2)].astype(bf16), (((1,), (1,)), ((), ())),
                                preferred_element_type=f32) for w in w_refs]
    if epi == "swiglu":
        a, u = accs
        out = a * _sigmoid(a) * u
    elif epi == "residual":
        out = res_ref[...] + (gate_scale * modn_ref[0, gate_id:gate_id + 1, :]) * accs[0]
    elif epi == "bias":
        out = accs[0] + bias_ref[...]
    else:
        out = accs[0]
    o_ref[...] = out.astype(o_ref.dtype)


def _w_spec(w, lead, col0, k, tn):
    base = col0 // tn
    return pl.BlockSpec((None,) * len(lead) + (k, tn), lambda i, j: tuple(lead) + (0, base + j))


def _mm(x, ws, *, n, tm, tn, out_dtype, name, pro=None, epi="store", norm_w=None, mod=None,
        shift_id=0, scale_id=0, gate_id=0, gate_scale=1.0, res=None, bias=None, w_rows=None):
    m, k = x.shape
    grid = (m // tm, n // tn)
    in_specs = [pl.BlockSpec((tm, k), lambda i, j: (i, 0))]
    args = [x]
    if pro in ("norm", "normmod"):
        in_specs.append(pl.BlockSpec((1, k), lambda i, j: (0, 0)))
        args.append(norm_w)
    if pro == "normmod":
        in_specs.append(pl.BlockSpec((1, 9, k), lambda i, j: (_mod_row(i, tm), 0, 0)))
        args.append(mod)
    for w, lead, col0 in ws:
        if w_rows is None:
            in_specs.append(_w_spec(w, lead, col0, k, tn))
        else:
            in_specs.append(pl.BlockSpec((pl.Element(1),) * len(lead) + (pl.Element(tn), pl.Element(k)),
                                         lambda i, j, lead=lead: tuple(lead) + (w_rows(j), 0)))
        args.append(w)
    if epi == "residual":
        in_specs.append(pl.BlockSpec((tm, tn), lambda i, j: (i, j)))
        args.append(res)
        in_specs.append(pl.BlockSpec((1, 9, tn), lambda i, j: (_mod_row(i, tm), 0, j)))
        args.append(mod)
    if epi == "bias":
        in_specs.append(pl.BlockSpec((1, tn), lambda i, j: (0, j)))
        args.append(bias)
    scratch = [pltpu.VMEM((tm, k), bf16)] if pro is not None else []
    kern = functools.partial(_mm_kernel, pro=pro, epi=epi, n_w=len(ws), shift_id=shift_id,
                             scale_id=scale_id, gate_id=gate_id, gate_scale=gate_scale, w_rows=w_rows)
    return pl.pallas_call(
        kern,
        grid=grid,
        in_specs=in_specs,
        out_specs=pl.BlockSpec((tm, tn), lambda i, j: (i, j)),
        out_shape=jax.ShapeDtypeStruct((m, n), out_dtype),
        scratch_shapes=scratch,
        compiler_params=_cparams(("parallel", "arbitrary")),
        name=name,
    )(*args)


def _merge_kernel(oa_ref, ob_ref, oc_ref, wb_ref, ga_ref, gb_ref, gc_ref, o_ref):
    acc = None
    for n, (o, g) in enumerate(((oa_ref, ga_ref), (ob_ref, gb_ref), (oc_ref, gc_ref))):
        br = jnp.dot(o[...], wb_ref[n].astype(bf16), preferred_element_type=f32)
        term = _sigmoid(g[...]) * br
        acc = term if acc is None else acc + term
    o_ref[...] = acc.astype(o_ref.dtype)


def _merge(out_a, out_b, out_c, wb, layer, z, *, tm=2048, tn=256):
    grid = (M_ROWS // tm, D_MODEL // tn)
    o_spec = pl.BlockSpec((tm, BR_W), lambda i, j: (i, 0))

    def g_spec(n):
        base = (COL_GBR + n * D_MODEL) // tn
        return pl.BlockSpec((tm, tn), lambda i, j: (i, base + j))

    return pl.pallas_call(
        _merge_kernel,
        grid=grid,
        in_specs=[o_spec, o_spec, o_spec,
                  pl.BlockSpec((None, N_BRANCH, BR_W, tn), lambda i, j: (layer, 0, 0, j)),
                  g_spec(0), g_spec(1), g_spec(2)],
        out_specs=pl.BlockSpec((tm, tn), lambda i, j: (i, j)),
        out_shape=jax.ShapeDtypeStruct((M_ROWS, D_MODEL), bf16),
        compiler_params=_cparams(("parallel", "arbitrary")),
        name="merge",
    )(out_a, out_b, out_c, wb, z, z, z)


def _rope(x, cos, sin, quarter):
    lane = lax.broadcasted_iota(jnp.int32, x.shape, 1)
    first = (lane % (2 * quarter)) < quarter
    partner = jnp.where(first, -pltpu.roll(x, LANE - quarter, axis=1), pltpu.roll(x, quarter, axis=1))
    return x * cos + partner * sin


def _rope_tables(width):
    quarter = width // 4
    t = np.arange(DEC_SEQ)
    inv = ROPE_THETA ** (-np.arange(quarter, dtype=np.float32) / quarter)
    ang_r = (t // GRID_W).astype(np.float32)[:, None] * inv[None, :]
    ang_c = (t % GRID_W).astype(np.float32)[:, None] * inv[None, :]
    ang = np.concatenate([ang_r, ang_r, ang_c, ang_c], axis=1).astype(np.float32)
    cos = np.ones((KV_SEQ, LANE), np.float32)
    sin = np.zeros((KV_SEQ, LANE), np.float32)
    cos[PAST_LEN:, :width] = np.cos(ang)
    sin[PAST_LEN:, :width] = np.sin(ang)
    return cos, sin


def _q_table_block(i):
    n_ctx = MC // ROW_TILE
    per = DEC_SEQ // ROW_TILE
    return jnp.where(i < n_ctx, 0, 1 + jnp.maximum(i - n_ctx, 0) % per)


LOG2_E = 1.4426950408889634


def _softmax_pv(s2, v_b):
    m = jnp.max(s2, axis=-1, keepdims=True)
    p = jnp.exp2(s2 - m)
    l = jnp.sum(p, axis=-1, keepdims=True)
    return jnp.dot(p.astype(bf16), v_b, preferred_element_type=f32) / l


_TRANS_B = (((1,), (1,)), ((), ()))


def _attend(qs, ks, vs):
    outs = []
    for i in range(0, len(qs), 2):
        ss = [lax.dot_general(q, k, _TRANS_B, preferred_element_type=f32)
              for q, k in zip(qs[i:i + 2], ks[i:i + 2])]
        outs += [_softmax_pv(s, v) for s, v in zip(ss, vs[i:i + 2])]
    return outs


def _mla_prep_kernel(bq_ref, bkv_ref, bpe_ref, qa_ref, kva_ref, rn_ref, nn_ref, wuq_ref, cm_ref, sm_ref,
                     q_ref, ckv_ref, kpe_ref):
    cm, sm = cm_ref[...], sm_ref[...]
    ckv_ref[...] = _rms(bkv_ref[...], kva_ref[...])
    bpe = bpe_ref[...]
    bpe = jnp.where(lax.broadcasted_iota(jnp.int32, bpe.shape, 1) < ROPE_B, bpe, 0.0)
    kpe_ref[...] = _rope(_rms(bpe, rn_ref[1:2, :], n=ROPE_B), cm, sm, ROPE_B // 4)
    cq = jnp.dot(_rms(bq_ref[...], qa_ref[...]).astype(bf16), wuq_ref[...], preferred_element_type=f32)
    for h in range(H_B):
        nope = _rms(cq[:, h * LANE:(h + 1) * LANE], nn_ref[...])
        pe = _rms(cq[:, (H_B + h) * LANE:(H_B + h + 1) * LANE], rn_ref[0:1, :], n=ROPE_B)
        q_ref[:, 2 * h * LANE:(2 * h + 1) * LANE] = nope.astype(bf16)
        q_ref[:, (2 * h + 1) * LANE:(2 * h + 2) * LANE] = _rope(pe, cm, sm, ROPE_B // 4).astype(bf16)


def _mla_prep(z, qa_norm, kva_norm, rope_norm, nope_norm_q, wuq, layer, tabs_m):
    tr = ROW_TILE
    width = 2 * H_B * LANE

    def zspec(col, w):
        return pl.BlockSpec((tr, w), lambda i: (i, col // w))

    def wspec(shape):
        return pl.BlockSpec(shape, lambda i: (0, 0))

    tspec = pl.BlockSpec((tr, LANE), lambda i: (_q_table_block(i), 0))
    return pl.pallas_call(
        _mla_prep_kernel,
        grid=(M_ROWS // tr,),
        in_specs=[zspec(COL_BQ, Q_LORA), zspec(COL_BKV, KV_LORA), zspec(COL_BPE, LANE),
                  wspec((1, Q_LORA)), wspec((1, KV_LORA)), wspec((2, LANE)), wspec((1, LANE)),
                  pl.BlockSpec((None, Q_LORA, width), lambda i: (layer, 0, 0)), tspec, tspec],
        out_specs=[pl.BlockSpec((tr, width), lambda i: (i, 0)),
                   pl.BlockSpec((tr, KV_LORA), lambda i: (i, 0)),
                   pl.BlockSpec((tr, LANE), lambda i: (i, 0))],
        out_shape=[jax.ShapeDtypeStruct((M_ROWS, width), bf16),
                   jax.ShapeDtypeStruct((M_ROWS, KV_LORA), f32),
                   jax.ShapeDtypeStruct((M_ROWS, LANE), f32)],
        compiler_params=_cparams(("parallel",)),
        name="mla_prep",
    )(z, z, z, qa_norm, kva_norm, rope_norm, nope_norm_q, wuq, *tabs_m)


def _mla_kv(ckv, kpe, w, nn, kscale):
    kv = jnp.dot(ckv.astype(bf16), w, preferred_element_type=f32)
    n_h = w.shape[1] // (2 * LANE)
    kpe_b = (kpe * kscale).astype(bf16)
    ks, vs = [], []
    for h in range(n_h):
        nope = _rms(kv[:, 2 * h * LANE:(2 * h + 1) * LANE], nn) * kscale
        ks.append(jnp.concatenate([nope.astype(bf16), kpe_b], axis=1))
        vs.append(kv[:, (2 * h + 1) * LANE:(2 * h + 2) * LANE].astype(bf16))
    return ks, vs


def _mla_ctx_kernel(q_ref, ckv_ref, kpe_ref, w_ref, nn_ref, o_ref, *, scale):
    ks, vs = _mla_kv(ckv_ref[...], kpe_ref[...], w_ref[...], nn_ref[...], scale * LOG2_E)
    for h in range(H_B):
        s = lax.dot_general(q_ref[:, 2 * h * LANE:(2 * h + 2) * LANE], ks[h], _TRANS_B,
                            preferred_element_type=f32)
        o_ref[:, h * V_B:(h + 1) * V_B] = _softmax_pv(s, vs[h]).astype(o_ref.dtype)


def _mla_ctx(q_mla, ckv, kpe, wukv, layer, nope_norm_k, scale):
    width = 2 * H_B * LANE
    return pl.pallas_call(
        functools.partial(_mla_ctx_kernel, scale=scale),
        grid=(BATCH,),
        in_specs=[pl.BlockSpec((SEQ, width), lambda b: (b, 0)),
                  pl.BlockSpec((SEQ, KV_LORA), lambda b: (b, 0)),
                  pl.BlockSpec((SEQ, LANE), lambda b: (b, 0)),
                  pl.BlockSpec((None, KV_LORA, width), lambda b: (layer, 0, 0)),
                  pl.BlockSpec((1, LANE), lambda b: (0, 0))],
        out_specs=pl.BlockSpec((SEQ, H_B * V_B), lambda b: (b, 0)),
        out_shape=jax.ShapeDtypeStruct((M_ROWS, H_B * V_B), bf16),
        compiler_params=_cparams(("parallel",)),
        name="mla_attn_ctx",
    )(q_mla, ckv, kpe, wukv, nope_norm_k)


def _mla_lat_kernel(prev_ref, q_ref, ckvc_ref, kpec_ref, ckvn_ref, kpen_ref, w_ref, nn_ref, o_ref,
                    k_scr, v_scr, *, scale):
    del prev_ref

    @pl.when(pl.program_id(2) == 0)
    def _():
        ks, vs = _mla_kv(ckvc_ref[...], kpec_ref[...], w_ref[...], nn_ref[...], scale * LOG2_E)
        k_scr[0:PAST_LEN, :] = ks[0]
        v_scr[0:PAST_LEN, :] = vs[0]
        ks, vs = _mla_kv(ckvn_ref[...], kpen_ref[...], w_ref[...], nn_ref[...], scale * LOG2_E)
        k_scr[PAST_LEN:KV_SEQ, :] = ks[0]
        v_scr[PAST_LEN:KV_SEQ, :] = vs[0]

    k_b, v_b = k_scr[...], v_scr[...]
    half = q_ref.shape[0] // 2
    qs = [q_ref[0:half, :], q_ref[half:2 * half, :]]
    for i, o in enumerate(_attend(qs, [k_b, k_b], [v_b, v_b])):
        o_ref[i * half:(i + 1) * half, :] = o.astype(o_ref.dtype)


def _mla_lat(prev, q_mla, ckv, kpe, cache_ckv, cache_kpe, wukv, layer, nope_norm_k, scale, *, tq=512):
    nq = DEC_SEQ // tq
    lat0 = MC // DEC_SEQ
    return pl.pallas_call(
        functools.partial(_mla_lat_kernel, scale=scale),
        grid=(DEC_BATCH, H_B, nq),
        in_specs=[pl.BlockSpec(memory_space=pl.ANY),
                  pl.BlockSpec((tq, 2 * LANE), lambda b, h, i: (MC // tq + b * nq + i, h)),
                  pl.BlockSpec((None, None, PAST_LEN, KV_LORA), lambda b, h, i: (b, layer, 0, 0)),
                  pl.BlockSpec((None, None, PAST_LEN, LANE), lambda b, h, i: (b, layer, 0, 0)),
                  pl.BlockSpec((DEC_SEQ, KV_LORA), lambda b, h, i: (lat0 + b, 0)),
                  pl.BlockSpec((DEC_SEQ, LANE), lambda b, h, i: (lat0 + b, 0)),
                  pl.BlockSpec((None, KV_LORA, 2 * LANE), lambda b, h, i: (layer, 0, h)),
                  pl.BlockSpec((1, LANE), lambda b, h, i: (0, 0))],
        out_specs=pl.BlockSpec((tq, V_B), lambda b, h, i: (MC // tq + b * nq + i, h)),
        out_shape=jax.ShapeDtypeStruct((M_ROWS, H_B * V_B), bf16),
        scratch_shapes=[pltpu.VMEM((KV_SEQ, 2 * LANE), bf16), pltpu.VMEM((KV_SEQ, V_B), bf16)],
        input_output_aliases={0: 0},
        compiler_params=_cparams(("parallel", "parallel", "arbitrary")),
        name="mla_attn_lat",
    )(prev, q_mla, cache_ckv, cache_kpe, ckv, kpe, wukv, nope_norm_k)


def _gqa_ctx_kernel(cq_ref, ck_ref, cv_ref, qkn_ref, o_ref, kg_ref, *, scale):
    rep = H_C // KVH_C
    for g in range(KVH_C):
        gs = slice(g * HD_C, (g + 1) * HD_C)
        k = _rms(ck_ref[:, gs], qkn_ref[1:2, :])
        kg_ref[:, gs] = k
        k_b = (k * (scale * LOG2_E)).astype(bf16)
        v_b = cv_ref[:, gs].astype(bf16)
        for h in range(g * rep, (g + 1) * rep):
            hs = slice(h * HD_C, (h + 1) * HD_C)
            q = _rms(cq_ref[:, hs], qkn_ref[0:1, :]).astype(bf16)
            s = lax.dot_general(q, k_b, _TRANS_B, preferred_element_type=f32)
            o_ref[:, hs] = _softmax_pv(s, v_b).astype(o_ref.dtype)


def _gqa_ctx(z, qk_norm, scale):
    def zspec(col, w):
        return pl.BlockSpec((SEQ, w), lambda b: (b, col // w))

    return pl.pallas_call(
        functools.partial(_gqa_ctx_kernel, scale=scale),
        grid=(BATCH,),
        in_specs=[zspec(COL_CQ, H_C * HD_C), zspec(COL_CK, KVH_C * HD_C), zspec(COL_CV, KVH_C * HD_C),
                  pl.BlockSpec((2, HD_C), lambda b: (0, 0))],
        out_specs=[pl.BlockSpec((SEQ, H_C * HD_C), lambda b: (b, 0)),
                   pl.BlockSpec((SEQ, KVH_C * HD_C), lambda b: (b, 0))],
        out_shape=[jax.ShapeDtypeStruct((M_ROWS, H_C * HD_C), bf16),
                   jax.ShapeDtypeStruct((MC, KVH_C * HD_C), f32)],
        compiler_params=_cparams(("parallel",)),
        name="gqa_attn_ctx",
    )(z, z, z, qk_norm)


def _gqa_lat_kernel(prev_ref, cq_ref, ckn_ref, cvn_ref, kc_ref, vc_ref, qkn_ref, ck_ref, sk_ref, cqt_ref, sqt_ref,
                    o_ref, k_scr, v_scr, *, scale):
    del prev_ref
    rep = H_C // KVH_C

    @pl.when(pl.program_id(2) == 0)
    def _():
        kscale = scale * LOG2_E
        k_scr[0:PAST_LEN, :] = (kc_ref[...] * kscale).astype(bf16)
        v_scr[0:PAST_LEN, :] = vc_ref[...].astype(bf16)
        k = _rope(_rms(ckn_ref[...], qkn_ref[1:2, :]), ck_ref[...], sk_ref[...], HD_C // 4)
        k_scr[PAST_LEN:KV_SEQ, :] = (k * kscale).astype(bf16)
        v_scr[PAST_LEN:KV_SEQ, :] = cvn_ref[...].astype(bf16)

    cq, sq = cqt_ref[...], sqt_ref[...]
    k_b, v_b = k_scr[...], v_scr[...]
    half = cq_ref.shape[0] // 2
    for r in range(rep):
        rs = slice(r * HD_C, (r + 1) * HD_C)
        q = _rope(_rms(cq_ref[:, rs], qkn_ref[0:1, :]), cq, sq, HD_C // 4).astype(bf16)
        for i, o in enumerate(_attend([q[0:half], q[half:2 * half]], [k_b, k_b], [v_b, v_b])):
            o_ref[i * half:(i + 1) * half, rs] = o.astype(o_ref.dtype)


def _gqa_lat(prev, z, cache_k, cache_v, layer, qk_norm, tabs_lat, scale, *, tq=512):
    nq = DEC_SEQ // tq
    lat0 = MC // DEC_SEQ
    rep = H_C // KVH_C
    gw = rep * HD_C
    cos, sin = tabs_lat
    return pl.pallas_call(
        functools.partial(_gqa_lat_kernel, scale=scale),
        grid=(DEC_BATCH, KVH_C, nq),
        in_specs=[pl.BlockSpec(memory_space=pl.ANY),
                  pl.BlockSpec((tq, gw), lambda b, g, i: (MC // tq + b * nq + i, COL_CQ // gw + g)),
                  pl.BlockSpec((DEC_SEQ, HD_C), lambda b, g, i: (lat0 + b, COL_CK // HD_C + g)),
                  pl.BlockSpec((DEC_SEQ, HD_C), lambda b, g, i: (lat0 + b, COL_CV // HD_C + g)),
                  pl.BlockSpec((None, None, PAST_LEN, HD_C), lambda b, g, i: (b, layer, 0, g)),
                  pl.BlockSpec((None, None, PAST_LEN, HD_C), lambda b, g, i: (b, layer, 0, g)),
                  pl.BlockSpec((2, HD_C), lambda b, g, i: (0, 0)),
                  pl.BlockSpec((DEC_SEQ, LANE), lambda b, g, i: (0, 0)),
                  pl.BlockSpec((DEC_SEQ, LANE), lambda b, g, i: (0, 0)),
                  pl.BlockSpec((tq, LANE), lambda b, g, i: (i, 0)),
                  pl.BlockSpec((tq, LANE), lambda b, g, i: (i, 0))],
        out_specs=pl.BlockSpec((tq, gw), lambda b, g, i: (MC // tq + b * nq + i, g)),
        out_shape=jax.ShapeDtypeStruct((M_ROWS, H_C * HD_C), bf16),
        scratch_shapes=[pltpu.VMEM((KV_SEQ, HD_C), bf16), pltpu.VMEM((KV_SEQ, HD_C), bf16)],
        input_output_aliases={0: 0},
        compiler_params=_cparams(("parallel", "parallel", "arbitrary")),
        name="gqa_attn_lat",
    )(prev, z, z, z, cache_k, cache_v, qk_norm, cos, sin, cos, sin)


def _hgrn_levels(c):
    w, out = 1, []
    while w < c:
        out.append(w)
        w *= 2
    return out


def _hgrn_consts(c):
    p = np.arange(c)[:, None]
    r = np.arange(c)[None, :]
    sums = [r <= p]
    masks = [r == p]
    for w in _hgrn_levels(c):
        base = (p // (2 * w)) * (2 * w)
        ref = base + w - 1
        later = (p % (2 * w)) >= w
        sums.append(np.where(later, (r > ref) & (r <= p), (r > p) & (r <= ref)))
        masks.append(((r // (2 * w)) * (2 * w) == base) & later & ((r % (2 * w)) < w))
    sums.append(r > p)
    g = np.stack(sums).astype(np.float32)
    m = np.stack(masks).astype(np.float32)
    flip = lambda a: a[:, ::-1, ::-1]
    g = np.stack([g, flip(g)]).reshape(2, -1, c)
    m = np.stack([m, flip(m)]).reshape(2, -1, c)
    return jnp.asarray(np.concatenate([g] * HGRN_SPLIT, axis=-1), bf16), jnp.asarray(m, f32)


def _hgrn_kernel(*refs, layer, t_len, latent):
    if latent:
        refs = refs[1:]
        (lb_ref, on_ref, gm_ref, mk_ref, q_ref, v_ref, ff_ref, fb_ref, g_ref, s_ref,
         out_ref, of_ref, ob_ref, qd_ref, u_ref, dec_ref, st_ref) = refs
        sout_ref = None
    else:
        (lb_ref, on_ref, gm_ref, mk_ref, q_ref, v_ref, ff_ref, fb_ref, g_ref,
         out_ref, sout_ref, of_ref, ob_ref, qd_ref, u_ref, dec_ref, st_ref) = refs
        s_ref = None
    c = HGRN_C
    hp = HGRN_HP
    n_chunks = t_len // c
    levels = _hgrn_levels(c)
    n_lv = len(levels)
    trans_b = (((1,), (1,)), ((), ()))
    heads = [slice(i * LANE, (i + 1) * LANE) for i in range(hp)]

    row = lax.broadcasted_iota(jnp.int32, (c, LANE), 0)

    if layer > 0:
        ps = [lb_ref[i] for i in range(DEPTH)]
        pmax = functools.reduce(jnp.maximum, ps)
        es = [jnp.exp(p - pmax) for p in ps]
        lb = functools.reduce(lambda a, b: a + b, es[1:layer + 1]) / functools.reduce(lambda a, b: a + b, es)
        log_lb = jnp.log(lb)
        log_1m = jnp.log1p(-lb)

    def log_forget(pre, d):
        ls = jnp.minimum(pre, 0.0) - jnp.log1p(jnp.exp(-jnp.abs(pre)))
        if layer == 0:
            return ls
        a = log_lb[d:d + 1, :]
        cc = log_1m[d:d + 1, :] + ls
        return jnp.maximum(a, cc) + jnp.log1p(jnp.exp(-jnp.abs(a - cc)))

    dirs = (0, 1)
    pre_refs = (ff_ref, fb_ref)
    oacc_refs = (of_ref, ob_ref)

    def stack(a):
        return jnp.concatenate([a[:, hs] for hs in heads], axis=0)

    def head_blocks(pp):
        return [pp[i * c:(i + 1) * c, i * c:(i + 1) * c] for i in range(hp)]

    def split(a):
        pieces, rem = [], a
        for _ in range(HGRN_SPLIT):
            pieces.append(rem.astype(bf16))
            rem = rem - pieces[-1].astype(f32)
        return jnp.concatenate(pieces, axis=0)

    def intra_body(ci, carry):
        sl = pl.ds(pl.multiple_of(ci * c, c), c)
        qx = q_ref[sl, :]
        q = qx * _sigmoid(qx)
        q_b = q.astype(bf16)
        v_b = v_ref[sl, :].astype(bf16)
        logf = [log_forget(pre_refs[d][sl, :], d) for d in dirs]
        k = [1.0 - jnp.exp(lf) for lf in logf]
        x = [jnp.exp(jnp.dot(gm_ref[d], split(logf[d]), preferred_element_type=f32)) for d in dirs]
        for d in dirs:
            qd_ref[d, sl, :] = (q * x[d][0:c]).astype(bf16)
        kd = [(k[d] * x[d][(n_lv + 1) * c:(n_lv + 2) * c]).astype(bf16) for d in dirs]

        qs = stack(q_b)
        pps = [lax.dot_general(qs, stack(k[d].astype(bf16)), trans_b, preferred_element_type=f32) for d in dirs]
        attn = [[mk_ref[d, 0:c, :] * blk for blk in head_blocks(pps[d])] for d in dirs]
        for li, w in enumerate(levels):
            rs = []
            for d in dirs:
                later = ((row % (2 * w)) < w) if d == 1 else ((row % (2 * w)) >= w)
                later = jnp.concatenate([later] * hp, axis=1)
                rs.append(stack((jnp.where(later, q, k[d]) * x[d][(1 + li) * c:(2 + li) * c]).astype(bf16)))
            pps = [lax.dot_general(r, r, trans_b, preferred_element_type=f32) for r in rs]
            for d in dirs:
                mask = mk_ref[d, (1 + li) * c:(2 + li) * c, :]
                attn[d] = [a + mask * blk for a, blk in zip(attn[d], head_blocks(pps[d]))]
        for hh, hs in enumerate(heads):
            outs = [jnp.dot(attn[d][hh].astype(bf16), v_b[:, hs], preferred_element_type=f32) for d in dirs]
            us = [lax.dot_general(v_b[:, hs], kd[d][:, hs], (((0,), (0,)), ((), ())),
                                  preferred_element_type=f32) for d in dirs]
            for d in dirs:
                oacc_refs[d][sl, hs] = outs[d]
                u_ref[d, ci, hh] = us[d]
        for d in dirs:
            decay = x[d][0:1] if d == 1 else x[d][c - 1:c]
            dec_ref[d, ci] = jnp.broadcast_to(decay, (8, hp * LANE))
        return carry

    lax.fori_loop(0, n_chunks, intra_body, 0)

    for d in range(2):
        for hh in range(hp):
            if latent:
                st_ref[d, hh] = s_ref[0, 0, d, hh].T
            else:
                st_ref[d, hh] = jnp.zeros((DV_A, DK_A), f32)

    def scan(d, cidx, oacc_ref):
        sl = pl.ds(pl.multiple_of(cidx * c, c), c)
        dec = dec_ref[d, cidx]
        for hh, hs in enumerate(heads):
            st = st_ref[d, hh]
            oacc_ref[sl, hs] = oacc_ref[sl, hs] + lax.dot_general(
                qd_ref[d, sl, hs], st.astype(bf16), trans_b, preferred_element_type=f32)
            st_ref[d, hh] = st * dec[0:1, hs] + u_ref[d, cidx, hh]

    def scan_body(ci, carry):
        scan(0, ci, of_ref)
        scan(1, n_chunks - 1 - ci, ob_ref)
        return carry

    lax.fori_loop(0, n_chunks, scan_body, 0)

    if not latent:
        for d in range(2):
            for hh in range(hp):
                sout_ref[0, d, hh] = st_ref[d, hh].T

    gx = g_ref[...]
    gate = gx * _sigmoid(gx)
    for hs in heads:
        o = of_ref[:, hs] + ob_ref[:, hs]
        out_ref[:, hs] = (_rms(o, on_ref[...]) * gate[:, hs]).astype(out_ref.dtype)


def _hgrn(z, hgrn_lb, onorm, state, consts, *, layer, latent, prev=None):
    t_len = DEC_SEQ if latent else SEQ
    n_samples = DEC_BATCH if latent else BATCH
    row0 = (MC // t_len) if latent else 0
    gmat, masks = consts
    hp = HGRN_HP
    wide = hp * LANE

    def zspec(col):
        base = col // wide
        return pl.BlockSpec((t_len, wide), lambda b, h: (row0 + b, base + h))

    in_specs = [pl.BlockSpec((DEPTH, 2, wide), lambda b, h: (0, 0, h)),
                pl.BlockSpec((1, LANE), lambda b, h: (0, 0)),
                pl.BlockSpec(gmat.shape, lambda b, h: (0, 0, 0)),
                pl.BlockSpec(masks.shape, lambda b, h: (0, 0, 0)),
                zspec(COL_AQ), zspec(COL_AV), zspec(COL_AFF), zspec(COL_AFB), zspec(COL_AG)]
    args = [hgrn_lb, onorm, gmat, masks, z, z, z, z, z]
    o_spec = pl.BlockSpec((t_len, wide), lambda b, h: (row0 + b, h))
    o_shape = jax.ShapeDtypeStruct((M_ROWS, H_A * DV_A), bf16)
    aliases = {}
    if latent:
        in_specs.insert(0, pl.BlockSpec(memory_space=pl.ANY))
        args.insert(0, prev)
        aliases = {0: 0}
        in_specs.append(pl.BlockSpec((1, 1, 2, hp, DK_A, DV_A), lambda b, h: (b, layer, 0, h, 0, 0)))
        args.append(state)
        out_specs, out_shape = o_spec, o_shape
    else:
        out_specs = [o_spec, pl.BlockSpec((1, 2, hp, DK_A, DV_A), lambda b, h: (b, 0, h, 0, 0))]
        out_shape = [o_shape, jax.ShapeDtypeStruct((BATCH, 2, H_A, DK_A, DV_A), f32)]
    n_chunks = t_len // HGRN_C
    return pl.pallas_call(
        functools.partial(_hgrn_kernel, layer=layer, t_len=t_len, latent=latent),
        grid=(n_samples, H_A // hp),
        in_specs=in_specs,
        out_specs=out_specs,
        out_shape=out_shape,
        scratch_shapes=[pltpu.VMEM((t_len, wide), f32), pltpu.VMEM((t_len, wide), f32),
                        pltpu.VMEM((2, t_len, wide), bf16),
                        pltpu.VMEM((2, n_chunks, hp, DV_A, DK_A), f32),
                        pltpu.VMEM((2, n_chunks, 8, wide), f32),
                        pltpu.VMEM((2, hp, DV_A, DK_A), f32)],
        input_output_aliases=aliases,
        compiler_params=_cparams(("parallel", "parallel")),
        name="hgrn_lat" if latent else "hgrn_ctx",
    )(*args)


def _permute_wuq(wuq):
    w = wuq.reshape(DEPTH, Q_LORA, H_B, NOPE_B + ROPE_B)
    nope = w[..., :NOPE_B].reshape(DEPTH, Q_LORA, H_B * NOPE_B)
    pe = jnp.pad(w[..., NOPE_B:], ((0, 0), (0, 0), (0, 0), (0, LANE - ROPE_B))).reshape(DEPTH, Q_LORA, H_B * LANE)
    return jnp.concatenate([nope, pe], axis=-1).astype(bf16)


def _pad_lane(v):
    return jnp.pad(v, ((0, 0),) * (v.ndim - 1) + ((0, LANE - v.shape[-1]),))


def kernel(x_prompt, x_sample, state_hgrn, cache_mla_ckv, cache_mla_kpe, cache_gqa_k, cache_gqa_v, c, c_ctx,
           w_mod, b_mod, norm_w, ffn_w13, ffn_w2, w_in, hgrn_lb, hgrn_onorm, mla_qa_norm, mla_kva_norm,
           mla_wuq, mla_wukv, mla_nope_norm, mla_rope_norm, gqa_qk_norm, w_branch, w_out):
    w_in_t = jnp.swapaxes(w_in, 1, 2)
    wuq_b = _permute_wuq(mla_wuq)
    wukv_b = mla_wukv.astype(bf16)
    hgrn_consts = _hgrn_consts(HGRN_C)
    rope_norm_p = _pad_lane(mla_rope_norm)
    cache_kpe_p = _pad_lane(cache_mla_kpe)
    cache_k = cache_gqa_k.reshape(DEC_BATCH, DEPTH, PAST_LEN, KVH_C * HD_C)
    cache_v = cache_gqa_v.reshape(DEC_BATCH, DEPTH, PAST_LEN, KVH_C * HD_C)
    tabs_m = tuple(jnp.asarray(t) for t in _rope_tables(ROPE_B))
    tabs_g_lat = tuple(jnp.asarray(t[PAST_LEN:]) for t in _rope_tables(HD_C))
    mla_scale = (NOPE_B + ROPE_B) ** -0.5
    gqa_scale = HD_C ** -0.5

    cond = jnp.zeros((N_MOD_ROWS, D_MODEL), f32).at[0].set(c_ctx).at[1:1 + DEC_BATCH].set(c)
    x = jnp.concatenate([x_prompt.reshape(MC, D_MODEL), x_sample.reshape(ML, D_MODEL)], axis=0)

    st_h, st_ckv, st_kpe, st_k, st_v = [], [], [], [], []
    for l in range(DEPTH):
        mod = _mm(cond, [(w_mod, (l,), 0)], n=9 * D_MODEL, tm=N_MOD_ROWS, tn=1024, out_dtype=f32, name="adaln",
                  pro="silu", epi="bias", bias=b_mod[l][None, :]).reshape(N_MOD_ROWS, 9, D_MODEL)

        def ffn(xin, i, ids):
            g = _mm(xin, [(ffn_w13, (l, i), 0), (ffn_w13, (l, i), D_FF)], n=D_FF, tm=1024, tn=512,
                    out_dtype=bf16, name="ffn_up", pro="normmod", epi="swiglu",
                    norm_w=norm_w[l, 2 * i][None, :], mod=mod, shift_id=ids[0], scale_id=ids[1])
            return _mm(g, [(ffn_w2, (l, i), 0)], n=D_MODEL, tm=1024, tn=256, out_dtype=f32, name="ffn_down",
                       epi="residual", res=xin, mod=mod, gate_id=ids[2], gate_scale=0.5)

        x = ffn(x, 0, (0, 1, 2))

        z = _mm(x, [(w_in_t, (l,), 0)], n=N_IN_PAD, tm=1024, tn=IN_TILE, out_dtype=f32, name="in_proj",
                pro="normmod", norm_w=norm_w[l, 1][None, :], mod=mod, shift_id=3, scale_id=4,
                w_rows=lambda j: (j * (IN_TILE // ROPE_B)
                                  - jnp.where(j >= IN_SHIFT_FROM, IN_SHIFT // ROPE_B, 0)) * ROPE_B)

        out_a, s_new = _hgrn(z, hgrn_lb, hgrn_onorm[l][None, :], None, hgrn_consts, layer=l, latent=False)
        out_a = _hgrn(z, hgrn_lb, hgrn_onorm[l][None, :], state_hgrn, hgrn_consts, layer=l, latent=True,
                      prev=out_a)
        st_h.append(s_new)

        q_mla, ckv, kpe = _mla_prep(z, mla_qa_norm[l][None, :], mla_kva_norm[l][None, :], rope_norm_p[l],
                                    mla_nope_norm[l, 0][None, :], wuq_b, l, tabs_m)
        nn_k = mla_nope_norm[l, 1][None, :]
        out_b = _mla_ctx(q_mla, ckv, kpe, wukv_b, l, nn_k, mla_scale)
        out_b = _mla_lat(out_b, q_mla, ckv, kpe, cache_mla_ckv, cache_kpe_p, wukv_b, l, nn_k, mla_scale)
        st_ckv.append(ckv[:MC].reshape(BATCH, SEQ, KV_LORA))
        st_kpe.append(kpe[:MC, :ROPE_B].reshape(BATCH, SEQ, ROPE_B))

        out_c, k_g = _gqa_ctx(z, gqa_qk_norm[l], gqa_scale)
        out_c = _gqa_lat(out_c, z, cache_k, cache_v, l, gqa_qk_norm[l], tabs_g_lat, gqa_scale)
        st_k.append(k_g.reshape(BATCH, SEQ, KVH_C, HD_C))
        st_v.append(z[:MC, COL_CV:COL_CV + KVH_C * HD_C].reshape(BATCH, SEQ, KVH_C, HD_C))

        merged = _merge(out_a, out_b, out_c, w_branch, l, z)
        x = _mm(merged, [(w_out, (l,), 0)], n=D_MODEL, tm=1024, tn=1024, out_dtype=f32, name="out_proj",
                epi="residual", res=x, mod=mod, gate_id=5, gate_scale=1.0)

        x = ffn(x, 1, (6, 7, 8))

    y_p = x[:MC].reshape(BATCH, SEQ, D_MODEL)
    y_s = x[MC:].reshape(DEC_BATCH, DEC_SEQ, D_MODEL)
    return (y_p, y_s,
            jnp.stack(st_h, axis=1), jnp.stack(st_ckv, axis=1), jnp.stack(st_kpe, axis=1),
            jnp.stack(st_k, axis=1), jnp.stack(st_v, axis=1))
```

```python
import functools

import numpy as np
import jax
import jax.numpy as jnp
from jax import lax
from jax.experimental import pallas as pl
from jax.experimental.pallas import tpu as pltpu

f32 = jnp.float32
bf16 = jnp.bfloat16

D_MODEL = 2048
BATCH = 16
SEQ = 256
DEPTH = 4
DEC_BATCH = 4
DEC_SEQ = 1024
PAST_LEN = 256
GRID_W = 64
ROPE_THETA = 10000.0
EPS = 1e-6
D_FF = 5632
N_BRANCH = 3
BR_W = 1024
H_A = 8
DK_A = 128
DV_A = 128
HK_A = H_A * DK_A
H_B = 8
Q_LORA = 512
KV_LORA = 256
NOPE_B = 128
ROPE_B = 64
V_B = 128
H_C = 8
KVH_C = 4
HD_C = 128

LANE = 128
MC = BATCH * SEQ
ML = DEC_BATCH * DEC_SEQ
M_ROWS = MC + ML
KV_SEQ = PAST_LEN + DEC_SEQ
KV_LAT_ROWS = DEC_BATCH * KV_SEQ
KV_ROWS = KV_LAT_ROWS + MC
N_MOD_ROWS = 16

COL_AQ, COL_AV, COL_AFF, COL_AFB, COL_AG = 0, 1024, 2048, 3072, 4096
COL_BQ = 5120
COL_BKV = 5632
COL_BPE = 5888
COL_CQ = 6144
COL_CK = 7168
COL_CV = 7680
COL_GBR = 8192
N_IN_PAD = 14336
IN_TILE = 1024
IN_SHIFT_FROM = COL_CQ // IN_TILE
IN_SHIFT = COL_CQ - (COL_BPE + ROPE_B)

ROW_TILE = 256
HGRN_C = 128
HGRN_HP = 2
HGRN_SPLIT = 2
HGRN_BLOCK = 16
HGRN_FAST_LIMIT = 70.0
VMEM_LIMIT = 56 * 1024 * 1024


def _cparams(sem):
    return pltpu.CompilerParams(dimension_semantics=sem, vmem_limit_bytes=VMEM_LIMIT)


def _sigmoid(x):
    return 1.0 / (1.0 + jnp.exp(-x))


def _rms(x, w, n=None):
    n = x.shape[-1] if n is None else n
    ms = jnp.sum(x * x, axis=-1, keepdims=True) * (1.0 / n)
    return x * lax.rsqrt(ms + EPS) * w


def _mod_row(i, tm):
    n_ctx = MC // tm
    per = DEC_SEQ // tm
    return jnp.where(i < n_ctx, 0, 1 + jnp.maximum(i - n_ctx, 0) // per)


def _mm_kernel(*refs, pro, epi, n_w, shift_id, scale_id, gate_id, gate_scale, w_rows):
    it = iter(refs)
    x_ref = next(it)
    nw_ref = next(it) if pro in ("norm", "normmod") else None
    modk_ref = next(it) if pro == "normmod" else None
    w_refs = [next(it) for _ in range(n_w)]
    res_ref = next(it) if epi == "residual" else None
    modn_ref = next(it) if epi == "residual" else None
    bias_ref = next(it) if epi == "bias" else None
    o_ref = next(it)
    h_ref = next(it) if pro is not None else None

    if pro is not None:
        @pl.when(pl.program_id(1) == 0)
        def _():
            x = x_ref[...].astype(f32)
            if pro == "silu":
                y = x * _sigmoid(x)
            else:
                y = _rms(x, nw_ref[...])
                if pro == "normmod":
                    y = y * (1.0 + modk_ref[0, scale_id:scale_id + 1, :]) + modk_ref[0, shift_id:shift_id + 1, :]
            h_ref[...] = y.astype(bf16)
        lhs = h_ref[...]
    else:
        lhs = x_ref[...]

    if w_rows is None:
        accs = [jnp.dot(lhs, w[...].astype(bf16), preferred_element_type=f32) for w in w_refs]
    else:
        accs = [lax.dot_general(lhs, w[(0,) * (len(w.shape) - 2)].astype(bf16), (((1,), (1,)), ((), ())),
                                preferred_element_type=f32) for w in w_refs]
    if epi == "swiglu":
        a, u = accs
        out = a * _sigmoid(a) * u
    elif epi == "residual":
        out = res_ref[...] + (gate_scale * modn_ref[0, gate_id:gate_id + 1, :]) * accs[0]
    elif epi == "bias":
        out = accs[0] + bias_ref[...]
    else:
        out = accs[0]
    o_ref[...] = out.astype(o_ref.dtype)


def _w_spec(w, lead, col0, k, tn):
    base = col0 // tn
    return pl.BlockSpec((None,) * len(lead) + (k, tn), lambda i, j: tuple(lead) + (0, base + j))


def _mm(x, ws, *, n, tm, tn, out_dtype, name, pro=None, epi="store", norm_w=None, mod=None,
        shift_id=0, scale_id=0, gate_id=0, gate_scale=1.0, res=None, bias=None, w_rows=None):
    m, k = x.shape
    grid = (m // tm, n // tn)
    in_specs = [pl.BlockSpec((tm, k), lambda i, j: (i, 0))]
    args = [x]
    if pro in ("norm", "normmod"):
        in_specs.append(pl.BlockSpec((1, k), lambda i, j: (0, 0)))
        args.append(norm_w)
    if pro == "normmod":
        in_specs.append(pl.BlockSpec((1, 9, k), lambda i, j: (_mod_row(i, tm), 0, 0)))
        args.append(mod)
    for w, lead, col0 in ws:
        if w_rows is None:
            in_specs.append(_w_spec(w, lead, col0, k, tn))
        else:
            in_specs.append(pl.BlockSpec((pl.Element(1),) * len(lead) + (pl.Element(tn), pl.Element(k)),
                                         lambda i, j, lead=lead: tuple(lead) + (w_rows(j), 0)))
        args.append(w)
    if epi == "residual":
        in_specs.append(pl.BlockSpec((tm, tn), lambda i, j: (i, j)))
        args.append(res)
        in_specs.append(pl.BlockSpec((1, 9, tn), lambda i, j: (_mod_row(i, tm), 0, j)))
        args.append(mod)
    if epi == "bias":
        in_specs.append(pl.BlockSpec((1, tn), lambda i, j: (0, j)))
        args.append(bias)
    scratch = [pltpu.VMEM((tm, k), bf16)] if pro is not None else []
    kern = functools.partial(_mm_kernel, pro=pro, epi=epi, n_w=len(ws), shift_id=shift_id,
                             scale_id=scale_id, gate_id=gate_id, gate_scale=gate_scale, w_rows=w_rows)
    return pl.pallas_call(
        kern,
        grid=grid,
        in_specs=in_specs,
        out_specs=pl.BlockSpec((tm, tn), lambda i, j: (i, j)),
        out_shape=jax.ShapeDtypeStruct((m, n), out_dtype),
        scratch_shapes=scratch,
        compiler_params=_cparams(("parallel", "arbitrary")),
        name=name,
    )(*args)


def _merge_kernel(oa_ref, ob_ref, oc_ref, wb_ref, ga_ref, gb_ref, gc_ref, o_ref):
    acc = None
    for n, (o, g) in enumerate(((oa_ref, ga_ref), (ob_ref, gb_ref), (oc_ref, gc_ref))):
        br = jnp.dot(o[...], wb_ref[n].astype(bf16), preferred_element_type=f32)
        term = _sigmoid(g[...]) * br
        acc = term if acc is None else acc + term
    o_ref[...] = acc.astype(o_ref.dtype)


def _merge(out_a, out_b, out_c, wb, layer, z, *, tm=2048, tn=256):
    grid = (M_ROWS // tm, D_MODEL // tn)
    o_spec = pl.BlockSpec((tm, BR_W), lambda i, j: (i, 0))

    def g_spec(n):
        base = (COL_GBR + n * D_MODEL) // tn
        return pl.BlockSpec((tm, tn), lambda i, j: (i, base + j))

    return pl.pallas_call(
        _merge_kernel,
        grid=grid,
        in_specs=[o_spec, o_spec, o_spec,
                  pl.BlockSpec((None, N_BRANCH, BR_W, tn), lambda i, j: (layer, 0, 0, j)),
                  g_spec(0), g_spec(1), g_spec(2)],
        out_specs=pl.BlockSpec((tm, tn), lambda i, j: (i, j)),
        out_shape=jax.ShapeDtypeStruct((M_ROWS, D_MODEL), bf16),
        compiler_params=_cparams(("parallel", "arbitrary")),
        name="merge",
    )(out_a, out_b, out_c, wb, z, z, z)


def _rope(x, cos, sin, quarter):
    lane = lax.broadcasted_iota(jnp.int32, x.shape, 1)
    first = (lane % (2 * quarter)) < quarter
    partner = jnp.where(first, -pltpu.roll(x, LANE - quarter, axis=1), pltpu.roll(x, quarter, axis=1))
    return x * cos + partner * sin


def _rope_tables(width):
    quarter = width // 4
    t = np.arange(DEC_SEQ)
    inv = ROPE_THETA ** (-np.arange(quarter, dtype=np.float32) / quarter)
    ang_r = (t // GRID_W).astype(np.float32)[:, None] * inv[None, :]
    ang_c = (t % GRID_W).astype(np.float32)[:, None] * inv[None, :]
    ang = np.concatenate([ang_r, ang_r, ang_c, ang_c], axis=1).astype(np.float32)
    cos = np.ones((KV_SEQ, LANE), np.float32)
    sin = np.zeros((KV_SEQ, LANE), np.float32)
    cos[PAST_LEN:, :width] = np.cos(ang)
    sin[PAST_LEN:, :width] = np.sin(ang)
    return cos, sin


def _q_table_block(i):
    n_ctx = MC // ROW_TILE
    per = DEC_SEQ // ROW_TILE
    return jnp.where(i < n_ctx, 0, 1 + jnp.maximum(i - n_ctx, 0) % per)


LOG2_E = 1.4426950408889634


def _softmax_pv(s2, v_b):
    m = jnp.max(s2, axis=-1, keepdims=True)
    p = jnp.exp2(s2 - m)
    l = jnp.sum(p, axis=-1, keepdims=True)
    return jnp.dot(p.astype(bf16), v_b, preferred_element_type=f32) / l


_TRANS_B = (((1,), (1,)), ((), ()))


def _attend(qs, ks, vs):
    outs = []
    for i in range(0, len(qs), 2):
        ss = [lax.dot_general(q, k, _TRANS_B, preferred_element_type=f32)
              for q, k in zip(qs[i:i + 2], ks[i:i + 2])]
        outs += [_softmax_pv(s, v) for s, v in zip(ss, vs[i:i + 2])]
    return outs


def _mla_prep_kernel(bq_ref, bkv_ref, bpe_ref, qa_ref, kva_ref, rn_ref, nn_ref, wuq_ref, cm_ref, sm_ref,
                     q_ref, ckv_ref, kpe_ref):
    cm, sm = cm_ref[...], sm_ref[...]
    ckv_ref[...] = _rms(bkv_ref[...], kva_ref[...])
    bpe = bpe_ref[...]
    bpe = jnp.where(lax.broadcasted_iota(jnp.int32, bpe.shape, 1) < ROPE_B, bpe, 0.0)
    kpe_ref[...] = _rope(_rms(bpe, rn_ref[1:2, :], n=ROPE_B), cm, sm, ROPE_B // 4)
    cq = jnp.dot(_rms(bq_ref[...], qa_ref[...]).astype(bf16), wuq_ref[...], preferred_element_type=f32)
    for h in range(H_B):
        nope = _rms(cq[:, h * LANE:(h + 1) * LANE], nn_ref[...])
        pe = _rms(cq[:, (H_B + h) * LANE:(H_B + h + 1) * LANE], rn_ref[0:1, :], n=ROPE_B)
        q_ref[:, 2 * h * LANE:(2 * h + 1) * LANE] = nope.astype(bf16)
        q_ref[:, (2 * h + 1) * LANE:(2 * h + 2) * LANE] = _rope(pe, cm, sm, ROPE_B // 4).astype(bf16)


def _mla_prep(z, qa_norm, kva_norm, rope_norm, nope_norm_q, wuq, layer, tabs_m):
    tr = ROW_TILE
    width = 2 * H_B * LANE

    def zspec(col, w):
        return pl.BlockSpec((tr, w), lambda i: (i, col // w))

    def wspec(shape):
        return pl.BlockSpec(shape, lambda i: (0, 0))

    tspec = pl.BlockSpec((tr, LANE), lambda i: (_q_table_block(i), 0))
    return pl.pallas_call(
        _mla_prep_kernel,
        grid=(M_ROWS // tr,),
        in_specs=[zspec(COL_BQ, Q_LORA), zspec(COL_BKV, KV_LORA), zspec(COL_BPE, LANE),
                  wspec((1, Q_LORA)), wspec((1, KV_LORA)), wspec((2, LANE)), wspec((1, LANE)),
                  pl.BlockSpec((None, Q_LORA, width), lambda i: (layer, 0, 0)), tspec, tspec],
        out_specs=[pl.BlockSpec((tr, width), lambda i: (i, 0)),
                   pl.BlockSpec((tr, KV_LORA), lambda i: (i, 0)),
                   pl.BlockSpec((tr, LANE), lambda i: (i, 0))],
        out_shape=[jax.ShapeDtypeStruct((M_ROWS, width), bf16),
                   jax.ShapeDtypeStruct((M_ROWS, KV_LORA), f32),
                   jax.ShapeDtypeStruct((M_ROWS, LANE), f32)],
        compiler_params=_cparams(("parallel",)),
        name="mla_prep",
    )(z, z, z, qa_norm, kva_norm, rope_norm, nope_norm_q, wuq, *tabs_m)


def _mla_kv(ckv, kpe, w, nn, kscale):
    kv = jnp.dot(ckv.astype(bf16), w, preferred_element_type=f32)
    n_h = w.shape[1] // (2 * LANE)
    kpe_b = (kpe * kscale).astype(bf16)
    ks, vs = [], []
    for h in range(n_h):
        nope = _rms(kv[:, 2 * h * LANE:(2 * h + 1) * LANE], nn) * kscale
        ks.append(jnp.concatenate([nope.astype(bf16), kpe_b], axis=1))
        vs.append(kv[:, (2 * h + 1) * LANE:(2 * h + 2) * LANE].astype(bf16))
    return ks, vs


def _mla_ctx_kernel(q_ref, ckv_ref, kpe_ref, w_ref, nn_ref, o_ref, *, scale):
    ks, vs = _mla_kv(ckv_ref[...], kpe_ref[...], w_ref[...], nn_ref[...], scale * LOG2_E)
    for h in range(H_B):
        s = lax.dot_general(q_ref[:, 2 * h * LANE:(2 * h + 2) * LANE], ks[h], _TRANS_B,
                            preferred_element_type=f32)
        o_ref[:, h * V_B:(h + 1) * V_B] = _softmax_pv(s, vs[h]).astype(o_ref.dtype)


def _mla_ctx(q_mla, ckv, kpe, wukv, layer, nope_norm_k, scale):
    width = 2 * H_B * LANE
    return pl.pallas_call(
        functools.partial(_mla_ctx_kernel, scale=scale),
        grid=(BATCH,),
        in_specs=[pl.BlockSpec((SEQ, width), lambda b: (b, 0)),
                  pl.BlockSpec((SEQ, KV_LORA), lambda b: (b, 0)),
                  pl.BlockSpec((SEQ, LANE), lambda b: (b, 0)),
                  pl.BlockSpec((None, KV_LORA, width), lambda b: (layer, 0, 0)),
                  pl.BlockSpec((1, LANE), lambda b: (0, 0))],
        out_specs=pl.BlockSpec((SEQ, H_B * V_B), lambda b: (b, 0)),
        out_shape=jax.ShapeDtypeStruct((M_ROWS, H_B * V_B), bf16),
        compiler_params=_cparams(("parallel",)),
        name="mla_attn_ctx",
    )(q_mla, ckv, kpe, wukv, nope_norm_k)


def _mla_lat_kernel(prev_ref, q_ref, ckvc_ref, kpec_ref, ckvn_ref, kpen_ref, w_ref, nn_ref, o_ref,
                    k_scr, v_scr, *, scale):
    del prev_ref

    @pl.when(pl.program_id(2) == 0)
    def _():
        ks, vs = _mla_kv(ckvc_ref[...], kpec_ref[...], w_ref[...], nn_ref[...], scale * LOG2_E)
        k_scr[0:PAST_LEN, :] = ks[0]
        v_scr[0:PAST_LEN, :] = vs[0]
        ks, vs = _mla_kv(ckvn_ref[...], kpen_ref[...], w_ref[...], nn_ref[...], scale * LOG2_E)
        k_scr[PAST_LEN:KV_SEQ, :] = ks[0]
        v_scr[PAST_LEN:KV_SEQ, :] = vs[0]

    k_b, v_b = k_scr[...], v_scr[...]
    half = q_ref.shape[0] // 2
    qs = [q_ref[0:half, :], q_ref[half:2 * half, :]]
    for i, o in enumerate(_attend(qs, [k_b, k_b], [v_b, v_b])):
        o_ref[i * half:(i + 1) * half, :] = o.astype(o_ref.dtype)


def _mla_lat(prev, q_mla, ckv, kpe, cache_ckv, cache_kpe, wukv, layer, nope_norm_k, scale, *, tq=512):
    nq = DEC_SEQ // tq
    lat0 = MC // DEC_SEQ
    return pl.pallas_call(
        functools.partial(_mla_lat_kernel, scale=scale),
        grid=(DEC_BATCH, H_B, nq),
        in_specs=[pl.BlockSpec(memory_space=pl.ANY),
                  pl.BlockSpec((tq, 2 * LANE), lambda b, h, i: (MC // tq + b * nq + i, h)),
                  pl.BlockSpec((None, None, PAST_LEN, KV_LORA), lambda b, h, i: (b, layer, 0, 0)),
                  pl.BlockSpec((None, None, PAST_LEN, LANE), lambda b, h, i: (b, layer, 0, 0)),
                  pl.BlockSpec((DEC_SEQ, KV_LORA), lambda b, h, i: (lat0 + b, 0)),
                  pl.BlockSpec((DEC_SEQ, LANE), lambda b, h, i: (lat0 + b, 0)),
                  pl.BlockSpec((None, KV_LORA, 2 * LANE), lambda b, h, i: (layer, 0, h)),
                  pl.BlockSpec((1, LANE), lambda b, h, i: (0, 0))],
        out_specs=pl.BlockSpec((tq, V_B), lambda b, h, i: (MC // tq + b * nq + i, h)),
        out_shape=jax.ShapeDtypeStruct((M_ROWS, H_B * V_B), bf16),
        scratch_shapes=[pltpu.VMEM((KV_SEQ, 2 * LANE), bf16), pltpu.VMEM((KV_SEQ, V_B), bf16)],
        input_output_aliases={0: 0},
        compiler_params=_cparams(("parallel", "parallel", "arbitrary")),
        name="mla_attn_lat",
    )(prev, q_mla, cache_ckv, cache_kpe, ckv, kpe, wukv, nope_norm_k)


def _gqa_ctx_kernel(cq_ref, ck_ref, cv_ref, qkn_ref, o_ref, kg_ref, *, scale):
    rep = H_C // KVH_C
    for g in range(KVH_C):
        gs = slice(g * HD_C, (g + 1) * HD_C)
        k = _rms(ck_ref[:, gs], qkn_ref[1:2, :])
        kg_ref[:, gs] = k
        k_b = (k * (scale * LOG2_E)).astype(bf16)
        v_b = cv_ref[:, gs].astype(bf16)
        for h in range(g * rep, (g + 1) * rep):
            hs = slice(h * HD_C, (h + 1) * HD_C)
            q = _rms(cq_ref[:, hs], qkn_ref[0:1, :]).astype(bf16)
            s = lax.dot_general(q, k_b, _TRANS_B, preferred_element_type=f32)
            o_ref[:, hs] = _softmax_pv(s, v_b).astype(o_ref.dtype)


def _gqa_ctx(z, qk_norm, scale):
    def zspec(col, w):
        return pl.BlockSpec((SEQ, w), lambda b: (b, col // w))

    return pl.pallas_call(
        functools.partial(_gqa_ctx_kernel, scale=scale),
        grid=(BATCH,),
        in_specs=[zspec(COL_CQ, H_C * HD_C), zspec(COL_CK, KVH_C * HD_C), zspec(COL_CV, KVH_C * HD_C),
                  pl.BlockSpec((2, HD_C), lambda b: (0, 0))],
        out_specs=[pl.BlockSpec((SEQ, H_C * HD_C), lambda b: (b, 0)),
                   pl.BlockSpec((SEQ, KVH_C * HD_C), lambda b: (b, 0))],
        out_shape=[jax.ShapeDtypeStruct((M_ROWS, H_C * HD_C), bf16),
                   jax.ShapeDtypeStruct((MC, KVH_C * HD_C), f32)],
        compiler_params=_cparams(("parallel",)),
        name="gqa_attn_ctx",
    )(z, z, z, qk_norm)


def _gqa_lat_kernel(prev_ref, cq_ref, ckn_ref, cvn_ref, kc_ref, vc_ref, qkn_ref, ck_ref, sk_ref, cqt_ref, sqt_ref,
                    o_ref, k_scr, v_scr, *, scale):
    del prev_ref
    rep = H_C // KVH_C

    @pl.when(pl.program_id(2) == 0)
    def _():
        kscale = scale * LOG2_E
        k_scr[0:PAST_LEN, :] = (kc_ref[...] * kscale).astype(bf16)
        v_scr[0:PAST_LEN, :] = vc_ref[...].astype(bf16)
        k = _rope(_rms(ckn_ref[...], qkn_ref[1:2, :]), ck_ref[...], sk_ref[...], HD_C // 4)
        k_scr[PAST_LEN:KV_SEQ, :] = (k * kscale).astype(bf16)
        v_scr[PAST_LEN:KV_SEQ, :] = cvn_ref[...].astype(bf16)

    cq, sq = cqt_ref[...], sqt_ref[...]
    k_b, v_b = k_scr[...], v_scr[...]
    half = cq_ref.shape[0] // 2
    for r in range(rep):
        rs = slice(r * HD_C, (r + 1) * HD_C)
        q = _rope(_rms(cq_ref[:, rs], qkn_ref[0:1, :]), cq, sq, HD_C // 4).astype(bf16)
        for i, o in enumerate(_attend([q[0:half], q[half:2 * half]], [k_b, k_b], [v_b, v_b])):
            o_ref[i * half:(i + 1) * half, rs] = o.astype(o_ref.dtype)


def _gqa_lat(prev, z, cache_k, cache_v, layer, qk_norm, tabs_lat, scale, *, tq=512):
    nq = DEC_SEQ // tq
    lat0 = MC // DEC_SEQ
    rep = H_C // KVH_C
    gw = rep * HD_C
    cos, sin = tabs_lat
    return pl.pallas_call(
        functools.partial(_gqa_lat_kernel, scale=scale),
        grid=(DEC_BATCH, KVH_C, nq),
        in_specs=[pl.BlockSpec(memory_space=pl.ANY),
                  pl.BlockSpec((tq, gw), lambda b, g, i: (MC // tq + b * nq + i, COL_CQ // gw + g)),
                  pl.BlockSpec((DEC_SEQ, HD_C), lambda b, g, i: (lat0 + b, COL_CK // HD_C + g)),
                  pl.BlockSpec((DEC_SEQ, HD_C), lambda b, g, i: (lat0 + b, COL_CV // HD_C + g)),
                  pl.BlockSpec((None, None, PAST_LEN, HD_C), lambda b, g, i: (b, layer, 0, g)),
                  pl.BlockSpec((None, None, PAST_LEN, HD_C), lambda b, g, i: (b, layer, 0, g)),
                  pl.BlockSpec((2, HD_C), lambda b, g, i: (0, 0)),
                  pl.BlockSpec((DEC_SEQ, LANE), lambda b, g, i: (0, 0)),
                  pl.BlockSpec((DEC_SEQ, LANE), lambda b, g, i: (0, 0)),
                  pl.BlockSpec((tq, LANE), lambda b, g, i: (i, 0)),
                  pl.BlockSpec((tq, LANE), lambda b, g, i: (i, 0))],
        out_specs=pl.BlockSpec((tq, gw), lambda b, g, i: (MC // tq + b * nq + i, g)),
        out_shape=jax.ShapeDtypeStruct((M_ROWS, H_C * HD_C), bf16),
        scratch_shapes=[pltpu.VMEM((KV_SEQ, HD_C), bf16), pltpu.VMEM((KV_SEQ, HD_C), bf16)],
        input_output_aliases={0: 0},
        compiler_params=_cparams(("parallel", "parallel", "arbitrary")),
        name="gqa_attn_lat",
    )(prev, z, z, z, cache_k, cache_v, qk_norm, cos, sin, cos, sin)


def _hgrn_levels(c, base):
    w, out = base, []
    while w < c:
        out.append(w)
        w *= 2
    return out


def _hgrn_consts(c, base):
    p = np.arange(c)[:, None]
    r = np.arange(c)[None, :]
    sums = [(r <= p).astype(np.float32)]
    if base == 1:
        masks = [r == p]
    else:
        in_block = ((r // base) == (p // base)) & (r <= p)
        sums += [in_block.astype(np.float32), -in_block.astype(np.float32)]
        masks = [in_block]
    for w in _hgrn_levels(c, base):
        start = (p // (2 * w)) * (2 * w)
        ref = start + w - 1
        later = (p % (2 * w)) >= w
        sums.append(np.where(later, (r > ref) & (r <= p), (r > p) & (r <= ref)).astype(np.float32))
        masks.append(((r // (2 * w)) * (2 * w) == start) & later & ((r % (2 * w)) < w))
    sums.append((r > p).astype(np.float32))
    g = np.stack(sums)
    m = np.stack(masks).astype(np.float32)
    flip = lambda a: a[:, ::-1, ::-1]
    g = np.stack([g, flip(g)]).reshape(2, -1, c)
    m = np.stack([m, flip(m)]).reshape(2, -1, c)
    return jnp.asarray(np.concatenate([g] * HGRN_SPLIT, axis=-1), bf16), jnp.asarray(m, f32)


def _hgrn_block_sum(t_len):
    return jnp.asarray(np.arange(t_len)[None, :] // HGRN_BLOCK == np.arange(t_len // HGRN_BLOCK)[:, None], bf16)


def _hgrn_kernel(*refs, layer, t_len, latent):
    if latent:
        refs = refs[1:]
        (lb_ref, on_ref, gm_ref, mk_ref, gmf_ref, mkf_ref, bs_ref, q_ref, v_ref, ff_ref, fb_ref, g_ref, s_ref,
         out_ref, of_ref, ob_ref, qd_ref, u_ref, dec_ref, st_ref) = refs
        sout_ref = None
    else:
        (lb_ref, on_ref, gm_ref, mk_ref, gmf_ref, mkf_ref, bs_ref, q_ref, v_ref, ff_ref, fb_ref, g_ref,
         out_ref, sout_ref, of_ref, ob_ref, qd_ref, u_ref, dec_ref, st_ref) = refs
        s_ref = None
    c = HGRN_C
    hp = HGRN_HP
    n_chunks = t_len // c
    trans_b = (((1,), (1,)), ((), ()))
    heads = [slice(i * LANE, (i + 1) * LANE) for i in range(hp)]

    row = lax.broadcasted_iota(jnp.int32, (c, LANE), 0)

    if layer > 0:
        ps = [lb_ref[i] for i in range(DEPTH)]
        pmax = functools.reduce(jnp.maximum, ps)
        es = [jnp.exp(p - pmax) for p in ps]
        lb = functools.reduce(lambda a, b: a + b, es[1:layer + 1]) / functools.reduce(lambda a, b: a + b, es)
        log_lb = jnp.log(lb)
        log_1m = jnp.log1p(-lb)

    def log_forget(pre, d):
        ls = jnp.minimum(pre, 0.0) - jnp.log(1.0 + jnp.exp(-jnp.abs(pre)))
        if layer == 0:
            return ls
        a = log_lb[d:d + 1, :]
        cc = log_1m[d:d + 1, :] + ls
        return jnp.maximum(a, cc) + jnp.log(1.0 + jnp.exp(-jnp.abs(a - cc)))

    dirs = (0, 1)
    pre_refs = (ff_ref, fb_ref)
    oacc_refs = (of_ref, ob_ref)

    def stack(a):
        return jnp.concatenate([a[:, hs] for hs in heads], axis=0)

    def head_blocks(pp):
        return [pp[i * c:(i + 1) * c, i * c:(i + 1) * c] for i in range(hp)]

    def split(a):
        pieces, rem = [], a
        for _ in range(HGRN_SPLIT):
            pieces.append(rem.astype(bf16))
            rem = rem - pieces[-1].astype(f32)
        return jnp.concatenate(pieces, axis=0)

    def intra_body(ci, carry, *, fast):
        base = HGRN_BLOCK if fast else 1
        levels = _hgrn_levels(c, base)
        lv0 = 3 if fast else 1
        last = lv0 + len(levels)
        gsel, msk = (gmf_ref, mkf_ref) if fast else (gm_ref, mk_ref)
        sl = pl.ds(pl.multiple_of(ci * c, c), c)
        qx = q_ref[sl, :]
        q = qx * _sigmoid(qx)
        q_b = q.astype(bf16)
        v_b = v_ref[sl, :].astype(bf16)
        logf = [log_forget(pre_refs[d][sl, :], d) for d in dirs]
        k = [1.0 - jnp.exp(lf) for lf in logf]
        x = [jnp.exp(jnp.dot(gsel[d], split(logf[d]), preferred_element_type=f32)) for d in dirs]
        xb = lambda d, i: x[d][i * c:(i + 1) * c]
        for d in dirs:
            qd_ref[d, sl, :] = (q * xb(d, 0)).astype(bf16)
        kd = [(k[d] * xb(d, last)).astype(bf16) for d in dirs]

        if fast:
            pps = [lax.dot_general(stack((q * xb(d, 1)).astype(bf16)), stack((k[d] * xb(d, 2)).astype(bf16)),
                                   trans_b, preferred_element_type=f32) for d in dirs]
        else:
            qs = stack(q_b)
            pps = [lax.dot_general(qs, stack(k[d].astype(bf16)), trans_b, preferred_element_type=f32)
                   for d in dirs]
        attn = [[msk[d, 0:c, :] * blk for blk in head_blocks(pps[d])] for d in dirs]
        for li, w in enumerate(levels):
            rs = []
            for d in dirs:
                later = ((row % (2 * w)) < w) if d == 1 else ((row % (2 * w)) >= w)
                later = jnp.concatenate([later] * hp, axis=1)
                rs.append(stack((jnp.where(later, q, k[d]) * xb(d, lv0 + li)).astype(bf16)))
            pps = [lax.dot_general(r, r, trans_b, preferred_element_type=f32) for r in rs]
            for d in dirs:
                mask = msk[d, (1 + li) * c:(2 + li) * c, :]
                attn[d] = [a + mask * blk for a, blk in zip(attn[d], head_blocks(pps[d]))]
        for hh, hs in enumerate(heads):
            outs = [jnp.dot(attn[d][hh].astype(bf16), v_b[:, hs], preferred_element_type=f32) for d in dirs]
            us = [lax.dot_general(v_b[:, hs], kd[d][:, hs], (((0,), (0,)), ((), ())),
                                  preferred_element_type=f32) for d in dirs]
            for d in dirs:
                oacc_refs[d][sl, hs] = outs[d]
                u_ref[d, ci, hh] = us[d]
        for d in dirs:
            decay = x[d][0:1] if d == 1 else x[d][c - 1:c]
            dec_ref[d, ci] = jnp.broadcast_to(decay, (8, hp * LANE))
        return carry

    worst = [jnp.dot(bs_ref[...], jnp.maximum(-pre_refs[d][...], 0.0).astype(bf16), preferred_element_type=f32)
             for d in dirs]
    fast_ok = jnp.max(jnp.maximum(worst[0], worst[1])) < HGRN_FAST_LIMIT

    @pl.when(fast_ok)
    def _():
        lax.fori_loop(0, n_chunks, functools.partial(intra_body, fast=True), 0)

    @pl.when(jnp.logical_not(fast_ok))
    def _():
        lax.fori_loop(0, n_chunks, functools.partial(intra_body, fast=False), 0)

    for d in range(2):
        for hh in range(hp):
            if latent:
                st_ref[d, hh] = s_ref[0, 0, d, hh].T
            else:
                st_ref[d, hh] = jnp.zeros((DV_A, DK_A), f32)

    def scan(d, cidx, oacc_ref):
        sl = pl.ds(pl.multiple_of(cidx * c, c), c)
        dec = dec_ref[d, cidx]
        for hh, hs in enumerate(heads):
            st = st_ref[d, hh]
            oacc_ref[sl, hs] = oacc_ref[sl, hs] + lax.dot_general(
                qd_ref[d, sl, hs], st.astype(bf16), trans_b, preferred_element_type=f32)
            st_ref[d, hh] = st * dec[0:1, hs] + u_ref[d, cidx, hh]

    def scan_body(ci, carry):
        scan(0, ci, of_ref)
        scan(1, n_chunks - 1 - ci, ob_ref)
        return carry

    lax.fori_loop(0, n_chunks, scan_body, 0)

    if not latent:
        for d in range(2):
            for hh in range(hp):
                sout_ref[0, d, hh] = st_ref[d, hh].T

    gx = g_ref[...]
    gate = gx * _sigmoid(gx)
    for hs in heads:
        o = of_ref[:, hs] + ob_ref[:, hs]
        out_ref[:, hs] = (_rms(o, on_ref[...]) * gate[:, hs]).astype(out_ref.dtype)


def _hgrn(z, hgrn_lb, onorm, state, consts, *, layer, latent, prev=None):
    t_len = DEC_SEQ if latent else SEQ
    n_samples = DEC_BATCH if latent else BATCH
    row0 = (MC // t_len) if latent else 0
    (gmat, masks), (gmat_f, masks_f) = consts
    bsum = _hgrn_block_sum(t_len)
    hp = HGRN_HP
    wide = hp * LANE

    def zspec(col):
        base = col // wide
        return pl.BlockSpec((t_len, wide), lambda b, h: (row0 + b, base + h))

    in_specs = [pl.BlockSpec((DEPTH, 2, wide), lambda b, h: (0, 0, h)),
                pl.BlockSpec((1, LANE), lambda b, h: (0, 0)),
                pl.BlockSpec(gmat.shape, lambda b, h: (0, 0, 0)),
                pl.BlockSpec(masks.shape, lambda b, h: (0, 0, 0)),
                pl.BlockSpec(gmat_f.shape, lambda b, h: (0, 0, 0)),
                pl.BlockSpec(masks_f.shape, lambda b, h: (0, 0, 0)),
                pl.BlockSpec(bsum.shape, lambda b, h: (0, 0)),
                zspec(COL_AQ), zspec(COL_AV), zspec(COL_AFF), zspec(COL_AFB), zspec(COL_AG)]
    args = [hgrn_lb, onorm, gmat, masks, gmat_f, masks_f, bsum, z, z, z, z, z]
    o_spec = pl.BlockSpec((t_len, wide), lambda b, h: (row0 + b, h))
    o_shape = jax.ShapeDtypeStruct((M_ROWS, H_A * DV_A), bf16)
    aliases = {}
    if latent:
        in_specs.insert(0, pl.BlockSpec(memory_space=pl.ANY))
        args.insert(0, prev)
        aliases = {0: 0}
        in_specs.append(pl.BlockSpec((1, 1, 2, hp, DK_A, DV_A), lambda b, h: (b, layer, 0, h, 0, 0)))
        args.append(state)
        out_specs, out_shape = o_spec, o_shape
    else:
        out_specs = [o_spec, pl.BlockSpec((1, 2, hp, DK_A, DV_A), lambda b, h: (b, 0, h, 0, 0))]
        out_shape = [o_shape, jax.ShapeDtypeStruct((BATCH, 2, H_A, DK_A, DV_A), f32)]
    n_chunks = t_len // HGRN_C
    return pl.pallas_call(
        functools.partial(_hgrn_kernel, layer=layer, t_len=t_len, latent=latent),
        grid=(n_samples, H_A // hp),
        in_specs=in_specs,
        out_specs=out_specs,
        out_shape=out_shape,
        scratch_shapes=[pltpu.VMEM((t_len, wide), f32), pltpu.VMEM((t_len, wide), f32),
                        pltpu.VMEM((2, t_len, wide), bf16),
                        pltpu.VMEM((2, n_chunks, hp, DV_A, DK_A), f32),
                        pltpu.VMEM((2, n_chunks, 8, wide), f32),
                        pltpu.VMEM((2, hp, DV_A, DK_A), f32)],
        input_output_aliases=aliases,
        compiler_params=_cparams(("parallel", "parallel")),
        name="hgrn_lat" if latent else "hgrn_ctx",
    )(*args)


def _permute_wuq(wuq):
    w = wuq.reshape(DEPTH, Q_LORA, H_B, NOPE_B + ROPE_B)
    nope = w[..., :NOPE_B].reshape(DEPTH, Q_LORA, H_B * NOPE_B)
    pe = jnp.pad(w[..., NOPE_B:], ((0, 0), (0, 0), (0, 0), (0, LANE - ROPE_B))).reshape(DEPTH, Q_LORA, H_B * LANE)
    return jnp.concatenate([nope, pe], axis=-1).astype(bf16)


def _pad_lane(v):
    return jnp.pad(v, ((0, 0),) * (v.ndim - 1) + ((0, LANE - v.shape[-1]),))


def kernel(x_prompt, x_sample, state_hgrn, cache_mla_ckv, cache_mla_kpe, cache_gqa_k, cache_gqa_v, c, c_ctx,
           w_mod, b_mod, norm_w, ffn_w13, ffn_w2, w_in, hgrn_lb, hgrn_onorm, mla_qa_norm, mla_kva_norm,
           mla_wuq, mla_wukv, mla_nope_norm, mla_rope_norm, gqa_qk_norm, w_branch, w_out):
    w_in_t = jnp.swapaxes(w_in, 1, 2)
    wuq_b = _permute_wuq(mla_wuq)
    wukv_b = mla_wukv.astype(bf16)
    hgrn_consts = (_hgrn_consts(HGRN_C, 1), _hgrn_consts(HGRN_C, HGRN_BLOCK))
    rope_norm_p = _pad_lane(mla_rope_norm)
    cache_kpe_p = _pad_lane(cache_mla_kpe)
    cache_k = cache_gqa_k.reshape(DEC_BATCH, DEPTH, PAST_LEN, KVH_C * HD_C)
    cache_v = cache_gqa_v.reshape(DEC_BATCH, DEPTH, PAST_LEN, KVH_C * HD_C)
    tabs_m = tuple(jnp.asarray(t) for t in _rope_tables(ROPE_B))
    tabs_g_lat = tuple(jnp.asarray(t[PAST_LEN:]) for t in _rope_tables(HD_C))
    mla_scale = (NOPE_B + ROPE_B) ** -0.5
    gqa_scale = HD_C ** -0.5

    cond = jnp.zeros((N_MOD_ROWS, D_MODEL), f32).at[0].set(c_ctx).at[1:1 + DEC_BATCH].set(c)
    x = jnp.concatenate([x_prompt.reshape(MC, D_MODEL), x_sample.reshape(ML, D_MODEL)], axis=0)

    st_h, st_ckv, st_kpe, st_k, st_v = [], [], [], [], []
    for l in range(DEPTH):
        mod = _mm(cond, [(w_mod, (l,), 0)], n=9 * D_MODEL, tm=N_MOD_ROWS, tn=1024, out_dtype=f32, name="adaln",
                  pro="silu", epi="bias", bias=b_mod[l][None, :]).reshape(N_MOD_ROWS, 9, D_MODEL)

        def ffn(xin, i, ids):
            g = _mm(xin, [(ffn_w13, (l, i), 0), (ffn_w13, (l, i), D_FF)], n=D_FF, tm=1024, tn=512,
                    out_dtype=bf16, name="ffn_up", pro="normmod", epi="swiglu",
                    norm_w=norm_w[l, 2 * i][None, :], mod=mod, shift_id=ids[0], scale_id=ids[1])
            return _mm(g, [(ffn_w2, (l, i), 0)], n=D_MODEL, tm=1024, tn=256, out_dtype=f32, name="ffn_down",
                       epi="residual", res=xin, mod=mod, gate_id=ids[2], gate_scale=0.5)

        x = ffn(x, 0, (0, 1, 2))

        z = _mm(x, [(w_in_t, (l,), 0)], n=N_IN_PAD, tm=1024, tn=IN_TILE, out_dtype=f32, name="in_proj",
                pro="normmod", norm_w=norm_w[l, 1][None, :], mod=mod, shift_id=3, scale_id=4,
                w_rows=lambda j: (j * (IN_TILE // ROPE_B)
                                  - jnp.where(j >= IN_SHIFT_FROM, IN_SHIFT // ROPE_B, 0)) * ROPE_B)

        out_a, s_new = _hgrn(z, hgrn_lb, hgrn_onorm[l][None, :], None, hgrn_consts, layer=l, latent=False)
        out_a = _hgrn(z, hgrn_lb, hgrn_onorm[l][None, :], state_hgrn, hgrn_consts, layer=l, latent=True,
                      prev=out_a)
        st_h.append(s_new)

        q_mla, ckv, kpe = _mla_prep(z, mla_qa_norm[l][None, :], mla_kva_norm[l][None, :], rope_norm_p[l],
                                    mla_nope_norm[l, 0][None, :], wuq_b, l, tabs_m)
        nn_k = mla_nope_norm[l, 1][None, :]
        out_b = _mla_ctx(q_mla, ckv, kpe, wukv_b, l, nn_k, mla_scale)
        out_b = _mla_lat(out_b, q_mla, ckv, kpe, cache_mla_ckv, cache_kpe_p, wukv_b, l, nn_k, mla_scale)
        st_ckv.append(ckv[:MC].reshape(BATCH, SEQ, KV_LORA))
        st_kpe.append(kpe[:MC, :ROPE_B].reshape(BATCH, SEQ, ROPE_B))

        out_c, k_g = _gqa_ctx(z, gqa_qk_norm[l], gqa_scale)
        out_c = _gqa_lat(out_c, z, cache_k, cache_v, l, gqa_qk_norm[l], tabs_g_lat, gqa_scale)
        st_k.append(k_g.reshape(BATCH, SEQ, KVH_C, HD_C))
        st_v.append(z[:MC, COL_CV:COL_CV + KVH_C * HD_C].reshape(BATCH, SEQ, KVH_C, HD_C))

        merged = _merge(out_a, out_b, out_c, w_branch, l, z)
        x = _mm(merged, [(w_out, (l,), 0)], n=D_MODEL, tm=1024, tn=1024, out_dtype=f32, name="out_proj",
                epi="residual", res=x, mod=mod, gate_id=5, gate_scale=1.0)

        x = ffn(x, 1, (6, 7, 8))

    y_p = x[:MC].reshape(BATCH, SEQ, D_MODEL)
    y_s = x[MC:].reshape(DEC_BATCH, DEC_SEQ, D_MODEL)
    return (y_p, y_s,
            jnp.stack(st_h, axis=1), jnp.stack(st_ckv, axis=1), jnp.stack(st_kpe, axis=1),
            jnp.stack(st_k, axis=1), jnp.stack(st_v, axis=1))
```

```python
import functools

import numpy as np
import jax
import jax.numpy as jnp
from jax import lax
from jax.experimental import pallas as pl
from jax.experimental.pallas import tpu as pltpu

f32 = jnp.float32
bf16 = jnp.bfloat16

D_MODEL = 2048
BATCH = 16
SEQ = 256
DEPTH = 4
DEC_BATCH = 4
DEC_SEQ = 1024
PAST_LEN = 256
GRID_W = 64
ROPE_THETA = 10000.0
EPS = 1e-6
D_FF = 5632
N_BRANCH = 3
BR_W = 1024
H_A = 8
DK_A = 128
DV_A = 128
HK_A = H_A * DK_A
H_B = 8
Q_LORA = 512
KV_LORA = 256
NOPE_B = 128
ROPE_B = 64
V_B = 128
H_C = 8
KVH_C = 4
HD_C = 128

LANE = 128
MC = BATCH * SEQ
ML = DEC_BATCH * DEC_SEQ
M_ROWS = MC + ML
KV_SEQ = PAST_LEN + DEC_SEQ
KV_LAT_ROWS = DEC_BATCH * KV_SEQ
KV_ROWS = KV_LAT_ROWS + MC
N_MOD_ROWS = 16

COL_AQ, COL_AV, COL_AFF, COL_AFB, COL_AG = 0, 1024, 2048, 3072, 4096
COL_BQ = 5120
COL_BKV = 5632
COL_BPE = 5888
COL_CQ = 6144
COL_CK = 7168
COL_CV = 7680
COL_GBR = 8192
N_IN_PAD = 14336
IN_TILE = 1024
IN_SHIFT_FROM = COL_CQ // IN_TILE
IN_SHIFT = COL_CQ - (COL_BPE + ROPE_B)

ROW_TILE = 256
HGRN_C = 128
HGRN_HP = 2
HGRN_SPLIT = 2
HGRN_BLOCK = 32
HGRN_FAST_LIMIT = 60.0
VMEM_LIMIT = 56 * 1024 * 1024


def _cparams(sem):
    return pltpu.CompilerParams(dimension_semantics=sem, vmem_limit_bytes=VMEM_LIMIT)


def _sigmoid(x):
    return 1.0 / (1.0 + jnp.exp(-x))


def _rms(x, w, n=None):
    n = x.shape[-1] if n is None else n
    ms = jnp.sum(x * x, axis=-1, keepdims=True) * (1.0 / n)
    return x * lax.rsqrt(ms + EPS) * w


def _mod_row(i, tm):
    n_ctx = MC // tm
    per = DEC_SEQ // tm
    return jnp.where(i < n_ctx, 0, 1 + jnp.maximum(i - n_ctx, 0) // per)


def _mm_kernel(*refs, pro, epi, n_w, shift_id, scale_id, gate_id, gate_scale, w_rows, ahead):
    it = iter(refs)
    x_ref = next(it)
    nw_ref = next(it) if pro in ("norm", "normmod") else None
    modk_ref = next(it) if pro == "normmod" else None
    w_refs = [next(it) for _ in range(n_w)]
    res_ref = next(it) if epi == "residual" else None
    modn_ref = next(it) if epi == "residual" else None
    bias_ref = next(it) if epi == "bias" else None
    o_ref = next(it)
    h_ref = next(it) if pro is not None else None

    def prologue(dst):
        x = x_ref[...].astype(f32)
        if pro == "silu":
            y = x * _sigmoid(x)
        else:
            y = _rms(x, nw_ref[...])
            if pro == "normmod":
                y = y * (1.0 + modk_ref[0, scale_id:scale_id + 1, :]) + modk_ref[0, shift_id:shift_id + 1, :]
        h_ref[dst] = y.astype(bf16)

    def product(lhs):
        if w_rows is None:
            accs = [jnp.dot(lhs, w[...].astype(bf16), preferred_element_type=f32) for w in w_refs]
        else:
            accs = [lax.dot_general(lhs, w[(0,) * (len(w.shape) - 2)].astype(bf16), (((1,), (1,)), ((), ())),
                                    preferred_element_type=f32) for w in w_refs]
        if epi == "swiglu":
            a, u = accs
            out = a * _sigmoid(a) * u
        elif epi == "residual":
            out = res_ref[...] + (gate_scale * modn_ref[0, gate_id:gate_id + 1, :]) * accs[0]
        elif epi == "bias":
            out = accs[0] + bias_ref[...]
        else:
            out = accs[0]
        o_ref[...] = out.astype(o_ref.dtype)

    i, j = pl.program_id(0), pl.program_id(1)
    if pro is None:
        product(x_ref[...])
    elif not ahead:
        pl.when(j == 0)(lambda: prologue(0))
        product(h_ref[0])
    else:
        slot = i % 2
        pl.when((i == 0) & (j == 0))(lambda: prologue(0))
        with_next = (j == pl.num_programs(1) - 1) & (i < pl.num_programs(0) - 1)

        @pl.when(with_next)
        def _():
            product(h_ref[slot])
            prologue(1 - slot)

        @pl.when(jnp.logical_not(with_next))
        def _():
            product(h_ref[slot])


def _w_spec(w, lead, col0, k, tn):
    base = col0 // tn
    return pl.BlockSpec((None,) * len(lead) + (k, tn), lambda i, j: tuple(lead) + (0, base + j))


def _mm(x, ws, *, n, tm, tn, out_dtype, name, pro=None, epi="store", norm_w=None, mod=None,
        shift_id=0, scale_id=0, gate_id=0, gate_scale=1.0, res=None, bias=None, w_rows=None, ahead=False):
    m, k = x.shape
    grid = (m // tm, n // tn)

    def x_tile(i, j):
        if not ahead:
            return i
        return jnp.minimum(i + jnp.where(j == grid[1] - 1, 1, 0), grid[0] - 1)

    in_specs = [pl.BlockSpec((tm, k), lambda i, j: (x_tile(i, j), 0))]
    args = [x]
    if pro in ("norm", "normmod"):
        in_specs.append(pl.BlockSpec((1, k), lambda i, j: (0, 0)))
        args.append(norm_w)
    if pro == "normmod":
        in_specs.append(pl.BlockSpec((1, 9, k), lambda i, j: (_mod_row(x_tile(i, j), tm), 0, 0)))
        args.append(mod)
    for w, lead, col0 in ws:
        if w_rows is None:
            in_specs.append(_w_spec(w, lead, col0, k, tn))
        else:
            in_specs.append(pl.BlockSpec((pl.Element(1),) * len(lead) + (pl.Element(tn), pl.Element(k)),
                                         lambda i, j, lead=lead: tuple(lead) + (w_rows(j), 0)))
        args.append(w)
    if epi == "residual":
        in_specs.append(pl.BlockSpec((tm, tn), lambda i, j: (i, j)))
        args.append(res)
        in_specs.append(pl.BlockSpec((1, 9, tn), lambda i, j: (_mod_row(i, tm), 0, j)))
        args.append(mod)
    if epi == "bias":
        in_specs.append(pl.BlockSpec((1, tn), lambda i, j: (0, j)))
        args.append(bias)
    scratch = [pltpu.VMEM((2 if ahead else 1, tm, k), bf16)] if pro is not None else []
    kern = functools.partial(_mm_kernel, pro=pro, epi=epi, n_w=len(ws), shift_id=shift_id,
                             scale_id=scale_id, gate_id=gate_id, gate_scale=gate_scale, w_rows=w_rows, ahead=ahead)
    return pl.pallas_call(
        kern,
        grid=grid,
        in_specs=in_specs,
        out_specs=pl.BlockSpec((tm, tn), lambda i, j: (i, j)),
        out_shape=jax.ShapeDtypeStruct((m, n), out_dtype),
        scratch_shapes=scratch,
        compiler_params=_cparams(("arbitrary" if ahead else "parallel", "arbitrary")),
        name=name,
    )(*args)


def _merge_kernel(oa_ref, ob_ref, oc_ref, wb_ref, ga_ref, gb_ref, gc_ref, o_ref):
    acc = None
    for n, (o, g) in enumerate(((oa_ref, ga_ref), (ob_ref, gb_ref), (oc_ref, gc_ref))):
        br = jnp.dot(o[...], wb_ref[n].astype(bf16), preferred_element_type=f32)
        term = _sigmoid(g[...]) * br
        acc = term if acc is None else acc + term
    o_ref[...] = acc.astype(o_ref.dtype)


def _merge(out_a, out_b, out_c, wb, layer, z, *, tm=2048, tn=256):
    grid = (M_ROWS // tm, D_MODEL // tn)
    o_spec = pl.BlockSpec((tm, BR_W), lambda i, j: (i, 0))

    def g_spec(n):
        base = (COL_GBR + n * D_MODEL) // tn
        return pl.BlockSpec((tm, tn), lambda i, j: (i, base + j))

    return pl.pallas_call(
        _merge_kernel,
        grid=grid,
        in_specs=[o_spec, o_spec, o_spec,
                  pl.BlockSpec((None, N_BRANCH, BR_W, tn), lambda i, j: (layer, 0, 0, j)),
                  g_spec(0), g_spec(1), g_spec(2)],
        out_specs=pl.BlockSpec((tm, tn), lambda i, j: (i, j)),
        out_shape=jax.ShapeDtypeStruct((M_ROWS, D_MODEL), bf16),
        compiler_params=_cparams(("parallel", "arbitrary")),
        name="merge",
    )(out_a, out_b, out_c, wb, z, z, z)


def _rope(x, cos, sin, quarter):
    lane = lax.broadcasted_iota(jnp.int32, x.shape, 1)
    first = (lane % (2 * quarter)) < quarter
    partner = jnp.where(first, -pltpu.roll(x, LANE - quarter, axis=1), pltpu.roll(x, quarter, axis=1))
    return x * cos + partner * sin


def _rope_tables(width):
    quarter = width // 4
    t = np.arange(DEC_SEQ)
    inv = ROPE_THETA ** (-np.arange(quarter, dtype=np.float32) / quarter)
    ang_r = (t // GRID_W).astype(np.float32)[:, None] * inv[None, :]
    ang_c = (t % GRID_W).astype(np.float32)[:, None] * inv[None, :]
    ang = np.concatenate([ang_r, ang_r, ang_c, ang_c], axis=1).astype(np.float32)
    cos = np.ones((KV_SEQ, LANE), np.float32)
    sin = np.zeros((KV_SEQ, LANE), np.float32)
    cos[PAST_LEN:, :width] = np.cos(ang)
    sin[PAST_LEN:, :width] = np.sin(ang)
    return cos, sin


def _q_table_block(i):
    n_ctx = MC // ROW_TILE
    per = DEC_SEQ // ROW_TILE
    return jnp.where(i < n_ctx, 0, 1 + jnp.maximum(i - n_ctx, 0) % per)


LOG2_E = 1.4426950408889634


def _softmax_pv(s2, v_b):
    m = jnp.max(s2, axis=-1, keepdims=True)
    p = jnp.exp2(s2 - m)
    l = jnp.sum(p, axis=-1, keepdims=True)
    return jnp.dot(p.astype(bf16), v_b, preferred_element_type=f32) / l


_TRANS_B = (((1,), (1,)), ((), ()))


def _attend(qs, ks, vs):
    outs = []
    for i in range(0, len(qs), 2):
        ss = [lax.dot_general(q, k, _TRANS_B, preferred_element_type=f32)
              for q, k in zip(qs[i:i + 2], ks[i:i + 2])]
        outs += [_softmax_pv(s, v) for s, v in zip(ss, vs[i:i + 2])]
    return outs


def _mla_prep_kernel(bq_ref, bkv_ref, bpe_ref, qa_ref, kva_ref, rn_ref, nn_ref, wuq_ref, cm_ref, sm_ref,
                     q_ref, ckv_ref, kpe_ref):
    cm, sm = cm_ref[...], sm_ref[...]
    ckv_ref[...] = _rms(bkv_ref[...], kva_ref[...])
    bpe = bpe_ref[...]
    bpe = jnp.where(lax.broadcasted_iota(jnp.int32, bpe.shape, 1) < ROPE_B, bpe, 0.0)
    kpe_ref[...] = _rope(_rms(bpe, rn_ref[1:2, :], n=ROPE_B), cm, sm, ROPE_B // 4)
    cq = jnp.dot(_rms(bq_ref[...], qa_ref[...]).astype(bf16), wuq_ref[...], preferred_element_type=f32)
    for h in range(H_B):
        nope = _rms(cq[:, h * LANE:(h + 1) * LANE], nn_ref[...])
        pe = _rms(cq[:, (H_B + h) * LANE:(H_B + h + 1) * LANE], rn_ref[0:1, :], n=ROPE_B)
        q_ref[:, 2 * h * LANE:(2 * h + 1) * LANE] = nope.astype(bf16)
        q_ref[:, (2 * h + 1) * LANE:(2 * h + 2) * LANE] = _rope(pe, cm, sm, ROPE_B // 4).astype(bf16)


def _mla_prep(z, qa_norm, kva_norm, rope_norm, nope_norm_q, wuq, layer, tabs_m):
    tr = ROW_TILE
    width = 2 * H_B * LANE

    def zspec(col, w):
        return pl.BlockSpec((tr, w), lambda i: (i, col // w))

    def wspec(shape):
        return pl.BlockSpec(shape, lambda i: (0, 0))

    tspec = pl.BlockSpec((tr, LANE), lambda i: (_q_table_block(i), 0))
    return pl.pallas_call(
        _mla_prep_kernel,
        grid=(M_ROWS // tr,),
        in_specs=[zspec(COL_BQ, Q_LORA), zspec(COL_BKV, KV_LORA), zspec(COL_BPE, LANE),
                  wspec((1, Q_LORA)), wspec((1, KV_LORA)), wspec((2, LANE)), wspec((1, LANE)),
                  pl.BlockSpec((None, Q_LORA, width), lambda i: (layer, 0, 0)), tspec, tspec],
        out_specs=[pl.BlockSpec((tr, width), lambda i: (i, 0)),
                   pl.BlockSpec((tr, KV_LORA), lambda i: (i, 0)),
                   pl.BlockSpec((tr, LANE), lambda i: (i, 0))],
        out_shape=[jax.ShapeDtypeStruct((M_ROWS, width), bf16),
                   jax.ShapeDtypeStruct((M_ROWS, KV_LORA), f32),
                   jax.ShapeDtypeStruct((M_ROWS, LANE), f32)],
        compiler_params=_cparams(("parallel",)),
        name="mla_prep",
    )(z, z, z, qa_norm, kva_norm, rope_norm, nope_norm_q, wuq, *tabs_m)


def _mla_kv(ckv, kpe, w, nn, kscale):
    kv = jnp.dot(ckv.astype(bf16), w, preferred_element_type=f32)
    n_h = w.shape[1] // (2 * LANE)
    kpe_b = (kpe * kscale).astype(bf16)
    ks, vs = [], []
    for h in range(n_h):
        nope = _rms(kv[:, 2 * h * LANE:(2 * h + 1) * LANE], nn) * kscale
        ks.append(jnp.concatenate([nope.astype(bf16), kpe_b], axis=1))
        vs.append(kv[:, (2 * h + 1) * LANE:(2 * h + 2) * LANE].astype(bf16))
    return ks, vs


def _mla_ctx_kernel(q_ref, ckv_ref, kpe_ref, w_ref, nn_ref, o_ref, *, scale):
    ks, vs = _mla_kv(ckv_ref[...], kpe_ref[...], w_ref[...], nn_ref[...], scale * LOG2_E)
    for h in range(H_B):
        s = lax.dot_general(q_ref[:, 2 * h * LANE:(2 * h + 2) * LANE], ks[h], _TRANS_B,
                            preferred_element_type=f32)
        o_ref[:, h * V_B:(h + 1) * V_B] = _softmax_pv(s, vs[h]).astype(o_ref.dtype)


def _mla_ctx(q_mla, ckv, kpe, wukv, layer, nope_norm_k, scale):
    width = 2 * H_B * LANE
    return pl.pallas_call(
        functools.partial(_mla_ctx_kernel, scale=scale),
        grid=(BATCH,),
        in_specs=[pl.BlockSpec((SEQ, width), lambda b: (b, 0)),
                  pl.BlockSpec((SEQ, KV_LORA), lambda b: (b, 0)),
                  pl.BlockSpec((SEQ, LANE), lambda b: (b, 0)),
                  pl.BlockSpec((None, KV_LORA, width), lambda b: (layer, 0, 0)),
                  pl.BlockSpec((1, LANE), lambda b: (0, 0))],
        out_specs=pl.BlockSpec((SEQ, H_B * V_B), lambda b: (b, 0)),
        out_shape=jax.ShapeDtypeStruct((M_ROWS, H_B * V_B), bf16),
        compiler_params=_cparams(("parallel",)),
        name="mla_attn_ctx",
    )(q_mla, ckv, kpe, wukv, nope_norm_k)


def _mla_lat_kernel(prev_ref, q_ref, ckvc_ref, kpec_ref, ckvn_ref, kpen_ref, w_ref, nn_ref, o_ref,
                    k_scr, v_scr, *, scale):
    del prev_ref

    @pl.when(pl.program_id(2) == 0)
    def _():
        ks, vs = _mla_kv(ckvc_ref[...], kpec_ref[...], w_ref[...], nn_ref[...], scale * LOG2_E)
        k_scr[0:PAST_LEN, :] = ks[0]
        v_scr[0:PAST_LEN, :] = vs[0]
        ks, vs = _mla_kv(ckvn_ref[...], kpen_ref[...], w_ref[...], nn_ref[...], scale * LOG2_E)
        k_scr[PAST_LEN:KV_SEQ, :] = ks[0]
        v_scr[PAST_LEN:KV_SEQ, :] = vs[0]

    k_b, v_b = k_scr[...], v_scr[...]
    half = q_ref.shape[0] // 2
    qs = [q_ref[0:half, :], q_ref[half:2 * half, :]]
    for i, o in enumerate(_attend(qs, [k_b, k_b], [v_b, v_b])):
        o_ref[i * half:(i + 1) * half, :] = o.astype(o_ref.dtype)


def _mla_lat(prev, q_mla, ckv, kpe, cache_ckv, cache_kpe, wukv, layer, nope_norm_k, scale, *, tq=1024):
    nq = DEC_SEQ // tq
    lat0 = MC // DEC_SEQ
    return pl.pallas_call(
        functools.partial(_mla_lat_kernel, scale=scale),
        grid=(DEC_BATCH, H_B, nq),
        in_specs=[pl.BlockSpec(memory_space=pl.ANY),
                  pl.BlockSpec((tq, 2 * LANE), lambda b, h, i: (MC // tq + b * nq + i, h)),
                  pl.BlockSpec((None, None, PAST_LEN, KV_LORA), lambda b, h, i: (b, layer, 0, 0)),
                  pl.BlockSpec((None, None, PAST_LEN, LANE), lambda b, h, i: (b, layer, 0, 0)),
                  pl.BlockSpec((DEC_SEQ, KV_LORA), lambda b, h, i: (lat0 + b, 0)),
                  pl.BlockSpec((DEC_SEQ, LANE), lambda b, h, i: (lat0 + b, 0)),
                  pl.BlockSpec((None, KV_LORA, 2 * LANE), lambda b, h, i: (layer, 0, h)),
                  pl.BlockSpec((1, LANE), lambda b, h, i: (0, 0))],
        out_specs=pl.BlockSpec((tq, V_B), lambda b, h, i: (MC // tq + b * nq + i, h)),
        out_shape=jax.ShapeDtypeStruct((M_ROWS, H_B * V_B), bf16),
        scratch_shapes=[pltpu.VMEM((KV_SEQ, 2 * LANE), bf16), pltpu.VMEM((KV_SEQ, V_B), bf16)],
        input_output_aliases={0: 0},
        compiler_params=_cparams(("parallel", "parallel", "arbitrary")),
        name="mla_attn_lat",
    )(prev, q_mla, cache_ckv, cache_kpe, ckv, kpe, wukv, nope_norm_k)


def _gqa_ctx_kernel(cq_ref, ck_ref, cv_ref, qkn_ref, o_ref, kg_ref, *, scale):
    rep = H_C // KVH_C
    for g in range(KVH_C):
        gs = slice(g * HD_C, (g + 1) * HD_C)
        k = _rms(ck_ref[:, gs], qkn_ref[1:2, :])
        kg_ref[:, gs] = k
        k_b = (k * (scale * LOG2_E)).astype(bf16)
        v_b = cv_ref[:, gs].astype(bf16)
        for h in range(g * rep, (g + 1) * rep):
            hs = slice(h * HD_C, (h + 1) * HD_C)
            q = _rms(cq_ref[:, hs], qkn_ref[0:1, :]).astype(bf16)
            s = lax.dot_general(q, k_b, _TRANS_B, preferred_element_type=f32)
            o_ref[:, hs] = _softmax_pv(s, v_b).astype(o_ref.dtype)


def _gqa_ctx(z, qk_norm, scale):
    def zspec(col, w):
        return pl.BlockSpec((SEQ, w), lambda b: (b, col // w))

    return pl.pallas_call(
        functools.partial(_gqa_ctx_kernel, scale=scale),
        grid=(BATCH,),
        in_specs=[zspec(COL_CQ, H_C * HD_C), zspec(COL_CK, KVH_C * HD_C), zspec(COL_CV, KVH_C * HD_C),
                  pl.BlockSpec((2, HD_C), lambda b: (0, 0))],
        out_specs=[pl.BlockSpec((SEQ, H_C * HD_C), lambda b: (b, 0)),
                   pl.BlockSpec((SEQ, KVH_C * HD_C), lambda b: (b, 0))],
        out_shape=[jax.ShapeDtypeStruct((M_ROWS, H_C * HD_C), bf16),
                   jax.ShapeDtypeStruct((MC, KVH_C * HD_C), f32)],
        compiler_params=_cparams(("parallel",)),
        name="gqa_attn_ctx",
    )(z, z, z, qk_norm)


def _gqa_lat_kernel(prev_ref, cq_ref, ckn_ref, cvn_ref, kc_ref, vc_ref, qkn_ref, ck_ref, sk_ref, cqt_ref, sqt_ref,
                    o_ref, k_scr, v_scr, *, scale):
    del prev_ref
    rep = H_C // KVH_C

    @pl.when(pl.program_id(2) == 0)
    def _():
        kscale = scale * LOG2_E
        k_scr[0:PAST_LEN, :] = (kc_ref[...] * kscale).astype(bf16)
        v_scr[0:PAST_LEN, :] = vc_ref[...].astype(bf16)
        k = _rope(_rms(ckn_ref[...], qkn_ref[1:2, :]), ck_ref[...], sk_ref[...], HD_C // 4)
        k_scr[PAST_LEN:KV_SEQ, :] = (k * kscale).astype(bf16)
        v_scr[PAST_LEN:KV_SEQ, :] = cvn_ref[...].astype(bf16)

    cq, sq = cqt_ref[...], sqt_ref[...]
    k_b, v_b = k_scr[...], v_scr[...]
    half = cq_ref.shape[0] // 2
    for r in range(rep):
        rs = slice(r * HD_C, (r + 1) * HD_C)
        q = _rope(_rms(cq_ref[:, rs], qkn_ref[0:1, :]), cq, sq, HD_C // 4).astype(bf16)
        for i, o in enumerate(_attend([q[0:half], q[half:2 * half]], [k_b, k_b], [v_b, v_b])):
            o_ref[i * half:(i + 1) * half, rs] = o.astype(o_ref.dtype)


def _gqa_lat(prev, z, cache_k, cache_v, layer, qk_norm, tabs_lat, scale, *, tq=512):
    nq = DEC_SEQ // tq
    lat0 = MC // DEC_SEQ
    rep = H_C // KVH_C
    gw = rep * HD_C
    cos, sin = tabs_lat
    return pl.pallas_call(
        functools.partial(_gqa_lat_kernel, scale=scale),
        grid=(DEC_BATCH, KVH_C, nq),
        in_specs=[pl.BlockSpec(memory_space=pl.ANY),
                  pl.BlockSpec((tq, gw), lambda b, g, i: (MC // tq + b * nq + i, COL_CQ // gw + g)),
                  pl.BlockSpec((DEC_SEQ, HD_C), lambda b, g, i: (lat0 + b, COL_CK // HD_C + g)),
                  pl.BlockSpec((DEC_SEQ, HD_C), lambda b, g, i: (lat0 + b, COL_CV // HD_C + g)),
                  pl.BlockSpec((None, None, PAST_LEN, HD_C), lambda b, g, i: (b, layer, 0, g)),
                  pl.BlockSpec((None, None, PAST_LEN, HD_C), lambda b, g, i: (b, layer, 0, g)),
                  pl.BlockSpec((2, HD_C), lambda b, g, i: (0, 0)),
                  pl.BlockSpec((DEC_SEQ, LANE), lambda b, g, i: (0, 0)),
                  pl.BlockSpec((DEC_SEQ, LANE), lambda b, g, i: (0, 0)),
                  pl.BlockSpec((tq, LANE), lambda b, g, i: (i, 0)),
                  pl.BlockSpec((tq, LANE), lambda b, g, i: (i, 0))],
        out_specs=pl.BlockSpec((tq, gw), lambda b, g, i: (MC // tq + b * nq + i, g)),
        out_shape=jax.ShapeDtypeStruct((M_ROWS, H_C * HD_C), bf16),
        scratch_shapes=[pltpu.VMEM((KV_SEQ, HD_C), bf16), pltpu.VMEM((KV_SEQ, HD_C), bf16)],
        input_output_aliases={0: 0},
        compiler_params=_cparams(("parallel", "parallel", "arbitrary")),
        name="gqa_attn_lat",
    )(prev, z, z, z, cache_k, cache_v, qk_norm, cos, sin, cos, sin)


def _hgrn_levels(c, base):
    w, out = base, []
    while w < c:
        out.append(w)
        w *= 2
    return out


def _hgrn_consts(c, base):
    p = np.arange(c)[:, None]
    r = np.arange(c)[None, :]
    sums = [(r <= p).astype(np.float32)]
    if base == 1:
        masks = [r == p]
    else:
        in_block = ((r // base) == (p // base)) & (r <= p)
        sums += [in_block.astype(np.float32), -in_block.astype(np.float32)]
        masks = [in_block]
    for w in _hgrn_levels(c, base):
        start = (p // (2 * w)) * (2 * w)
        ref = start + w - 1
        later = (p % (2 * w)) >= w
        sums.append(np.where(later, (r > ref) & (r <= p), (r > p) & (r <= ref)).astype(np.float32))
        masks.append(((r // (2 * w)) * (2 * w) == start) & later & ((r % (2 * w)) < w))
    sums.append((r > p).astype(np.float32))
    g = np.stack(sums)
    m = np.stack(masks).astype(np.float32)
    flip = lambda a: a[:, ::-1, ::-1]
    g = np.stack([g, flip(g)]).reshape(2, -1, c)
    m = np.stack([m, flip(m)]).reshape(2, -1, c)
    return jnp.asarray(np.concatenate([g] * HGRN_SPLIT, axis=-1), bf16), jnp.asarray(m, f32)


def _hgrn_block_sum(t_len):
    return jnp.asarray(np.arange(t_len)[None, :] // HGRN_BLOCK == np.arange(t_len // HGRN_BLOCK)[:, None], bf16)


def _hgrn_kernel(*refs, layer, t_len, latent):
    if latent:
        refs = refs[1:]
        (lb_ref, on_ref, gm_ref, mk_ref, gmf_ref, mkf_ref, bs_ref, q_ref, v_ref, ff_ref, fb_ref, g_ref, s_ref,
         out_ref, of_ref, ob_ref, qd_ref, u_ref, dec_ref, st_ref) = refs
        sout_ref = None
    else:
        (lb_ref, on_ref, gm_ref, mk_ref, gmf_ref, mkf_ref, bs_ref, q_ref, v_ref, ff_ref, fb_ref, g_ref,
         out_ref, sout_ref, of_ref, ob_ref, qd_ref, u_ref, dec_ref, st_ref) = refs
        s_ref = None
    c = HGRN_C
    hp = HGRN_HP
    n_chunks = t_len // c
    trans_b = (((1,), (1,)), ((), ()))
    heads = [slice(i * LANE, (i + 1) * LANE) for i in range(hp)]

    row = lax.broadcasted_iota(jnp.int32, (c, LANE), 0)

    if layer > 0:
        ps = [lb_ref[i] for i in range(DEPTH)]
        pmax = functools.reduce(jnp.maximum, ps)
        es = [jnp.exp(p - pmax) for p in ps]
        lb = functools.reduce(lambda a, b: a + b, es[1:layer + 1]) / functools.reduce(lambda a, b: a + b, es)
        log_lb = jnp.log(lb)
        log_1m = jnp.log1p(-lb)

    def log_forget(pre, d):
        ls = jnp.minimum(pre, 0.0) - jnp.log(1.0 + jnp.exp(-jnp.abs(pre)))
        if layer == 0:
            return ls
        a = log_lb[d:d + 1, :]
        cc = log_1m[d:d + 1, :] + ls
        return jnp.maximum(a, cc) + jnp.log(1.0 + jnp.exp(-jnp.abs(a - cc)))

    dirs = (0, 1)
    pre_refs = (ff_ref, fb_ref)
    oacc_refs = (of_ref, ob_ref)

    def stack(a):
        return jnp.concatenate([a[:, hs] for hs in heads], axis=0)

    def head_blocks(pp):
        return [pp[i * c:(i + 1) * c, i * c:(i + 1) * c] for i in range(hp)]

    def split(a):
        pieces, rem = [], a
        for _ in range(HGRN_SPLIT):
            pieces.append(rem.astype(bf16))
            rem = rem - pieces[-1].astype(f32)
        return jnp.concatenate(pieces, axis=0)

    def intra_body(ci, carry, *, fast):
        base = HGRN_BLOCK if fast else 1
        levels = _hgrn_levels(c, base)
        lv0 = 3 if fast else 1
        last = lv0 + len(levels)
        gsel, msk = (gmf_ref, mkf_ref) if fast else (gm_ref, mk_ref)
        sl = pl.ds(pl.multiple_of(ci * c, c), c)
        qx = q_ref[sl, :]
        q = qx * _sigmoid(qx)
        q_b = q.astype(bf16)
        v_b = v_ref[sl, :].astype(bf16)
        logf = [log_forget(pre_refs[d][sl, :], d) for d in dirs]
        k = [1.0 - jnp.exp(lf) for lf in logf]
        x = [jnp.exp(jnp.dot(gsel[d], split(logf[d]), preferred_element_type=f32)) for d in dirs]
        xb = lambda d, i: x[d][i * c:(i + 1) * c]
        for d in dirs:
            qd_ref[d, sl, :] = (q * xb(d, 0)).astype(bf16)
        kd = [(k[d] * xb(d, last)).astype(bf16) for d in dirs]

        if fast:
            pps = [lax.dot_general(stack((q * xb(d, 1)).astype(bf16)), stack((k[d] * xb(d, 2)).astype(bf16)),
                                   trans_b, preferred_element_type=f32) for d in dirs]
        else:
            qs = stack(q_b)
            pps = [lax.dot_general(qs, stack(k[d].astype(bf16)), trans_b, preferred_element_type=f32)
                   for d in dirs]
        attn = [[msk[d, 0:c, :] * blk for blk in head_blocks(pps[d])] for d in dirs]
        for li, w in enumerate(levels):
            rs = []
            for d in dirs:
                later = ((row % (2 * w)) < w) if d == 1 else ((row % (2 * w)) >= w)
                later = jnp.concatenate([later] * hp, axis=1)
                rs.append(stack((jnp.where(later, q, k[d]) * xb(d, lv0 + li)).astype(bf16)))
            pps = [lax.dot_general(r, r, trans_b, preferred_element_type=f32) for r in rs]
            for d in dirs:
                mask = msk[d, (1 + li) * c:(2 + li) * c, :]
                attn[d] = [a + mask * blk for a, blk in zip(attn[d], head_blocks(pps[d]))]
        for hh, hs in enumerate(heads):
            outs = [jnp.dot(attn[d][hh].astype(bf16), v_b[:, hs], preferred_element_type=f32) for d in dirs]
            us = [lax.dot_general(v_b[:, hs], kd[d][:, hs], (((0,), (0,)), ((), ())),
                                  preferred_element_type=f32) for d in dirs]
            for d in dirs:
                oacc_refs[d][sl, hs] = outs[d]
                u_ref[d, ci, hh] = us[d]
        for d in dirs:
            decay = x[d][0:1] if d == 1 else x[d][c - 1:c]
            dec_ref[d, ci] = jnp.broadcast_to(decay, (8, hp * LANE))
        return carry

    worst = [jnp.dot(bs_ref[...], jnp.maximum(-pre_refs[d][...], 0.0).astype(bf16), preferred_element_type=f32)
             for d in dirs]
    fast_ok = jnp.max(jnp.maximum(worst[0], worst[1])) < HGRN_FAST_LIMIT

    @pl.when(fast_ok)
    def _():
        lax.fori_loop(0, n_chunks, functools.partial(intra_body, fast=True), 0)

    @pl.when(jnp.logical_not(fast_ok))
    def _():
        lax.fori_loop(0, n_chunks, functools.partial(intra_body, fast=False), 0)

    for d in range(2):
        for hh in range(hp):
            if latent:
                st_ref[d, hh] = s_ref[0, 0, d, hh].T
            else:
                st_ref[d, hh] = jnp.zeros((DV_A, DK_A), f32)

    def scan(d, cidx, oacc_ref):
        sl = pl.ds(pl.multiple_of(cidx * c, c), c)
        dec = dec_ref[d, cidx]
        for hh, hs in enumerate(heads):
            st = st_ref[d, hh]
            oacc_ref[sl, hs] = oacc_ref[sl, hs] + lax.dot_general(
                qd_ref[d, sl, hs], st.astype(bf16), trans_b, preferred_element_type=f32)
            st_ref[d, hh] = st * dec[0:1, hs] + u_ref[d, cidx, hh]

    def scan_body(ci, carry):
        scan(0, ci, of_ref)
        scan(1, n_chunks - 1 - ci, ob_ref)
        return carry

    lax.fori_loop(0, n_chunks, scan_body, 0)

    if not latent:
        for d in range(2):
            for hh in range(hp):
                sout_ref[0, d, hh] = st_ref[d, hh].T

    gx = g_ref[...]
    gate = gx * _sigmoid(gx)
    for hs in heads:
        o = of_ref[:, hs] + ob_ref[:, hs]
        out_ref[:, hs] = (_rms(o, on_ref[...]) * gate[:, hs]).astype(out_ref.dtype)


def _hgrn(z, hgrn_lb, onorm, state, consts, *, layer, latent, prev=None):
    t_len = DEC_SEQ if latent else SEQ
    n_samples = DEC_BATCH if latent else BATCH
    row0 = (MC // t_len) if latent else 0
    (gmat, masks), (gmat_f, masks_f) = consts
    bsum = _hgrn_block_sum(t_len)
    hp = HGRN_HP
    wide = hp * LANE

    def zspec(col):
        base = col // wide
        return pl.BlockSpec((t_len, wide), lambda b, h: (row0 + b, base + h))

    in_specs = [pl.BlockSpec((DEPTH, 2, wide), lambda b, h: (0, 0, h)),
                pl.BlockSpec((1, LANE), lambda b, h: (0, 0)),
                pl.BlockSpec(gmat.shape, lambda b, h: (0, 0, 0)),
                pl.BlockSpec(masks.shape, lambda b, h: (0, 0, 0)),
                pl.BlockSpec(gmat_f.shape, lambda b, h: (0, 0, 0)),
                pl.BlockSpec(masks_f.shape, lambda b, h: (0, 0, 0)),
                pl.BlockSpec(bsum.shape, lambda b, h: (0, 0)),
                zspec(COL_AQ), zspec(COL_AV), zspec(COL_AFF), zspec(COL_AFB), zspec(COL_AG)]
    args = [hgrn_lb, onorm, gmat, masks, gmat_f, masks_f, bsum, z, z, z, z, z]
    o_spec = pl.BlockSpec((t_len, wide), lambda b, h: (row0 + b, h))
    o_shape = jax.ShapeDtypeStruct((M_ROWS, H_A * DV_A), bf16)
    aliases = {}
    if latent:
        in_specs.insert(0, pl.BlockSpec(memory_space=pl.ANY))
        args.insert(0, prev)
        aliases = {0: 0}
        in_specs.append(pl.BlockSpec((1, 1, 2, hp, DK_A, DV_A), lambda b, h: (b, layer, 0, h, 0, 0)))
        args.append(state)
        out_specs, out_shape = o_spec, o_shape
    else:
        out_specs = [o_spec, pl.BlockSpec((1, 2, hp, DK_A, DV_A), lambda b, h: (b, 0, h, 0, 0))]
        out_shape = [o_shape, jax.ShapeDtypeStruct((BATCH, 2, H_A, DK_A, DV_A), f32)]
    n_chunks = t_len // HGRN_C
    return pl.pallas_call(
        functools.partial(_hgrn_kernel, layer=layer, t_len=t_len, latent=latent),
        grid=(n_samples, H_A // hp),
        in_specs=in_specs,
        out_specs=out_specs,
        out_shape=out_shape,
        scratch_shapes=[pltpu.VMEM((t_len, wide), f32), pltpu.VMEM((t_len, wide), f32),
                        pltpu.VMEM((2, t_len, wide), bf16),
                        pltpu.VMEM((2, n_chunks, hp, DV_A, DK_A), f32),
                        pltpu.VMEM((2, n_chunks, 8, wide), f32),
                        pltpu.VMEM((2, hp, DV_A, DK_A), f32)],
        input_output_aliases=aliases,
        compiler_params=_cparams(("parallel", "parallel")),
        name="hgrn_lat" if latent else "hgrn_ctx",
    )(*args)


def _permute_wuq(wuq):
    w = wuq.reshape(DEPTH, Q_LORA, H_B, NOPE_B + ROPE_B)
    nope = w[..., :NOPE_B].reshape(DEPTH, Q_LORA, H_B * NOPE_B)
    pe = jnp.pad(w[..., NOPE_B:], ((0, 0), (0, 0), (0, 0), (0, LANE - ROPE_B))).reshape(DEPTH, Q_LORA, H_B * LANE)
    return jnp.concatenate([nope, pe], axis=-1).astype(bf16)


def _pad_lane(v):
    return jnp.pad(v, ((0, 0),) * (v.ndim - 1) + ((0, LANE - v.shape[-1]),))


def kernel(x_prompt, x_sample, state_hgrn, cache_mla_ckv, cache_mla_kpe, cache_gqa_k, cache_gqa_v, c, c_ctx,
           w_mod, b_mod, norm_w, ffn_w13, ffn_w2, w_in, hgrn_lb, hgrn_onorm, mla_qa_norm, mla_kva_norm,
           mla_wuq, mla_wukv, mla_nope_norm, mla_rope_norm, gqa_qk_norm, w_branch, w_out):
    w_in_t = jnp.swapaxes(w_in, 1, 2)
    wuq_b = _permute_wuq(mla_wuq)
    wukv_b = mla_wukv.astype(bf16)
    hgrn_consts = (_hgrn_consts(HGRN_C, 1), _hgrn_consts(HGRN_C, HGRN_BLOCK))
    rope_norm_p = _pad_lane(mla_rope_norm)
    cache_kpe_p = _pad_lane(cache_mla_kpe)
    cache_k = cache_gqa_k.reshape(DEC_BATCH, DEPTH, PAST_LEN, KVH_C * HD_C)
    cache_v = cache_gqa_v.reshape(DEC_BATCH, DEPTH, PAST_LEN, KVH_C * HD_C)
    tabs_m = tuple(jnp.asarray(t) for t in _rope_tables(ROPE_B))
    tabs_g_lat = tuple(jnp.asarray(t[PAST_LEN:]) for t in _rope_tables(HD_C))
    mla_scale = (NOPE_B + ROPE_B) ** -0.5
    gqa_scale = HD_C ** -0.5

    cond = jnp.zeros((N_MOD_ROWS, D_MODEL), f32).at[0].set(c_ctx).at[1:1 + DEC_BATCH].set(c)
    x = jnp.concatenate([x_prompt.reshape(MC, D_MODEL), x_sample.reshape(ML, D_MODEL)], axis=0)

    st_h, st_ckv, st_kpe, st_k, st_v = [], [], [], [], []
    for l in range(DEPTH):
        mod = _mm(cond, [(w_mod, (l,), 0)], n=9 * D_MODEL, tm=N_MOD_ROWS, tn=1024, out_dtype=f32, name="adaln",
                  pro="silu", epi="bias", bias=b_mod[l][None, :]).reshape(N_MOD_ROWS, 9, D_MODEL)

        def ffn(xin, i, ids):
            g = _mm(xin, [(ffn_w13, (l, i), 0), (ffn_w13, (l, i), D_FF)], n=D_FF, tm=1024, tn=512,
                    out_dtype=bf16, name="ffn_up", pro="normmod", epi="swiglu", ahead=True,
                    norm_w=norm_w[l, 2 * i][None, :], mod=mod, shift_id=ids[0], scale_id=ids[1])
            return _mm(g, [(ffn_w2, (l, i), 0)], n=D_MODEL, tm=1024, tn=256, out_dtype=f32, name="ffn_down",
                       epi="residual", res=xin, mod=mod, gate_id=ids[2], gate_scale=0.5)

        x = ffn(x, 0, (0, 1, 2))

        z = _mm(x, [(w_in_t, (l,), 0)], n=N_IN_PAD, tm=1024, tn=IN_TILE, out_dtype=f32, name="in_proj",
                pro="normmod", norm_w=norm_w[l, 1][None, :], mod=mod, shift_id=3, scale_id=4,
                w_rows=lambda j: (j * (IN_TILE // ROPE_B)
                                  - jnp.where(j >= IN_SHIFT_FROM, IN_SHIFT // ROPE_B, 0)) * ROPE_B)

        out_a, s_new = _hgrn(z, hgrn_lb, hgrn_onorm[l][None, :], None, hgrn_consts, layer=l, latent=False)
        out_a = _hgrn(z, hgrn_lb, hgrn_onorm[l][None, :], state_hgrn, hgrn_consts, layer=l, latent=True,
                      prev=out_a)
        st_h.append(s_new)

        q_mla, ckv, kpe = _mla_prep(z, mla_qa_norm[l][None, :], mla_kva_norm[l][None, :], rope_norm_p[l],
                                    mla_nope_norm[l, 0][None, :], wuq_b, l, tabs_m)
        nn_k = mla_nope_norm[l, 1][None, :]
        out_b = _mla_ctx(q_mla, ckv, kpe, wukv_b, l, nn_k, mla_scale)
        out_b = _mla_lat(out_b, q_mla, ckv, kpe, cache_mla_ckv, cache_kpe_p, wukv_b, l, nn_k, mla_scale)
        st_ckv.append(ckv[:MC].reshape(BATCH, SEQ, KV_LORA))
        st_kpe.append(kpe[:MC, :ROPE_B].reshape(BATCH, SEQ, ROPE_B))

        out_c, k_g = _gqa_ctx(z, gqa_qk_norm[l], gqa_scale)
        out_c = _gqa_lat(out_c, z, cache_k, cache_v, l, gqa_qk_norm[l], tabs_g_lat, gqa_scale)
        st_k.append(k_g.reshape(BATCH, SEQ, KVH_C, HD_C))
        st_v.append(z[:MC, COL_CV:COL_CV + KVH_C * HD_C].reshape(BATCH, SEQ, KVH_C, HD_C))

        merged = _merge(out_a, out_b, out_c, w_branch, l, z)
        x = _mm(merged, [(w_out, (l,), 0)], n=D_MODEL, tm=1024, tn=1024, out_dtype=f32, name="out_proj",
                epi="residual", res=x, mod=mod, gate_id=5, gate_scale=1.0)

        x = ffn(x, 1, (6, 7, 8))

    y_p = x[:MC].reshape(BATCH, SEQ, D_MODEL)
    y_s = x[MC:].reshape(DEC_BATCH, DEC_SEQ, D_MODEL)
    return (y_p, y_s,
            jnp.stack(st_h, axis=1), jnp.stack(st_ckv, axis=1), jnp.stack(st_kpe, axis=1),
            jnp.stack(st_k, axis=1), jnp.stack(st_v, axis=1))
```

```python
import functools

import numpy as np
import jax
import jax.numpy as jnp
from jax import lax
from jax.experimental import pallas as pl
from jax.experimental.pallas import tpu as pltpu

f32 = jnp.float32
bf16 = jnp.bfloat16

D_MODEL = 2048
BATCH = 16
SEQ = 256
DEPTH = 4
DEC_BATCH = 4
DEC_SEQ = 1024
PAST_LEN = 256
GRID_W = 64
ROPE_THETA = 10000.0
EPS = 1e-6
D_FF = 5632
N_BRANCH = 3
BR_W = 1024
H_A = 8
DK_A = 128
DV_A = 128
H_B = 8
Q_LORA = 512
KV_LORA = 256
NOPE_B = 128
ROPE_B = 64
V_B = 128
H_C = 8
KVH_C = 4
HD_C = 128

LANE = 128
MC = BATCH * SEQ
ML = DEC_BATCH * DEC_SEQ
M_ROWS = MC + ML
KV_SEQ = PAST_LEN + DEC_SEQ
N_MOD_ROWS = 16

COL_AQ, COL_AV, COL_AFF, COL_AFB, COL_AG = 0, 1024, 2048, 3072, 4096
COL_BQ = 5120
COL_BKV = 5632
COL_BPE = 5888
COL_CQ = 6144
COL_CK = 7168
COL_CV = 7680
COL_GBR = 8192
N_IN_PAD = 14336
IN_TILE = 1024
IN_SHIFT_FROM = COL_CQ // IN_TILE
IN_SHIFT = COL_CQ - (COL_BPE + ROPE_B)

ROW_TILE = 256
HGRN_C = 128
HGRN_HP = 2
HGRN_SPLIT = 2
HGRN_BLOCK = 32
HGRN_FAST_LIMIT = 60.0
VMEM_LIMIT = 56 * 1024 * 1024


def _cparams(sem):
    return pltpu.CompilerParams(dimension_semantics=sem, vmem_limit_bytes=VMEM_LIMIT)


def _sigmoid(x):
    return 1.0 / (1.0 + jnp.exp(-x))


def _rms(x, w, n=None):
    n = x.shape[-1] if n is None else n
    ms = jnp.sum(x * x, axis=-1, keepdims=True) * (1.0 / n)
    return x * lax.rsqrt(ms + EPS) * w


def _mod_row(i, tm):
    n_ctx = MC // tm
    per = DEC_SEQ // tm
    return jnp.where(i < n_ctx, 0, 1 + jnp.maximum(i - n_ctx, 0) // per)


def _mm_kernel(*refs, pro, epi, n_w, shift_id, scale_id, gate_id, gate_scale, w_rows, ahead):
    it = iter(refs)
    x_ref = next(it)
    nw_ref = next(it) if pro in ("norm", "normmod") else None
    modk_ref = next(it) if pro == "normmod" else None
    w_refs = [next(it) for _ in range(n_w)]
    res_ref = next(it) if epi == "residual" else None
    modn_ref = next(it) if epi == "residual" else None
    bias_ref = next(it) if epi == "bias" else None
    o_ref = next(it)
    h_ref = next(it) if pro is not None else None

    def prologue(dst):
        x = x_ref[...].astype(f32)
        if pro == "silu":
            y = x * _sigmoid(x)
        else:
            y = _rms(x, nw_ref[...])
            if pro == "normmod":
                y = y * (1.0 + modk_ref[0, scale_id:scale_id + 1, :]) + modk_ref[0, shift_id:shift_id + 1, :]
        h_ref[dst] = y.astype(bf16)

    def product(lhs):
        if w_rows is None:
            accs = [jnp.dot(lhs, w[...].astype(bf16), preferred_element_type=f32) for w in w_refs]
        else:
            accs = [lax.dot_general(lhs, w[(0,) * (len(w.shape) - 2)].astype(bf16), (((1,), (1,)), ((), ())),
                                    preferred_element_type=f32) for w in w_refs]
        if epi == "swiglu":
            a, u = accs
            out = a * _sigmoid(a) * u
        elif epi == "residual":
            out = res_ref[...] + (gate_scale * modn_ref[0, gate_id:gate_id + 1, :]) * accs[0]
        elif epi == "bias":
            out = accs[0] + bias_ref[...]
        else:
            out = accs[0]
        o_ref[...] = out.astype(o_ref.dtype)

    i, j = pl.program_id(0), pl.program_id(1)
    if pro is None:
        product(x_ref[...])
    elif not ahead:
        pl.when(j == 0)(lambda: prologue(0))
        product(h_ref[0])
    else:
        slot = i % 2
        pl.when((i == 0) & (j == 0))(lambda: prologue(0))
        with_next = (j == pl.num_programs(1) - 1) & (i < pl.num_programs(0) - 1)

        @pl.when(with_next)
        def _():
            product(h_ref[slot])
            prologue(1 - slot)

        @pl.when(jnp.logical_not(with_next))
        def _():
            product(h_ref[slot])


def _w_spec(w, lead, col0, k, tn):
    base = col0 // tn
    return pl.BlockSpec((None,) * len(lead) + (k, tn), lambda i, j: tuple(lead) + (0, base + j))


def _mm(x, ws, *, n, tm, tn, out_dtype, name, pro=None, epi="store", norm_w=None, mod=None,
        shift_id=0, scale_id=0, gate_id=0, gate_scale=1.0, res=None, bias=None, w_rows=None, ahead=False):
    m, k = x.shape
    grid = (m // tm, n // tn)

    def x_tile(i, j):
        if not ahead:
            return i
        return jnp.minimum(i + jnp.where(j == grid[1] - 1, 1, 0), grid[0] - 1)

    in_specs = [pl.BlockSpec((tm, k), lambda i, j: (x_tile(i, j), 0))]
    args = [x]
    if pro in ("norm", "normmod"):
        in_specs.append(pl.BlockSpec((1, k), lambda i, j: (0, 0)))
        args.append(norm_w)
    if pro == "normmod":
        in_specs.append(pl.BlockSpec((1, 9, k), lambda i, j: (_mod_row(x_tile(i, j), tm), 0, 0)))
        args.append(mod)
    for w, lead, col0 in ws:
        if w_rows is None:
            in_specs.append(_w_spec(w, lead, col0, k, tn))
        else:
            in_specs.append(pl.BlockSpec((pl.Element(1),) * len(lead) + (pl.Element(tn), pl.Element(k)),
                                         lambda i, j, lead=lead: tuple(lead) + (w_rows(j), 0)))
        args.append(w)
    if epi == "residual":
        in_specs.append(pl.BlockSpec((tm, tn), lambda i, j: (i, j)))
        args.append(res)
        in_specs.append(pl.BlockSpec((1, 9, tn), lambda i, j: (_mod_row(i, tm), 0, j)))
        args.append(mod)
    if epi == "bias":
        in_specs.append(pl.BlockSpec((1, tn), lambda i, j: (0, j)))
        args.append(bias)
    scratch = [pltpu.VMEM((2 if ahead else 1, tm, k), bf16)] if pro is not None else []
    kern = functools.partial(_mm_kernel, pro=pro, epi=epi, n_w=len(ws), shift_id=shift_id,
                             scale_id=scale_id, gate_id=gate_id, gate_scale=gate_scale, w_rows=w_rows, ahead=ahead)
    return pl.pallas_call(
        kern,
        grid=grid,
        in_specs=in_specs,
        out_specs=pl.BlockSpec((tm, tn), lambda i, j: (i, j)),
        out_shape=jax.ShapeDtypeStruct((m, n), out_dtype),
        scratch_shapes=scratch,
        compiler_params=_cparams(("arbitrary" if ahead else "parallel", "arbitrary")),
        name=name,
    )(*args)


def _merge_kernel(oa_ref, ob_ref, oc_ref, wb_ref, ga_ref, gb_ref, gc_ref, o_ref):
    acc = None
    for n, (o, g) in enumerate(((oa_ref, ga_ref), (ob_ref, gb_ref), (oc_ref, gc_ref))):
        br = jnp.dot(o[...], wb_ref[n].astype(bf16), preferred_element_type=f32)
        term = _sigmoid(g[...]) * br
        acc = term if acc is None else acc + term
    o_ref[...] = acc.astype(o_ref.dtype)


def _merge(out_a, out_b, out_c, wb, layer, z, *, tm=2048, tn=256):
    grid = (M_ROWS // tm, D_MODEL // tn)
    o_spec = pl.BlockSpec((tm, BR_W), lambda i, j: (i, 0))

    def g_spec(n):
        base = (COL_GBR + n * D_MODEL) // tn
        return pl.BlockSpec((tm, tn), lambda i, j: (i, base + j))

    return pl.pallas_call(
        _merge_kernel,
        grid=grid,
        in_specs=[o_spec, o_spec, o_spec,
                  pl.BlockSpec((None, N_BRANCH, BR_W, tn), lambda i, j: (layer, 0, 0, j)),
                  g_spec(0), g_spec(1), g_spec(2)],
        out_specs=pl.BlockSpec((tm, tn), lambda i, j: (i, j)),
        out_shape=jax.ShapeDtypeStruct((M_ROWS, D_MODEL), bf16),
        compiler_params=_cparams(("parallel", "arbitrary")),
        name="merge",
    )(out_a, out_b, out_c, wb, z, z, z)


def _rope(x, cos, sin, quarter):
    lane = lax.broadcasted_iota(jnp.int32, x.shape, 1)
    first = (lane % (2 * quarter)) < quarter
    partner = jnp.where(first, -pltpu.roll(x, LANE - quarter, axis=1), pltpu.roll(x, quarter, axis=1))
    return x * cos + partner * sin


def _rope_tables(width):
    quarter = width // 4
    t = np.arange(DEC_SEQ)
    inv = ROPE_THETA ** (-np.arange(quarter, dtype=np.float32) / quarter)
    ang_r = (t // GRID_W).astype(np.float32)[:, None] * inv[None, :]
    ang_c = (t % GRID_W).astype(np.float32)[:, None] * inv[None, :]
    ang = np.concatenate([ang_r, ang_r, ang_c, ang_c], axis=1).astype(np.float32)
    cos = np.ones((KV_SEQ, LANE), np.float32)
    sin = np.zeros((KV_SEQ, LANE), np.float32)
    cos[PAST_LEN:, :width] = np.cos(ang)
    sin[PAST_LEN:, :width] = np.sin(ang)
    return cos, sin


def _q_table_block(i):
    n_ctx = MC // ROW_TILE
    per = DEC_SEQ // ROW_TILE
    return jnp.where(i < n_ctx, 0, 1 + jnp.maximum(i - n_ctx, 0) % per)


LOG2_E = 1.4426950408889634


def _softmax_pv(s2, v_b):
    m = jnp.max(s2, axis=-1, keepdims=True)
    p = jnp.exp2(s2 - m)
    l = jnp.sum(p, axis=-1, keepdims=True)
    return jnp.dot(p.astype(bf16), v_b, preferred_element_type=f32) / l


_TRANS_B = (((1,), (1,)), ((), ()))


def _attend(qs, ks, vs):
    outs = []
    for i in range(0, len(qs), 2):
        ss = [lax.dot_general(q, k, _TRANS_B, preferred_element_type=f32)
              for q, k in zip(qs[i:i + 2], ks[i:i + 2])]
        outs += [_softmax_pv(s, v) for s, v in zip(ss, vs[i:i + 2])]
    return outs


def _mla_prep_kernel(bq_ref, bkv_ref, bpe_ref, qa_ref, kva_ref, rn_ref, nn_ref, wuq_ref, cm_ref, sm_ref,
                     q_ref, ckv_ref, kpe_ref):
    cm, sm = cm_ref[...], sm_ref[...]
    ckv_ref[...] = _rms(bkv_ref[...], kva_ref[...])
    bpe = bpe_ref[...]
    bpe = jnp.where(lax.broadcasted_iota(jnp.int32, bpe.shape, 1) < ROPE_B, bpe, 0.0)
    kpe_ref[...] = _rope(_rms(bpe, rn_ref[1:2, :], n=ROPE_B), cm, sm, ROPE_B // 4)
    cq = jnp.dot(_rms(bq_ref[...], qa_ref[...]).astype(bf16), wuq_ref[...], preferred_element_type=f32)
    for h in range(H_B):
        nope = _rms(cq[:, h * LANE:(h + 1) * LANE], nn_ref[...])
        pe = _rms(cq[:, (H_B + h) * LANE:(H_B + h + 1) * LANE], rn_ref[0:1, :], n=ROPE_B)
        q_ref[:, 2 * h * LANE:(2 * h + 1) * LANE] = nope.astype(bf16)
        q_ref[:, (2 * h + 1) * LANE:(2 * h + 2) * LANE] = _rope(pe, cm, sm, ROPE_B // 4).astype(bf16)


def _mla_prep(z, qa_norm, kva_norm, rope_norm, nope_norm_q, wuq, layer, tabs_m):
    tr = ROW_TILE
    width = 2 * H_B * LANE

    def zspec(col, w):
        return pl.BlockSpec((tr, w), lambda i: (i, col // w))

    def wspec(shape):
        return pl.BlockSpec(shape, lambda i: (0, 0))

    tspec = pl.BlockSpec((tr, LANE), lambda i: (_q_table_block(i), 0))
    return pl.pallas_call(
        _mla_prep_kernel,
        grid=(M_ROWS // tr,),
        in_specs=[zspec(COL_BQ, Q_LORA), zspec(COL_BKV, KV_LORA), zspec(COL_BPE, LANE),
                  wspec((1, Q_LORA)), wspec((1, KV_LORA)), wspec((2, LANE)), wspec((1, LANE)),
                  pl.BlockSpec((None, Q_LORA, width), lambda i: (layer, 0, 0)), tspec, tspec],
        out_specs=[pl.BlockSpec((tr, width), lambda i: (i, 0)),
                   pl.BlockSpec((tr, KV_LORA), lambda i: (i, 0)),
                   pl.BlockSpec((tr, LANE), lambda i: (i, 0))],
        out_shape=[jax.ShapeDtypeStruct((M_ROWS, width), bf16),
                   jax.ShapeDtypeStruct((M_ROWS, KV_LORA), f32),
                   jax.ShapeDtypeStruct((M_ROWS, LANE), f32)],
        compiler_params=_cparams(("parallel",)),
        name="mla_prep",
    )(z, z, z, qa_norm, kva_norm, rope_norm, nope_norm_q, wuq, *tabs_m)


def _mla_kv(ckv, kpe, w, nn, kscale):
    kv = jnp.dot(ckv.astype(bf16), w, preferred_element_type=f32)
    n_h = w.shape[1] // (2 * LANE)
    kpe_b = (kpe * kscale).astype(bf16)
    ks, vs = [], []
    for h in range(n_h):
        nope = _rms(kv[:, 2 * h * LANE:(2 * h + 1) * LANE], nn) * kscale
        ks.append(jnp.concatenate([nope.astype(bf16), kpe_b], axis=1))
        vs.append(kv[:, (2 * h + 1) * LANE:(2 * h + 2) * LANE].astype(bf16))
    return ks, vs


def _mla_ctx_kernel(prev_ref, q_ref, ckv_ref, kpe_ref, w_ref, nn_ref, o_ref, *, scale):
    del prev_ref
    ks, vs = _mla_kv(ckv_ref[...], kpe_ref[...], w_ref[...], nn_ref[...], scale * LOG2_E)
    for h in range(H_B):
        s = lax.dot_general(q_ref[:, 2 * h * LANE:(2 * h + 2) * LANE], ks[h], _TRANS_B,
                            preferred_element_type=f32)
        o_ref[:, h * V_B:(h + 1) * V_B] = _softmax_pv(s, vs[h]).astype(o_ref.dtype)


def _mla_ctx(prev, q_mla, ckv, kpe, wukv, layer, nope_norm_k, scale):
    width = 2 * H_B * LANE
    return pl.pallas_call(
        functools.partial(_mla_ctx_kernel, scale=scale),
        grid=(BATCH,),
        in_specs=[pl.BlockSpec(memory_space=pl.ANY),
                  pl.BlockSpec((SEQ, width), lambda b: (b, 0)),
                  pl.BlockSpec((SEQ, KV_LORA), lambda b: (b, 0)),
                  pl.BlockSpec((SEQ, LANE), lambda b: (b, 0)),
                  pl.BlockSpec((None, KV_LORA, width), lambda b: (layer, 0, 0)),
                  pl.BlockSpec((1, LANE), lambda b: (0, 0))],
        out_specs=pl.BlockSpec((SEQ, H_B * V_B), lambda b: (b, 0)),
        out_shape=jax.ShapeDtypeStruct((M_ROWS, H_B * V_B), bf16),
        input_output_aliases={0: 0},
        compiler_params=_cparams(("parallel",)),
        name="mla_attn_ctx",
    )(prev, q_mla, ckv, kpe, wukv, nope_norm_k)


def _mla_lat_kernel(prev_ref, q_ref, ckvc_ref, kpec_ref, ckvn_ref, kpen_ref, w_ref, nn_ref, o_ref,
                    k_scr, v_scr, *, scale):
    del prev_ref

    @pl.when(pl.program_id(2) == 0)
    def _():
        ks, vs = _mla_kv(ckvc_ref[...], kpec_ref[...], w_ref[...], nn_ref[...], scale * LOG2_E)
        k_scr[0:PAST_LEN, :] = ks[0]
        v_scr[0:PAST_LEN, :] = vs[0]
        ks, vs = _mla_kv(ckvn_ref[...], kpen_ref[...], w_ref[...], nn_ref[...], scale * LOG2_E)
        k_scr[PAST_LEN:KV_SEQ, :] = ks[0]
        v_scr[PAST_LEN:KV_SEQ, :] = vs[0]

    k_b, v_b = k_scr[...], v_scr[...]
    half = q_ref.shape[0] // 2
    qs = [q_ref[0:half, :], q_ref[half:2 * half, :]]
    for i, o in enumerate(_attend(qs, [k_b, k_b], [v_b, v_b])):
        o_ref[i * half:(i + 1) * half, :] = o.astype(o_ref.dtype)


def _mla_lat(prev, q_mla, ckv, kpe, cache_ckv, cache_kpe, wukv, layer, nope_norm_k, scale, *, tq=1024):
    nq = DEC_SEQ // tq
    lat0 = MC // DEC_SEQ
    return pl.pallas_call(
        functools.partial(_mla_lat_kernel, scale=scale),
        grid=(DEC_BATCH, H_B, nq),
        in_specs=[pl.BlockSpec(memory_space=pl.ANY),
                  pl.BlockSpec((tq, 2 * LANE), lambda b, h, i: (MC // tq + b * nq + i, h)),
                  pl.BlockSpec((None, None, PAST_LEN, KV_LORA), lambda b, h, i: (b, layer, 0, 0)),
                  pl.BlockSpec((None, None, PAST_LEN, LANE), lambda b, h, i: (b, layer, 0, 0)),
                  pl.BlockSpec((DEC_SEQ, KV_LORA), lambda b, h, i: (lat0 + b, 0)),
                  pl.BlockSpec((DEC_SEQ, LANE), lambda b, h, i: (lat0 + b, 0)),
                  pl.BlockSpec((None, KV_LORA, 2 * LANE), lambda b, h, i: (layer, 0, h)),
                  pl.BlockSpec((1, LANE), lambda b, h, i: (0, 0))],
        out_specs=pl.BlockSpec((tq, V_B), lambda b, h, i: (MC // tq + b * nq + i, h)),
        out_shape=jax.ShapeDtypeStruct((M_ROWS, H_B * V_B), bf16),
        scratch_shapes=[pltpu.VMEM((KV_SEQ, 2 * LANE), bf16), pltpu.VMEM((KV_SEQ, V_B), bf16)],
        input_output_aliases={0: 0},
        compiler_params=_cparams(("parallel", "parallel", "arbitrary")),
        name="mla_attn_lat",
    )(prev, q_mla, cache_ckv, cache_kpe, ckv, kpe, wukv, nope_norm_k)


def _gqa_ctx_kernel(prev_ref, cq_ref, ck_ref, cv_ref, qkn_ref, o_ref, kg_ref, *, scale):
    del prev_ref
    rep = H_C // KVH_C
    for g in range(KVH_C):
        gs = slice(g * HD_C, (g + 1) * HD_C)
        k = _rms(ck_ref[:, gs], qkn_ref[1:2, :])
        kg_ref[:, gs] = k
        k_b = (k * (scale * LOG2_E)).astype(bf16)
        v_b = cv_ref[:, gs].astype(bf16)
        for h in range(g * rep, (g + 1) * rep):
            hs = slice(h * HD_C, (h + 1) * HD_C)
            q = _rms(cq_ref[:, hs], qkn_ref[0:1, :]).astype(bf16)
            s = lax.dot_general(q, k_b, _TRANS_B, preferred_element_type=f32)
            o_ref[:, hs] = _softmax_pv(s, v_b).astype(o_ref.dtype)


def _gqa_ctx(prev, z, qk_norm, scale):
    def zspec(col, w):
        return pl.BlockSpec((SEQ, w), lambda b: (b, col // w))

    return pl.pallas_call(
        functools.partial(_gqa_ctx_kernel, scale=scale),
        grid=(BATCH,),
        in_specs=[pl.BlockSpec(memory_space=pl.ANY),
                  zspec(COL_CQ, H_C * HD_C), zspec(COL_CK, KVH_C * HD_C), zspec(COL_CV, KVH_C * HD_C),
                  pl.BlockSpec((2, HD_C), lambda b: (0, 0))],
        out_specs=[pl.BlockSpec((SEQ, H_C * HD_C), lambda b: (b, 0)),
                   pl.BlockSpec((SEQ, KVH_C * HD_C), lambda b: (b, 0))],
        out_shape=[jax.ShapeDtypeStruct((M_ROWS, H_C * HD_C), bf16),
                   jax.ShapeDtypeStruct((MC, KVH_C * HD_C), f32)],
        input_output_aliases={0: 0},
        compiler_params=_cparams(("parallel",)),
        name="gqa_attn_ctx",
    )(prev, z, z, z, qk_norm)


def _gqa_lat_kernel(prev_ref, cq_ref, ckn_ref, cvn_ref, kc_ref, vc_ref, qkn_ref, ck_ref, sk_ref, cqt_ref, sqt_ref,
                    o_ref, k_scr, v_scr, *, scale):
    del prev_ref
    rep = H_C // KVH_C

    @pl.when(pl.program_id(2) == 0)
    def _():
        kscale = scale * LOG2_E
        k_scr[0:PAST_LEN, :] = (kc_ref[...] * kscale).astype(bf16)
        v_scr[0:PAST_LEN, :] = vc_ref[...].astype(bf16)
        k = _rope(_rms(ckn_ref[...], qkn_ref[1:2, :]), ck_ref[...], sk_ref[...], HD_C // 4)
        k_scr[PAST_LEN:KV_SEQ, :] = (k * kscale).astype(bf16)
        v_scr[PAST_LEN:KV_SEQ, :] = cvn_ref[...].astype(bf16)

    cq, sq = cqt_ref[...], sqt_ref[...]
    k_b, v_b = k_scr[...], v_scr[...]
    half = cq_ref.shape[0] // 2
    for r in range(rep):
        rs = slice(r * HD_C, (r + 1) * HD_C)
        q = _rope(_rms(cq_ref[:, rs], qkn_ref[0:1, :]), cq, sq, HD_C // 4).astype(bf16)
        for i, o in enumerate(_attend([q[0:half], q[half:2 * half]], [k_b, k_b], [v_b, v_b])):
            o_ref[i * half:(i + 1) * half, rs] = o.astype(o_ref.dtype)


def _gqa_lat(prev, z, cache_k, cache_v, layer, qk_norm, tabs_lat, scale, *, tq=512):
    nq = DEC_SEQ // tq
    lat0 = MC // DEC_SEQ
    rep = H_C // KVH_C
    gw = rep * HD_C
    cos, sin = tabs_lat
    return pl.pallas_call(
        functools.partial(_gqa_lat_kernel, scale=scale),
        grid=(DEC_BATCH, KVH_C, nq),
        in_specs=[pl.BlockSpec(memory_space=pl.ANY),
                  pl.BlockSpec((tq, gw), lambda b, g, i: (MC // tq + b * nq + i, COL_CQ // gw + g)),
                  pl.BlockSpec((DEC_SEQ, HD_C), lambda b, g, i: (lat0 + b, COL_CK // HD_C + g)),
                  pl.BlockSpec((DEC_SEQ, HD_C), lambda b, g, i: (lat0 + b, COL_CV // HD_C + g)),
                  pl.BlockSpec((None, None, PAST_LEN, HD_C), lambda b, g, i: (b, layer, 0, g)),
                  pl.BlockSpec((None, None, PAST_LEN, HD_C), lambda b, g, i: (b, layer, 0, g)),
                  pl.BlockSpec((2, HD_C), lambda b, g, i: (0, 0)),
                  pl.BlockSpec((DEC_SEQ, LANE), lambda b, g, i: (0, 0)),
                  pl.BlockSpec((DEC_SEQ, LANE), lambda b, g, i: (0, 0)),
                  pl.BlockSpec((tq, LANE), lambda b, g, i: (i, 0)),
                  pl.BlockSpec((tq, LANE), lambda b, g, i: (i, 0))],
        out_specs=pl.BlockSpec((tq, gw), lambda b, g, i: (MC // tq + b * nq + i, g)),
        out_shape=jax.ShapeDtypeStruct((M_ROWS, H_C * HD_C), bf16),
        scratch_shapes=[pltpu.VMEM((KV_SEQ, HD_C), bf16), pltpu.VMEM((KV_SEQ, HD_C), bf16)],
        input_output_aliases={0: 0},
        compiler_params=_cparams(("parallel", "parallel", "arbitrary")),
        name="gqa_attn_lat",
    )(prev, z, z, z, cache_k, cache_v, qk_norm, cos, sin, cos, sin)


def _hgrn_levels(c, base):
    w, out = base, []
    while w < c:
        out.append(w)
        w *= 2
    return out


def _hgrn_consts(c, base):
    p = np.arange(c)[:, None]
    r = np.arange(c)[None, :]
    sums = [(r <= p).astype(np.float32)]
    if base == 1:
        masks = [r == p]
    else:
        in_block = ((r // base) == (p // base)) & (r <= p)
        sums += [in_block.astype(np.float32), -in_block.astype(np.float32)]
        masks = [in_block]
    for w in _hgrn_levels(c, base):
        start = (p // (2 * w)) * (2 * w)
        ref = start + w - 1
        later = (p % (2 * w)) >= w
        sums.append(np.where(later, (r > ref) & (r <= p), (r > p) & (r <= ref)).astype(np.float32))
        masks.append(((r // (2 * w)) * (2 * w) == start) & later & ((r % (2 * w)) < w))
    sums.append((r > p).astype(np.float32))
    g = np.stack(sums)
    m = np.stack(masks).astype(np.float32)
    flip = lambda a: a[:, ::-1, ::-1]
    g = np.stack([g, flip(g)]).reshape(2, -1, c)
    m = np.stack([m, flip(m)]).reshape(2, -1, c)
    return jnp.asarray(np.concatenate([g] * HGRN_SPLIT, axis=-1), bf16), jnp.asarray(m, f32)


def _hgrn_block_sum(t_len):
    return jnp.asarray(np.arange(t_len)[None, :] // HGRN_BLOCK == np.arange(t_len // HGRN_BLOCK)[:, None], bf16)


def _hgrn_kernel(*refs, layer, t_len, latent):
    refs = refs[1:]
    if latent:
        (lb_ref, on_ref, gm_ref, mk_ref, gmf_ref, mkf_ref, bs_ref, q_ref, v_ref, ff_ref, fb_ref, g_ref, s_ref,
         out_ref, of_ref, ob_ref, qd_ref, u_ref, dec_ref, st_ref) = refs
        sout_ref = None
    else:
        (lb_ref, on_ref, gm_ref, mk_ref, gmf_ref, mkf_ref, bs_ref, q_ref, v_ref, ff_ref, fb_ref, g_ref,
         out_ref, sout_ref, of_ref, ob_ref, qd_ref, u_ref, dec_ref, st_ref) = refs
        s_ref = None
    c = HGRN_C
    hp = HGRN_HP
    n_chunks = t_len // c
    trans_b = (((1,), (1,)), ((), ()))
    heads = [slice(i * LANE, (i + 1) * LANE) for i in range(hp)]

    row = lax.broadcasted_iota(jnp.int32, (c, LANE), 0)

    if layer > 0:
        ps = [lb_ref[i] for i in range(DEPTH)]
        pmax = functools.reduce(jnp.maximum, ps)
        es = [jnp.exp(p - pmax) for p in ps]
        lb = functools.reduce(lambda a, b: a + b, es[1:layer + 1]) / functools.reduce(lambda a, b: a + b, es)
        log_lb = jnp.log(lb)
        log_1m = jnp.log1p(-lb)

    def log_forget(pre, d):
        ls = jnp.minimum(pre, 0.0) - jnp.log(1.0 + jnp.exp(-jnp.abs(pre)))
        if layer == 0:
            return ls
        a = log_lb[d:d + 1, :]
        cc = log_1m[d:d + 1, :] + ls
        return jnp.maximum(a, cc) + jnp.log(1.0 + jnp.exp(-jnp.abs(a - cc)))

    dirs = (0, 1)
    pre_refs = (ff_ref, fb_ref)
    oacc_refs = (of_ref, ob_ref)

    def stack(a):
        return jnp.concatenate([a[:, hs] for hs in heads], axis=0)

    def head_blocks(pp):
        return [pp[i * c:(i + 1) * c, i * c:(i + 1) * c] for i in range(hp)]

    def split(a):
        pieces, rem = [], a
        for _ in range(HGRN_SPLIT):
            pieces.append(rem.astype(bf16))
            rem = rem - pieces[-1].astype(f32)
        return jnp.concatenate(pieces, axis=0)

    def intra_body(ci, carry, *, fast):
        base = HGRN_BLOCK if fast else 1
        levels = _hgrn_levels(c, base)
        lv0 = 3 if fast else 1
        last = lv0 + len(levels)
        gsel, msk = (gmf_ref, mkf_ref) if fast else (gm_ref, mk_ref)
        sl = pl.ds(pl.multiple_of(ci * c, c), c)
        qx = q_ref[sl, :]
        q = qx * _sigmoid(qx)
        q_b = q.astype(bf16)
        v_b = v_ref[sl, :].astype(bf16)
        logf = [log_forget(pre_refs[d][sl, :], d) for d in dirs]
        k = [1.0 - jnp.exp(lf) for lf in logf]
        x = [jnp.exp(jnp.dot(gsel[d], split(logf[d]), preferred_element_type=f32)) for d in dirs]
        xb = lambda d, i: x[d][i * c:(i + 1) * c]
        for d in dirs:
            qd_ref[d, sl, :] = (q * xb(d, 0)).astype(bf16)
        kd = [(k[d] * xb(d, last)).astype(bf16) for d in dirs]

        if fast:
            pps = [lax.dot_general(stack((q * xb(d, 1)).astype(bf16)), stack((k[d] * xb(d, 2)).astype(bf16)),
                                   trans_b, preferred_element_type=f32) for d in dirs]
        else:
            qs = stack(q_b)
            pps = [lax.dot_general(qs, stack(k[d].astype(bf16)), trans_b, preferred_element_type=f32)
                   for d in dirs]
        attn = [[jnp.where(msk[d, 0:c, :] > 0.5, blk, 0.0) for blk in head_blocks(pps[d])] for d in dirs]
        for li, w in enumerate(levels):
            rs = []
            for d in dirs:
                later = ((row % (2 * w)) < w) if d == 1 else ((row % (2 * w)) >= w)
                later = jnp.concatenate([later] * hp, axis=1)
                rs.append(stack((jnp.where(later, q, k[d]) * xb(d, lv0 + li)).astype(bf16)))
            pps = [lax.dot_general(r, r, trans_b, preferred_element_type=f32) for r in rs]
            for d in dirs:
                mask = msk[d, (1 + li) * c:(2 + li) * c, :]
                attn[d] = [a + mask * blk for a, blk in zip(attn[d], head_blocks(pps[d]))]
        for hh, hs in enumerate(heads):
            outs = [jnp.dot(attn[d][hh].astype(bf16), v_b[:, hs], preferred_element_type=f32) for d in dirs]
            us = [lax.dot_general(v_b[:, hs], kd[d][:, hs], (((0,), (0,)), ((), ())),
                                  preferred_element_type=f32) for d in dirs]
            for d in dirs:
                oacc_refs[d][sl, hs] = outs[d]
                u_ref[d, ci, hh] = us[d]
        for d in dirs:
            decay = x[d][0:1] if d == 1 else x[d][c - 1:c]
            dec_ref[d, ci] = jnp.broadcast_to(decay, (8, hp * LANE))
        return carry

    worst = [jnp.dot(bs_ref[...], jnp.maximum(-pre_refs[d][...], 0.0).astype(bf16), preferred_element_type=f32)
             for d in dirs]
    fast_ok = jnp.max(jnp.maximum(worst[0], worst[1])) < HGRN_FAST_LIMIT

    @pl.when(fast_ok)
    def _():
        lax.fori_loop(0, n_chunks, functools.partial(intra_body, fast=True), 0)

    @pl.when(jnp.logical_not(fast_ok))
    def _():
        lax.fori_loop(0, n_chunks, functools.partial(intra_body, fast=False), 0)

    for d in range(2):
        for hh in range(hp):
            if latent:
                st_ref[d, hh] = s_ref[0, 0, d, hh].T
            else:
                st_ref[d, hh] = jnp.zeros((DV_A, DK_A), f32)

    def scan(d, cidx, oacc_ref):
        sl = pl.ds(pl.multiple_of(cidx * c, c), c)
        dec = dec_ref[d, cidx]
        for hh, hs in enumerate(heads):
            st = st_ref[d, hh]
            oacc_ref[sl, hs] = oacc_ref[sl, hs] + lax.dot_general(
                qd_ref[d, sl, hs], st.astype(bf16), trans_b, preferred_element_type=f32)
            st_ref[d, hh] = st * dec[0:1, hs] + u_ref[d, cidx, hh]

    def scan_body(ci, carry):
        scan(0, ci, of_ref)
        scan(1, n_chunks - 1 - ci, ob_ref)
        return carry

    lax.fori_loop(0, n_chunks, scan_body, 0)

    if not latent:
        for d in range(2):
            for hh in range(hp):
                sout_ref[0, d, hh] = st_ref[d, hh].T

    gx = g_ref[...]
    gate = gx * _sigmoid(gx)
    for hs in heads:
        o = of_ref[:, hs] + ob_ref[:, hs]
        out_ref[:, hs] = (_rms(o, on_ref[...]) * gate[:, hs]).astype(out_ref.dtype)


def _hgrn(prev, z, hgrn_lb, onorm, state, consts, *, layer, latent):
    t_len = DEC_SEQ if latent else SEQ
    n_samples = DEC_BATCH if latent else BATCH
    row0 = (MC // t_len) if latent else 0
    (gmat, masks), (gmat_f, masks_f) = consts
    bsum = _hgrn_block_sum(t_len)
    hp = HGRN_HP
    wide = hp * LANE

    def zspec(col):
        base = col // wide
        return pl.BlockSpec((t_len, wide), lambda b, h: (row0 + b, base + h))

    in_specs = [pl.BlockSpec(memory_space=pl.ANY),
                pl.BlockSpec((DEPTH, 2, wide), lambda b, h: (0, 0, h)),
                pl.BlockSpec((1, LANE), lambda b, h: (0, 0)),
                pl.BlockSpec(gmat.shape, lambda b, h: (0, 0, 0)),
                pl.BlockSpec(masks.shape, lambda b, h: (0, 0, 0)),
                pl.BlockSpec(gmat_f.shape, lambda b, h: (0, 0, 0)),
                pl.BlockSpec(masks_f.shape, lambda b, h: (0, 0, 0)),
                pl.BlockSpec(bsum.shape, lambda b, h: (0, 0)),
                zspec(COL_AQ), zspec(COL_AV), zspec(COL_AFF), zspec(COL_AFB), zspec(COL_AG)]
    args = [prev, hgrn_lb, onorm, gmat, masks, gmat_f, masks_f, bsum, z, z, z, z, z]
    o_spec = pl.BlockSpec((t_len, wide), lambda b, h: (row0 + b, h))
    o_shape = jax.ShapeDtypeStruct((M_ROWS, H_A * DV_A), bf16)
    if latent:
        in_specs.append(pl.BlockSpec((1, 1, 2, hp, DK_A, DV_A), lambda b, h: (b, layer, 0, h, 0, 0)))
        args.append(state)
        out_specs, out_shape = o_spec, o_shape
    else:
        out_specs = [o_spec, pl.BlockSpec((1, 2, hp, DK_A, DV_A), lambda b, h: (b, 0, h, 0, 0))]
        out_shape = [o_shape, jax.ShapeDtypeStruct((BATCH, 2, H_A, DK_A, DV_A), f32)]
    n_chunks = t_len // HGRN_C
    return pl.pallas_call(
        functools.partial(_hgrn_kernel, layer=layer, t_len=t_len, latent=latent),
        grid=(n_samples, H_A // hp),
        in_specs=in_specs,
        out_specs=out_specs,
        out_shape=out_shape,
        scratch_shapes=[pltpu.VMEM((t_len, wide), f32), pltpu.VMEM((t_len, wide), f32),
                        pltpu.VMEM((2, t_len, wide), bf16),
                        pltpu.VMEM((2, n_chunks, hp, DV_A, DK_A), f32),
                        pltpu.VMEM((2, n_chunks, 8, wide), f32),
                        pltpu.VMEM((2, hp, DV_A, DK_A), f32)],
        input_output_aliases={0: 0},
        compiler_params=_cparams(("parallel", "parallel")),
        name="hgrn_lat" if latent else "hgrn_ctx",
    )(*args)


def _permute_wuq(wuq):
    w = wuq.reshape(DEPTH, Q_LORA, H_B, NOPE_B + ROPE_B)
    nope = w[..., :NOPE_B].reshape(DEPTH, Q_LORA, H_B * NOPE_B)
    pe = jnp.pad(w[..., NOPE_B:], ((0, 0), (0, 0), (0, 0), (0, LANE - ROPE_B))).reshape(DEPTH, Q_LORA, H_B * LANE)
    return jnp.concatenate([nope, pe], axis=-1).astype(bf16)


def _pad_lane(v):
    return jnp.pad(v, ((0, 0),) * (v.ndim - 1) + ((0, LANE - v.shape[-1]),))


def kernel(x_prompt, x_sample, state_hgrn, cache_mla_ckv, cache_mla_kpe, cache_gqa_k, cache_gqa_v, c, c_ctx,
           w_mod, b_mod, norm_w, ffn_w13, ffn_w2, w_in, hgrn_lb, hgrn_onorm, mla_qa_norm, mla_kva_norm,
           mla_wuq, mla_wukv, mla_nope_norm, mla_rope_norm, gqa_qk_norm, w_branch, w_out):
    w_in_t = jnp.swapaxes(w_in, 1, 2)
    wuq_b = _permute_wuq(mla_wuq)
    wukv_b = mla_wukv.astype(bf16)
    hgrn_consts = (_hgrn_consts(HGRN_C, 1), _hgrn_consts(HGRN_C, HGRN_BLOCK))
    rope_norm_p = _pad_lane(mla_rope_norm)
    cache_kpe_p = _pad_lane(cache_mla_kpe)
    cache_k = cache_gqa_k.reshape(DEC_BATCH, DEPTH, PAST_LEN, KVH_C * HD_C)
    cache_v = cache_gqa_v.reshape(DEC_BATCH, DEPTH, PAST_LEN, KVH_C * HD_C)
    tabs_m = tuple(jnp.asarray(t) for t in _rope_tables(ROPE_B))
    tabs_g_lat = tuple(jnp.asarray(t[PAST_LEN:]) for t in _rope_tables(HD_C))
    mla_scale = (NOPE_B + ROPE_B) ** -0.5
    gqa_scale = HD_C ** -0.5

    cond = jnp.zeros((N_MOD_ROWS, D_MODEL), f32).at[0].set(c_ctx).at[1:1 + DEC_BATCH].set(c)
    x = jnp.concatenate([x_prompt.reshape(MC, D_MODEL), x_sample.reshape(ML, D_MODEL)], axis=0)

    out_a, out_b, out_c = (jnp.zeros((M_ROWS, BR_W), bf16) for _ in range(N_BRANCH))
    st_h, st_ckv, st_kpe, st_k, st_v = [], [], [], [], []
    for l in range(DEPTH):
        mod = _mm(cond, [(w_mod, (l,), 0)], n=9 * D_MODEL, tm=N_MOD_ROWS, tn=1024, out_dtype=f32, name="adaln",
                  pro="silu", epi="bias", bias=b_mod[l][None, :]).reshape(N_MOD_ROWS, 9, D_MODEL)

        def ffn(xin, i, ids):
            g = _mm(xin, [(ffn_w13, (l, i), 0), (ffn_w13, (l, i), D_FF)], n=D_FF, tm=1024, tn=512,
                    out_dtype=bf16, name="ffn_up", pro="normmod", epi="swiglu", ahead=True,
                    norm_w=norm_w[l, 2 * i][None, :], mod=mod, shift_id=ids[0], scale_id=ids[1])
            return _mm(g, [(ffn_w2, (l, i), 0)], n=D_MODEL, tm=1024, tn=256, out_dtype=f32, name="ffn_down",
                       epi="residual", res=xin, mod=mod, gate_id=ids[2], gate_scale=0.5)

        x = ffn(x, 0, (0, 1, 2))

        z = _mm(x, [(w_in_t, (l,), 0)], n=N_IN_PAD, tm=1024, tn=IN_TILE, out_dtype=f32, name="in_proj",
                pro="normmod", norm_w=norm_w[l, 1][None, :], mod=mod, shift_id=3, scale_id=4,
                w_rows=lambda j: (j * (IN_TILE // ROPE_B)
                                  - jnp.where(j >= IN_SHIFT_FROM, IN_SHIFT // ROPE_B, 0)) * ROPE_B)

        out_a, s_new = _hgrn(out_a, z, hgrn_lb, hgrn_onorm[l][None, :], None, hgrn_consts, layer=l, latent=False)
        out_a = _hgrn(out_a, z, hgrn_lb, hgrn_onorm[l][None, :], state_hgrn, hgrn_consts, layer=l, latent=True)
        st_h.append(s_new)

        q_mla, ckv, kpe = _mla_prep(z, mla_qa_norm[l][None, :], mla_kva_norm[l][None, :], rope_norm_p[l],
                                    mla_nope_norm[l, 0][None, :], wuq_b, l, tabs_m)
        nn_k = mla_nope_norm[l, 1][None, :]
        out_b = _mla_ctx(out_b, q_mla, ckv, kpe, wukv_b, l, nn_k, mla_scale)
        out_b = _mla_lat(out_b, q_mla, ckv, kpe, cache_mla_ckv, cache_kpe_p, wukv_b, l, nn_k, mla_scale)
        st_ckv.append(ckv[:MC].reshape(BATCH, SEQ, KV_LORA))
        st_kpe.append(kpe[:MC, :ROPE_B].reshape(BATCH, SEQ, ROPE_B))

        out_c, k_g = _gqa_ctx(out_c, z, gqa_qk_norm[l], gqa_scale)
        out_c = _gqa_lat(out_c, z, cache_k, cache_v, l, gqa_qk_norm[l], tabs_g_lat, gqa_scale)
        st_k.append(k_g.reshape(BATCH, SEQ, KVH_C, HD_C))
        st_v.append(z[:MC, COL_CV:COL_CV + KVH_C * HD_C].reshape(BATCH, SEQ, KVH_C, HD_C))

        merged = _merge(out_a, out_b, out_c, w_branch, l, z)
        x = _mm(merged, [(w_out, (l,), 0)], n=D_MODEL, tm=1024, tn=1024, out_dtype=f32, name="out_proj",
                epi="residual", res=x, mod=mod, gate_id=5, gate_scale=1.0)

        x = ffn(x, 1, (6, 7, 8))

    y_p = x[:MC].reshape(BATCH, SEQ, D_MODEL)
    y_s = x[MC:].reshape(DEC_BATCH, DEC_SEQ, D_MODEL)
    return (y_p, y_s,
            jnp.stack(st_h, axis=1), jnp.stack(st_ckv, axis=1), jnp.stack(st_kpe, axis=1),
            jnp.stack(st_k, axis=1), jnp.stack(st_v, axis=1))
```

```python
import functools

import numpy as np
import jax
import jax.numpy as jnp
from jax import lax
from jax.experimental import pallas as pl
from jax.experimental.pallas import tpu as pltpu

f32 = jnp.float32
bf16 = jnp.bfloat16

D_MODEL = 2048
BATCH = 16
SEQ = 256
DEPTH = 4
DEC_BATCH = 4
DEC_SEQ = 1024
PAST_LEN = 256
GRID_W = 64
ROPE_THETA = 10000.0
EPS = 1e-6
D_FF = 5632
N_BRANCH = 3
BR_W = 1024
H_A = 8
DK_A = 128
DV_A = 128
H_B = 8
Q_LORA = 512
KV_LORA = 256
NOPE_B = 128
ROPE_B = 64
V_B = 128
H_C = 8
KVH_C = 4
HD_C = 128

LANE = 128
MC = BATCH * SEQ
ML = DEC_BATCH * DEC_SEQ
M_ROWS = MC + ML
KV_SEQ = PAST_LEN + DEC_SEQ
N_MOD_ROWS = 16

COL_AQ, COL_AV, COL_AFF, COL_AFB, COL_AG = 0, 1024, 2048, 3072, 4096
COL_BQ = 5120
COL_BKV = 5632
COL_BPE = 5888
COL_CQ = 6144
COL_CK = 7168
COL_CV = 7680
COL_GBR = 8192
N_IN_PAD = 14336
IN_TILE = 1024
IN_SHIFT_FROM = COL_CQ // IN_TILE
IN_SHIFT = COL_CQ - (COL_BPE + ROPE_B)

ROW_TILE = 256
HGRN_C = 128
HGRN_HP = 2
HGRN_SPLIT = 2
HGRN_BLOCK = 32
HGRN_FAST_LIMIT = 60.0
VMEM_LIMIT = 56 * 1024 * 1024


def _cparams(sem):
    return pltpu.CompilerParams(dimension_semantics=sem, vmem_limit_bytes=VMEM_LIMIT)


def _sigmoid(x):
    return 1.0 / (1.0 + jnp.exp(-x))


def _rms(x, w, n=None):
    n = x.shape[-1] if n is None else n
    ms = jnp.sum(x * x, axis=-1, keepdims=True) * (1.0 / n)
    return x * lax.rsqrt(ms + EPS) * w


def _mod_row(i, tm):
    n_ctx = MC // tm
    per = DEC_SEQ // tm
    return jnp.where(i < n_ctx, 0, 1 + jnp.maximum(i - n_ctx, 0) // per)


def _mm_kernel(*refs, pro, epi, n_w, shift_id, scale_id, gate_id, gate_scale, w_rows, ahead):
    it = iter(refs)
    x_ref = next(it)
    nw_ref = next(it) if pro in ("norm", "normmod") else None
    modk_ref = next(it) if pro == "normmod" else None
    w_refs = [next(it) for _ in range(n_w)]
    res_ref = next(it) if epi == "residual" else None
    modn_ref = next(it) if epi == "residual" else None
    bias_ref = next(it) if epi == "bias" else None
    o_ref = next(it)
    h_ref = next(it) if pro is not None else None

    def prologue(dst):
        x = x_ref[...].astype(f32)
        if pro == "silu":
            y = x * _sigmoid(x)
        else:
            y = _rms(x, nw_ref[...])
            if pro == "normmod":
                y = y * (1.0 + modk_ref[0, scale_id:scale_id + 1, :]) + modk_ref[0, shift_id:shift_id + 1, :]
        h_ref[dst] = y.astype(bf16)

    def product(lhs):
        if w_rows is None:
            accs = [jnp.dot(lhs, w[...].astype(bf16), preferred_element_type=f32) for w in w_refs]
        else:
            accs = [lax.dot_general(lhs, w[(0,) * (len(w.shape) - 2)].astype(bf16), (((1,), (1,)), ((), ())),
                                    preferred_element_type=f32) for w in w_refs]
        if epi == "swiglu":
            a, u = accs
            out = a * _sigmoid(a) * u
        elif epi == "residual":
            out = res_ref[...] + (gate_scale * modn_ref[0, gate_id:gate_id + 1, :]) * accs[0]
        elif epi == "bias":
            out = accs[0] + bias_ref[...]
        else:
            out = accs[0]
        o_ref[...] = out.astype(o_ref.dtype)

    i, j = pl.program_id(0), pl.program_id(1)
    if pro is None:
        product(x_ref[...])
    elif not ahead:
        pl.when(j == 0)(lambda: prologue(0))
        product(h_ref[0])
    else:
        slot = i % 2
        pl.when((i == 0) & (j == 0))(lambda: prologue(0))
        with_next = (j == pl.num_programs(1) - 1) & (i < pl.num_programs(0) - 1)

        @pl.when(with_next)
        def _():
            product(h_ref[slot])
            prologue(1 - slot)

        @pl.when(jnp.logical_not(with_next))
        def _():
            product(h_ref[slot])


def _w_spec(w, lead, col0, k, tn):
    base = col0 // tn
    return pl.BlockSpec((None,) * len(lead) + (k, tn), lambda i, j: tuple(lead) + (0, base + j))


def _mm(x, ws, *, n, tm, tn, out_dtype, name, pro=None, epi="store", norm_w=None, mod=None,
        shift_id=0, scale_id=0, gate_id=0, gate_scale=1.0, res=None, bias=None, w_rows=None, ahead=False):
    m, k = x.shape
    grid = (m // tm, n // tn)

    def x_tile(i, j):
        if not ahead:
            return i
        return jnp.minimum(i + jnp.where(j == grid[1] - 1, 1, 0), grid[0] - 1)

    in_specs = [pl.BlockSpec((tm, k), lambda i, j: (x_tile(i, j), 0))]
    args = [x]
    if pro in ("norm", "normmod"):
        in_specs.append(pl.BlockSpec((1, k), lambda i, j: (0, 0)))
        args.append(norm_w)
    if pro == "normmod":
        in_specs.append(pl.BlockSpec((1, 9, k), lambda i, j: (_mod_row(x_tile(i, j), tm), 0, 0)))
        args.append(mod)
    for w, lead, col0 in ws:
        if w_rows is None:
            in_specs.append(_w_spec(w, lead, col0, k, tn))
        else:
            in_specs.append(pl.BlockSpec((pl.Element(1),) * len(lead) + (pl.Element(tn), pl.Element(k)),
                                         lambda i, j, lead=lead: tuple(lead) + (w_rows(j), 0)))
        args.append(w)
    if epi == "residual":
        in_specs.append(pl.BlockSpec((tm, tn), lambda i, j: (i, j)))
        args.append(res)
        in_specs.append(pl.BlockSpec((1, 9, tn), lambda i, j: (_mod_row(i, tm), 0, j)))
        args.append(mod)
    if epi == "bias":
        in_specs.append(pl.BlockSpec((1, tn), lambda i, j: (0, j)))
        args.append(bias)
    scratch = [pltpu.VMEM((2 if ahead else 1, tm, k), bf16)] if pro is not None else []
    kern = functools.partial(_mm_kernel, pro=pro, epi=epi, n_w=len(ws), shift_id=shift_id,
                             scale_id=scale_id, gate_id=gate_id, gate_scale=gate_scale, w_rows=w_rows, ahead=ahead)
    return pl.pallas_call(
        kern,
        grid=grid,
        in_specs=in_specs,
        out_specs=pl.BlockSpec((tm, tn), lambda i, j: (i, j)),
        out_shape=jax.ShapeDtypeStruct((m, n), out_dtype),
        scratch_shapes=scratch,
        compiler_params=_cparams(("arbitrary" if ahead else "parallel", "arbitrary")),
        name=name,
    )(*args)


def _out_proj_kernel(m_ref, w_ref, res_ref, mod_ref, o_ref, wb_ref, *, gate_id):
    @pl.when(pl.program_id(0) == 0)
    def _():
        wb_ref[...] = w_ref[...].astype(bf16)

    acc = jnp.dot(m_ref[...], wb_ref[...], preferred_element_type=f32)
    o_ref[...] = res_ref[...] + mod_ref[0, gate_id:gate_id + 1, :] * acc


def _out_proj(merged, w_out, layer, res, mod, *, gate_id, tm=512):
    row = pl.BlockSpec((tm, D_MODEL), lambda i: (i, 0))
    return pl.pallas_call(
        functools.partial(_out_proj_kernel, gate_id=gate_id),
        grid=(M_ROWS // tm,),
        in_specs=[row,
                  pl.BlockSpec((None, D_MODEL, D_MODEL), lambda i: (layer, 0, 0), pipeline_mode=pl.Buffered(1)),
                  row,
                  pl.BlockSpec((1, 9, D_MODEL), lambda i: (_mod_row(i, tm), 0, 0))],
        out_specs=row,
        out_shape=jax.ShapeDtypeStruct((M_ROWS, D_MODEL), f32),
        scratch_shapes=[pltpu.VMEM((D_MODEL, D_MODEL), bf16)],
        compiler_params=_cparams(("arbitrary",)),
        name="out_proj",
    )(merged, w_out, res, mod)


def _merge_kernel(oa_ref, ob_ref, oc_ref, wb_ref, ga_ref, gb_ref, gc_ref, o_ref):
    acc = None
    for n, (o, g) in enumerate(((oa_ref, ga_ref), (ob_ref, gb_ref), (oc_ref, gc_ref))):
        br = jnp.dot(o[...], wb_ref[n].astype(bf16), preferred_element_type=f32)
        term = _sigmoid(g[...]) * br
        acc = term if acc is None else acc + term
    o_ref[...] = acc.astype(o_ref.dtype)


def _merge(out_a, out_b, out_c, wb, layer, z, *, tm=2048, tn=256):
    grid = (M_ROWS // tm, D_MODEL // tn)
    o_spec = pl.BlockSpec((tm, BR_W), lambda i, j: (i, 0))

    def g_spec(n):
        base = (COL_GBR + n * D_MODEL) // tn
        return pl.BlockSpec((tm, tn), lambda i, j: (i, base + j))

    return pl.pallas_call(
        _merge_kernel,
        grid=grid,
        in_specs=[o_spec, o_spec, o_spec,
                  pl.BlockSpec((None, N_BRANCH, BR_W, tn), lambda i, j: (layer, 0, 0, j)),
                  g_spec(0), g_spec(1), g_spec(2)],
        out_specs=pl.BlockSpec((tm, tn), lambda i, j: (i, j)),
        out_shape=jax.ShapeDtypeStruct((M_ROWS, D_MODEL), bf16),
        compiler_params=_cparams(("parallel", "arbitrary")),
        name="merge",
    )(out_a, out_b, out_c, wb, z, z, z)


def _rope(x, cos, sin, quarter):
    lane = lax.broadcasted_iota(jnp.int32, x.shape, 1)
    first = (lane % (2 * quarter)) < quarter
    partner = jnp.where(first, -pltpu.roll(x, LANE - quarter, axis=1), pltpu.roll(x, quarter, axis=1))
    return x * cos + partner * sin


def _rope_tables(width):
    quarter = width // 4
    t = np.arange(DEC_SEQ)
    inv = ROPE_THETA ** (-np.arange(quarter, dtype=np.float32) / quarter)
    ang_r = (t // GRID_W).astype(np.float32)[:, None] * inv[None, :]
    ang_c = (t % GRID_W).astype(np.float32)[:, None] * inv[None, :]
    ang = np.concatenate([ang_r, ang_r, ang_c, ang_c], axis=1).astype(np.float32)
    cos = np.ones((KV_SEQ, LANE), np.float32)
    sin = np.zeros((KV_SEQ, LANE), np.float32)
    cos[PAST_LEN:, :width] = np.cos(ang)
    sin[PAST_LEN:, :width] = np.sin(ang)
    return cos, sin


def _q_table_block(i):
    n_ctx = MC // ROW_TILE
    per = DEC_SEQ // ROW_TILE
    return jnp.where(i < n_ctx, 0, 1 + jnp.maximum(i - n_ctx, 0) % per)


LOG2_E = 1.4426950408889634


def _softmax_pv(s2, v_b):
    m = jnp.max(s2, axis=-1, keepdims=True)
    p = jnp.exp2(s2 - m)
    l = jnp.sum(p, axis=-1, keepdims=True)
    return jnp.dot(p.astype(bf16), v_b, preferred_element_type=f32) / l


_TRANS_B = (((1,), (1,)), ((), ()))


def _attend(qs, ks, vs):
    outs = []
    for i in range(0, len(qs), 2):
        ss = [lax.dot_general(q, k, _TRANS_B, preferred_element_type=f32)
              for q, k in zip(qs[i:i + 2], ks[i:i + 2])]
        outs += [_softmax_pv(s, v) for s, v in zip(ss, vs[i:i + 2])]
    return outs


def _mla_prep_kernel(bq_ref, bkv_ref, bpe_ref, qa_ref, kva_ref, rn_ref, nn_ref, wuq_ref, cm_ref, sm_ref,
                     q_ref, ckv_ref, kpe_ref):
    cm, sm = cm_ref[...], sm_ref[...]
    ckv_ref[...] = _rms(bkv_ref[...], kva_ref[...])
    bpe = bpe_ref[...]
    bpe = jnp.where(lax.broadcasted_iota(jnp.int32, bpe.shape, 1) < ROPE_B, bpe, 0.0)
    kpe_ref[...] = _rope(_rms(bpe, rn_ref[1:2, :], n=ROPE_B), cm, sm, ROPE_B // 4)
    cq = jnp.dot(_rms(bq_ref[...], qa_ref[...]).astype(bf16), wuq_ref[...], preferred_element_type=f32)
    for h in range(H_B):
        nope = _rms(cq[:, h * LANE:(h + 1) * LANE], nn_ref[...])
        pe = _rms(cq[:, (H_B + h) * LANE:(H_B + h + 1) * LANE], rn_ref[0:1, :], n=ROPE_B)
        q_ref[:, 2 * h * LANE:(2 * h + 1) * LANE] = nope.astype(bf16)
        q_ref[:, (2 * h + 1) * LANE:(2 * h + 2) * LANE] = _rope(pe, cm, sm, ROPE_B // 4).astype(bf16)


def _mla_prep(z, qa_norm, kva_norm, rope_norm, nope_norm_q, wuq, layer, tabs_m):
    tr = ROW_TILE
    width = 2 * H_B * LANE

    def zspec(col, w):
        return pl.BlockSpec((tr, w), lambda i: (i, col // w))

    def wspec(shape):
        return pl.BlockSpec(shape, lambda i: (0, 0))

    tspec = pl.BlockSpec((tr, LANE), lambda i: (_q_table_block(i), 0))
    return pl.pallas_call(
        _mla_prep_kernel,
        grid=(M_ROWS // tr,),
        in_specs=[zspec(COL_BQ, Q_LORA), zspec(COL_BKV, KV_LORA), zspec(COL_BPE, LANE),
                  wspec((1, Q_LORA)), wspec((1, KV_LORA)), wspec((2, LANE)), wspec((1, LANE)),
                  pl.BlockSpec((None, Q_LORA, width), lambda i: (layer, 0, 0)), tspec, tspec],
        out_specs=[pl.BlockSpec((tr, width), lambda i: (i, 0)),
                   pl.BlockSpec((tr, KV_LORA), lambda i: (i, 0)),
                   pl.BlockSpec((tr, LANE), lambda i: (i, 0))],
        out_shape=[jax.ShapeDtypeStruct((M_ROWS, width), bf16),
                   jax.ShapeDtypeStruct((M_ROWS, KV_LORA), f32),
                   jax.ShapeDtypeStruct((M_ROWS, LANE), f32)],
        compiler_params=_cparams(("parallel",)),
        name="mla_prep",
    )(z, z, z, qa_norm, kva_norm, rope_norm, nope_norm_q, wuq, *tabs_m)


def _mla_kv(ckv, kpe, w, nn, kscale):
    kv = jnp.dot(ckv.astype(bf16), w, preferred_element_type=f32)
    n_h = w.shape[1] // (2 * LANE)
    kpe_b = (kpe * kscale).astype(bf16)
    ks, vs = [], []
    for h in range(n_h):
        nope = _rms(kv[:, 2 * h * LANE:(2 * h + 1) * LANE], nn) * kscale
        ks.append(jnp.concatenate([nope.astype(bf16), kpe_b], axis=1))
        vs.append(kv[:, (2 * h + 1) * LANE:(2 * h + 2) * LANE].astype(bf16))
    return ks, vs


def _mla_ctx_kernel(prev_ref, q_ref, ckv_ref, kpe_ref, w_ref, nn_ref, o_ref, *, scale):
    del prev_ref
    ks, vs = _mla_kv(ckv_ref[...], kpe_ref[...], w_ref[...], nn_ref[...], scale * LOG2_E)
    for h in range(H_B):
        s = lax.dot_general(q_ref[:, 2 * h * LANE:(2 * h + 2) * LANE], ks[h], _TRANS_B,
                            preferred_element_type=f32)
        o_ref[:, h * V_B:(h + 1) * V_B] = _softmax_pv(s, vs[h]).astype(o_ref.dtype)


def _mla_ctx(prev, q_mla, ckv, kpe, wukv, layer, nope_norm_k, scale):
    width = 2 * H_B * LANE
    return pl.pallas_call(
        functools.partial(_mla_ctx_kernel, scale=scale),
        grid=(BATCH,),
        in_specs=[pl.BlockSpec(memory_space=pl.ANY),
                  pl.BlockSpec((SEQ, width), lambda b: (b, 0)),
                  pl.BlockSpec((SEQ, KV_LORA), lambda b: (b, 0)),
                  pl.BlockSpec((SEQ, LANE), lambda b: (b, 0)),
                  pl.BlockSpec((None, KV_LORA, width), lambda b: (layer, 0, 0)),
                  pl.BlockSpec((1, LANE), lambda b: (0, 0))],
        out_specs=pl.BlockSpec((SEQ, H_B * V_B), lambda b: (b, 0)),
        out_shape=jax.ShapeDtypeStruct((M_ROWS, H_B * V_B), bf16),
        input_output_aliases={0: 0},
        compiler_params=_cparams(("parallel",)),
        name="mla_attn_ctx",
    )(prev, q_mla, ckv, kpe, wukv, nope_norm_k)


def _mla_lat_kernel(prev_ref, q_ref, ckvc_ref, kpec_ref, ckvn_ref, kpen_ref, w_ref, nn_ref, o_ref,
                    k_scr, v_scr, *, scale):
    del prev_ref

    @pl.when(pl.program_id(2) == 0)
    def _():
        ks, vs = _mla_kv(ckvc_ref[...], kpec_ref[...], w_ref[...], nn_ref[...], scale * LOG2_E)
        k_scr[0:PAST_LEN, :] = ks[0]
        v_scr[0:PAST_LEN, :] = vs[0]
        ks, vs = _mla_kv(ckvn_ref[...], kpen_ref[...], w_ref[...], nn_ref[...], scale * LOG2_E)
        k_scr[PAST_LEN:KV_SEQ, :] = ks[0]
        v_scr[PAST_LEN:KV_SEQ, :] = vs[0]

    k_b, v_b = k_scr[...], v_scr[...]
    half = q_ref.shape[0] // 2
    qs = [q_ref[0:half, :], q_ref[half:2 * half, :]]
    for i, o in enumerate(_attend(qs, [k_b, k_b], [v_b, v_b])):
        o_ref[i * half:(i + 1) * half, :] = o.astype(o_ref.dtype)


def _mla_lat(prev, q_mla, ckv, kpe, cache_ckv, cache_kpe, wukv, layer, nope_norm_k, scale, *, tq=1024):
    nq = DEC_SEQ // tq
    lat0 = MC // DEC_SEQ
    return pl.pallas_call(
        functools.partial(_mla_lat_kernel, scale=scale),
        grid=(DEC_BATCH, H_B, nq),
        in_specs=[pl.BlockSpec(memory_space=pl.ANY),
                  pl.BlockSpec((tq, 2 * LANE), lambda b, h, i: (MC // tq + b * nq + i, h)),
                  pl.BlockSpec((None, None, PAST_LEN, KV_LORA), lambda b, h, i: (b, layer, 0, 0)),
                  pl.BlockSpec((None, None, PAST_LEN, LANE), lambda b, h, i: (b, layer, 0, 0)),
                  pl.BlockSpec((DEC_SEQ, KV_LORA), lambda b, h, i: (lat0 + b, 0)),
                  pl.BlockSpec((DEC_SEQ, LANE), lambda b, h, i: (lat0 + b, 0)),
                  pl.BlockSpec((None, KV_LORA, 2 * LANE), lambda b, h, i: (layer, 0, h)),
                  pl.BlockSpec((1, LANE), lambda b, h, i: (0, 0))],
        out_specs=pl.BlockSpec((tq, V_B), lambda b, h, i: (MC // tq + b * nq + i, h)),
        out_shape=jax.ShapeDtypeStruct((M_ROWS, H_B * V_B), bf16),
        scratch_shapes=[pltpu.VMEM((KV_SEQ, 2 * LANE), bf16), pltpu.VMEM((KV_SEQ, V_B), bf16)],
        input_output_aliases={0: 0},
        compiler_params=_cparams(("parallel", "parallel", "arbitrary")),
        name="mla_attn_lat",
    )(prev, q_mla, cache_ckv, cache_kpe, ckv, kpe, wukv, nope_norm_k)


def _gqa_ctx_kernel(prev_ref, cq_ref, ck_ref, cv_ref, qkn_ref, o_ref, kg_ref, *, scale):
    del prev_ref
    rep = H_C // KVH_C
    for g in range(KVH_C):
        gs = slice(g * HD_C, (g + 1) * HD_C)
        k = _rms(ck_ref[:, gs], qkn_ref[1:2, :])
        kg_ref[:, gs] = k
        k_b = (k * (scale * LOG2_E)).astype(bf16)
        v_b = cv_ref[:, gs].astype(bf16)
        for h in range(g * rep, (g + 1) * rep):
            hs = slice(h * HD_C, (h + 1) * HD_C)
            q = _rms(cq_ref[:, hs], qkn_ref[0:1, :]).astype(bf16)
            s = lax.dot_general(q, k_b, _TRANS_B, preferred_element_type=f32)
            o_ref[:, hs] = _softmax_pv(s, v_b).astype(o_ref.dtype)


def _gqa_ctx(prev, z, qk_norm, scale):
    def zspec(col, w):
        return pl.BlockSpec((SEQ, w), lambda b: (b, col // w))

    return pl.pallas_call(
        functools.partial(_gqa_ctx_kernel, scale=scale),
        grid=(BATCH,),
        in_specs=[pl.BlockSpec(memory_space=pl.ANY),
                  zspec(COL_CQ, H_C * HD_C), zspec(COL_CK, KVH_C * HD_C), zspec(COL_CV, KVH_C * HD_C),
                  pl.BlockSpec((2, HD_C), lambda b: (0, 0))],
        out_specs=[pl.BlockSpec((SEQ, H_C * HD_C), lambda b: (b, 0)),
                   pl.BlockSpec((SEQ, KVH_C * HD_C), lambda b: (b, 0))],
        out_shape=[jax.ShapeDtypeStruct((M_ROWS, H_C * HD_C), bf16),
                   jax.ShapeDtypeStruct((MC, KVH_C * HD_C), f32)],
        input_output_aliases={0: 0},
        compiler_params=_cparams(("parallel",)),
        name="gqa_attn_ctx",
    )(prev, z, z, z, qk_norm)


def _gqa_lat_kernel(prev_ref, cq_ref, ckn_ref, cvn_ref, kc_ref, vc_ref, qkn_ref, ck_ref, sk_ref, cqt_ref, sqt_ref,
                    o_ref, k_scr, v_scr, *, scale):
    del prev_ref
    rep = H_C // KVH_C

    @pl.when(pl.program_id(2) == 0)
    def _():
        kscale = scale * LOG2_E
        k_scr[0:PAST_LEN, :] = (kc_ref[...] * kscale).astype(bf16)
        v_scr[0:PAST_LEN, :] = vc_ref[...].astype(bf16)
        k = _rope(_rms(ckn_ref[...], qkn_ref[1:2, :]), ck_ref[...], sk_ref[...], HD_C // 4)
        k_scr[PAST_LEN:KV_SEQ, :] = (k * kscale).astype(bf16)
        v_scr[PAST_LEN:KV_SEQ, :] = cvn_ref[...].astype(bf16)

    cq, sq = cqt_ref[...], sqt_ref[...]
    k_b, v_b = k_scr[...], v_scr[...]
    half = cq_ref.shape[0] // 2
    for r in range(rep):
        rs = slice(r * HD_C, (r + 1) * HD_C)
        q = _rope(_rms(cq_ref[:, rs], qkn_ref[0:1, :]), cq, sq, HD_C // 4).astype(bf16)
        for i, o in enumerate(_attend([q[0:half], q[half:2 * half]], [k_b, k_b], [v_b, v_b])):
            o_ref[i * half:(i + 1) * half, rs] = o.astype(o_ref.dtype)


def _gqa_lat(prev, z, cache_k, cache_v, layer, qk_norm, tabs_lat, scale, *, tq=512):
    nq = DEC_SEQ // tq
    lat0 = MC // DEC_SEQ
    rep = H_C // KVH_C
    gw = rep * HD_C
    cos, sin = tabs_lat
    return pl.pallas_call(
        functools.partial(_gqa_lat_kernel, scale=scale),
        grid=(DEC_BATCH, KVH_C, nq),
        in_specs=[pl.BlockSpec(memory_space=pl.ANY),
                  pl.BlockSpec((tq, gw), lambda b, g, i: (MC // tq + b * nq + i, COL_CQ // gw + g)),
                  pl.BlockSpec((DEC_SEQ, HD_C), lambda b, g, i: (lat0 + b, COL_CK // HD_C + g)),
                  pl.BlockSpec((DEC_SEQ, HD_C), lambda b, g, i: (lat0 + b, COL_CV // HD_C + g)),
                  pl.BlockSpec((None, None, PAST_LEN, HD_C), lambda b, g, i: (b, layer, 0, g)),
                  pl.BlockSpec((None, None, PAST_LEN, HD_C), lambda b, g, i: (b, layer, 0, g)),
                  pl.BlockSpec((2, HD_C), lambda b, g, i: (0, 0)),
                  pl.BlockSpec((DEC_SEQ, LANE), lambda b, g, i: (0, 0)),
                  pl.BlockSpec((DEC_SEQ, LANE), lambda b, g, i: (0, 0)),
                  pl.BlockSpec((tq, LANE), lambda b, g, i: (i, 0)),
                  pl.BlockSpec((tq, LANE), lambda b, g, i: (i, 0))],
        out_specs=pl.BlockSpec((tq, gw), lambda b, g, i: (MC // tq + b * nq + i, g)),
        out_shape=jax.ShapeDtypeStruct((M_ROWS, H_C * HD_C), bf16),
        scratch_shapes=[pltpu.VMEM((KV_SEQ, HD_C), bf16), pltpu.VMEM((KV_SEQ, HD_C), bf16)],
        input_output_aliases={0: 0},
        compiler_params=_cparams(("parallel", "parallel", "arbitrary")),
        name="gqa_attn_lat",
    )(prev, z, z, z, cache_k, cache_v, qk_norm, cos, sin, cos, sin)


def _hgrn_levels(c, base):
    w, out = base, []
    while w < c:
        out.append(w)
        w *= 2
    return out


def _hgrn_consts(c, base):
    p = np.arange(c)[:, None]
    r = np.arange(c)[None, :]
    sums = [(r <= p).astype(np.float32)]
    if base == 1:
        masks = [r == p]
    else:
        in_block = ((r // base) == (p // base)) & (r <= p)
        sums += [in_block.astype(np.float32), -in_block.astype(np.float32)]
        masks = [in_block]
    for w in _hgrn_levels(c, base):
        start = (p // (2 * w)) * (2 * w)
        ref = start + w - 1
        later = (p % (2 * w)) >= w
        sums.append(np.where(later, (r > ref) & (r <= p), (r > p) & (r <= ref)).astype(np.float32))
        masks.append(((r // (2 * w)) * (2 * w) == start) & later & ((r % (2 * w)) < w))
    sums.append((r > p).astype(np.float32))
    g = np.stack(sums)
    m = np.stack(masks).astype(np.float32)
    flip = lambda a: a[:, ::-1, ::-1]
    g = np.stack([g, flip(g)]).reshape(2, -1, c)
    m = np.stack([m, flip(m)]).reshape(2, -1, c)
    return jnp.asarray(np.concatenate([g] * HGRN_SPLIT, axis=-1), bf16), jnp.asarray(m, f32)


def _hgrn_block_sum(t_len):
    return jnp.asarray(np.arange(t_len)[None, :] // HGRN_BLOCK == np.arange(t_len // HGRN_BLOCK)[:, None], bf16)


def _hgrn_kernel(*refs, layer, t_len, latent):
    refs = refs[1:]
    if latent:
        (lb_ref, on_ref, gm_ref, mk_ref, gmf_ref, mkf_ref, bs_ref, q_ref, v_ref, ff_ref, fb_ref, g_ref, s_ref,
         out_ref, of_ref, ob_ref, qd_ref, u_ref, dec_ref, st_ref) = refs
        sout_ref = None
    else:
        (lb_ref, on_ref, gm_ref, mk_ref, gmf_ref, mkf_ref, bs_ref, q_ref, v_ref, ff_ref, fb_ref, g_ref,
         out_ref, sout_ref, of_ref, ob_ref, qd_ref, u_ref, dec_ref, st_ref) = refs
        s_ref = None
    c = HGRN_C
    hp = HGRN_HP
    n_chunks = t_len // c
    trans_b = (((1,), (1,)), ((), ()))
    heads = [slice(i * LANE, (i + 1) * LANE) for i in range(hp)]

    row = lax.broadcasted_iota(jnp.int32, (c, LANE), 0)

    if layer > 0:
        ps = [lb_ref[i] for i in range(DEPTH)]
        pmax = functools.reduce(jnp.maximum, ps)
        es = [jnp.exp(p - pmax) for p in ps]
        lb = functools.reduce(lambda a, b: a + b, es[1:layer + 1]) / functools.reduce(lambda a, b: a + b, es)
        log_lb = jnp.log(lb)
        log_1m = jnp.log1p(-lb)

    def log_forget(pre, d):
        ls = jnp.minimum(pre, 0.0) - jnp.log(1.0 + jnp.exp(-jnp.abs(pre)))
        if layer == 0:
            return ls
        a = log_lb[d:d + 1, :]
        cc = log_1m[d:d + 1, :] + ls
        return jnp.maximum(a, cc) + jnp.log(1.0 + jnp.exp(-jnp.abs(a - cc)))

    dirs = (0, 1)
    pre_refs = (ff_ref, fb_ref)
    oacc_refs = (of_ref, ob_ref)

    def stack(a):
        return jnp.concatenate([a[:, hs] for hs in heads], axis=0)

    def head_blocks(pp):
        return [pp[i * c:(i + 1) * c, i * c:(i + 1) * c] for i in range(hp)]

    def split(a):
        pieces, rem = [], a
        for _ in range(HGRN_SPLIT):
            pieces.append(rem.astype(bf16))
            rem = rem - pieces[-1].astype(f32)
        return jnp.concatenate(pieces, axis=0)

    def intra_body(ci, carry, *, fast):
        base = HGRN_BLOCK if fast else 1
        levels = _hgrn_levels(c, base)
        lv0 = 3 if fast else 1
        last = lv0 + len(levels)
        gsel, msk = (gmf_ref, mkf_ref) if fast else (gm_ref, mk_ref)
        sl = pl.ds(pl.multiple_of(ci * c, c), c)
        qx = q_ref[sl, :]
        q = qx * _sigmoid(qx)
        q_b = q.astype(bf16)
        v_b = v_ref[sl, :].astype(bf16)
        logf = [log_forget(pre_refs[d][sl, :], d) for d in dirs]
        k = [1.0 - jnp.exp(lf) for lf in logf]
        x = [jnp.exp(jnp.dot(gsel[d], split(logf[d]), preferred_element_type=f32)) for d in dirs]
        xb = lambda d, i: x[d][i * c:(i + 1) * c]
        for d in dirs:
            qd_ref[d, sl, :] = (q * xb(d, 0)).astype(bf16)
        kd = [(k[d] * xb(d, last)).astype(bf16) for d in dirs]

        if fast:
            pps = [lax.dot_general(stack((q * xb(d, 1)).astype(bf16)), stack((k[d] * xb(d, 2)).astype(bf16)),
                                   trans_b, preferred_element_type=f32) for d in dirs]
        else:
            qs = stack(q_b)
            pps = [lax.dot_general(qs, stack(k[d].astype(bf16)), trans_b, preferred_element_type=f32)
                   for d in dirs]
        attn = [[jnp.where(msk[d, 0:c, :] > 0.5, blk, 0.0) for blk in head_blocks(pps[d])] for d in dirs]
        for li, w in enumerate(levels):
            rs = []
            for d in dirs:
                later = ((row % (2 * w)) < w) if d == 1 else ((row % (2 * w)) >= w)
                later = jnp.concatenate([later] * hp, axis=1)
                rs.append(stack((jnp.where(later, q, k[d]) * xb(d, lv0 + li)).astype(bf16)))
            pps = [lax.dot_general(r, r, trans_b, preferred_element_type=f32) for r in rs]
            for d in dirs:
                mask = msk[d, (1 + li) * c:(2 + li) * c, :]
                attn[d] = [a + mask * blk for a, blk in zip(attn[d], head_blocks(pps[d]))]
        for hh, hs in enumerate(heads):
            outs = [jnp.dot(attn[d][hh].astype(bf16), v_b[:, hs], preferred_element_type=f32) for d in dirs]
            us = [lax.dot_general(v_b[:, hs], kd[d][:, hs], (((0,), (0,)), ((), ())),
                                  preferred_element_type=f32) for d in dirs]
            for d in dirs:
                oacc_refs[d][sl, hs] = outs[d]
                u_ref[d, ci, hh] = us[d]
        for d in dirs:
            decay = x[d][0:1] if d == 1 else x[d][c - 1:c]
            dec_ref[d, ci] = jnp.broadcast_to(decay, (8, hp * LANE))
        return carry

    worst = [jnp.dot(bs_ref[...], jnp.maximum(-pre_refs[d][...], 0.0).astype(bf16), preferred_element_type=f32)
             for d in dirs]
    fast_ok = jnp.max(jnp.maximum(worst[0], worst[1])) < HGRN_FAST_LIMIT

    @pl.when(fast_ok)
    def _():
        lax.fori_loop(0, n_chunks, functools.partial(intra_body, fast=True), 0)

    @pl.when(jnp.logical_not(fast_ok))
    def _():
        lax.fori_loop(0, n_chunks, functools.partial(intra_body, fast=False), 0)

    for d in range(2):
        for hh in range(hp):
            if latent:
                st_ref[d, hh] = s_ref[0, 0, d, hh].T
            else:
                st_ref[d, hh] = jnp.zeros((DV_A, DK_A), f32)

    def scan(d, cidx, oacc_ref):
        sl = pl.ds(pl.multiple_of(cidx * c, c), c)
        dec = dec_ref[d, cidx]
        for hh, hs in enumerate(heads):
            st = st_ref[d, hh]
            oacc_ref[sl, hs] = oacc_ref[sl, hs] + lax.dot_general(
                qd_ref[d, sl, hs], st.astype(bf16), trans_b, preferred_element_type=f32)
            st_ref[d, hh] = st * dec[0:1, hs] + u_ref[d, cidx, hh]

    def scan_body(ci, carry):
        scan(0, ci, of_ref)
        scan(1, n_chunks - 1 - ci, ob_ref)
        return carry

    lax.fori_loop(0, n_chunks, scan_body, 0)

    if not latent:
        for d in range(2):
            for hh in range(hp):
                sout_ref[0, d, hh] = st_ref[d, hh].T

    gx = g_ref[...]
    gate = gx * _sigmoid(gx)
    for hs in heads:
        o = of_ref[:, hs] + ob_ref[:, hs]
        out_ref[:, hs] = (_rms(o, on_ref[...]) * gate[:, hs]).astype(out_ref.dtype)


def _hgrn(prev, z, hgrn_lb, onorm, state, consts, *, layer, latent):
    t_len = DEC_SEQ if latent else SEQ
    n_samples = DEC_BATCH if latent else BATCH
    row0 = (MC // t_len) if latent else 0
    (gmat, masks), (gmat_f, masks_f) = consts
    bsum = _hgrn_block_sum(t_len)
    hp = HGRN_HP
    wide = hp * LANE

    def zspec(col):
        base = col // wide
        return pl.BlockSpec((t_len, wide), lambda b, h: (row0 + b, base + h))

    in_specs = [pl.BlockSpec(memory_space=pl.ANY),
                pl.BlockSpec((DEPTH, 2, wide), lambda b, h: (0, 0, h)),
                pl.BlockSpec((1, LANE), lambda b, h: (0, 0)),
                pl.BlockSpec(gmat.shape, lambda b, h: (0, 0, 0)),
                pl.BlockSpec(masks.shape, lambda b, h: (0, 0, 0)),
                pl.BlockSpec(gmat_f.shape, lambda b, h: (0, 0, 0)),
                pl.BlockSpec(masks_f.shape, lambda b, h: (0, 0, 0)),
                pl.BlockSpec(bsum.shape, lambda b, h: (0, 0)),
                zspec(COL_AQ), zspec(COL_AV), zspec(COL_AFF), zspec(COL_AFB), zspec(COL_AG)]
    args = [prev, hgrn_lb, onorm, gmat, masks, gmat_f, masks_f, bsum, z, z, z, z, z]
    o_spec = pl.BlockSpec((t_len, wide), lambda b, h: (row0 + b, h))
    o_shape = jax.ShapeDtypeStruct((M_ROWS, H_A * DV_A), bf16)
    if latent:
        in_specs.append(pl.BlockSpec((1, 1, 2, hp, DK_A, DV_A), lambda b, h: (b, layer, 0, h, 0, 0)))
        args.append(state)
        out_specs, out_shape = o_spec, o_shape
    else:
        out_specs = [o_spec, pl.BlockSpec((1, 2, hp, DK_A, DV_A), lambda b, h: (b, 0, h, 0, 0))]
        out_shape = [o_shape, jax.ShapeDtypeStruct((BATCH, 2, H_A, DK_A, DV_A), f32)]
    n_chunks = t_len // HGRN_C
    return pl.pallas_call(
        functools.partial(_hgrn_kernel, layer=layer, t_len=t_len, latent=latent),
        grid=(n_samples, H_A // hp),
        in_specs=in_specs,
        out_specs=out_specs,
        out_shape=out_shape,
        scratch_shapes=[pltpu.VMEM((t_len, wide), f32), pltpu.VMEM((t_len, wide), f32),
                        pltpu.VMEM((2, t_len, wide), bf16),
                        pltpu.VMEM((2, n_chunks, hp, DV_A, DK_A), f32),
                        pltpu.VMEM((2, n_chunks, 8, wide), f32),
                        pltpu.VMEM((2, hp, DV_A, DK_A), f32)],
        input_output_aliases={0: 0},
        compiler_params=_cparams(("parallel", "parallel")),
        name="hgrn_lat" if latent else "hgrn_ctx",
    )(*args)


def _permute_wuq(wuq):
    w = wuq.reshape(DEPTH, Q_LORA, H_B, NOPE_B + ROPE_B)
    nope = w[..., :NOPE_B].reshape(DEPTH, Q_LORA, H_B * NOPE_B)
    pe = jnp.pad(w[..., NOPE_B:], ((0, 0), (0, 0), (0, 0), (0, LANE - ROPE_B))).reshape(DEPTH, Q_LORA, H_B * LANE)
    return jnp.concatenate([nope, pe], axis=-1).astype(bf16)


def _pad_lane(v):
    return jnp.pad(v, ((0, 0),) * (v.ndim - 1) + ((0, LANE - v.shape[-1]),))


def kernel(x_prompt, x_sample, state_hgrn, cache_mla_ckv, cache_mla_kpe, cache_gqa_k, cache_gqa_v, c, c_ctx,
           w_mod, b_mod, norm_w, ffn_w13, ffn_w2, w_in, hgrn_lb, hgrn_onorm, mla_qa_norm, mla_kva_norm,
           mla_wuq, mla_wukv, mla_nope_norm, mla_rope_norm, gqa_qk_norm, w_branch, w_out):
    w_in_t = jnp.swapaxes(w_in, 1, 2)
    wuq_b = _permute_wuq(mla_wuq)
    wukv_b = mla_wukv.astype(bf16)
    hgrn_consts = (_hgrn_consts(HGRN_C, 1), _hgrn_consts(HGRN_C, HGRN_BLOCK))
    rope_norm_p = _pad_lane(mla_rope_norm)
    cache_kpe_p = _pad_lane(cache_mla_kpe)
    cache_k = cache_gqa_k.reshape(DEC_BATCH, DEPTH, PAST_LEN, KVH_C * HD_C)
    cache_v = cache_gqa_v.reshape(DEC_BATCH, DEPTH, PAST_LEN, KVH_C * HD_C)
    tabs_m = tuple(jnp.asarray(t) for t in _rope_tables(ROPE_B))
    tabs_g_lat = tuple(jnp.asarray(t[PAST_LEN:]) for t in _rope_tables(HD_C))
    mla_scale = (NOPE_B + ROPE_B) ** -0.5
    gqa_scale = HD_C ** -0.5

    cond = jnp.zeros((N_MOD_ROWS, D_MODEL), f32).at[0].set(c_ctx).at[1:1 + DEC_BATCH].set(c)
    x = jnp.concatenate([x_prompt.reshape(MC, D_MODEL), x_sample.reshape(ML, D_MODEL)], axis=0)

    out_a, out_b, out_c = (jnp.zeros((M_ROWS, BR_W), bf16) for _ in range(N_BRANCH))
    st_h, st_ckv, st_kpe, st_k, st_v = [], [], [], [], []
    for l in range(DEPTH):
        mod = _mm(cond, [(w_mod, (l,), 0)], n=9 * D_MODEL, tm=N_MOD_ROWS, tn=1024, out_dtype=f32, name="adaln",
                  pro="silu", epi="bias", bias=b_mod[l][None, :]).reshape(N_MOD_ROWS, 9, D_MODEL)

        def ffn(xin, i, ids):
            g = _mm(xin, [(ffn_w13, (l, i), 0), (ffn_w13, (l, i), D_FF)], n=D_FF, tm=1024, tn=512,
                    out_dtype=bf16, name="ffn_up", pro="normmod", epi="swiglu", ahead=True,
                    norm_w=norm_w[l, 2 * i][None, :], mod=mod, shift_id=ids[0], scale_id=ids[1])
            return _mm(g, [(ffn_w2, (l, i), 0)], n=D_MODEL, tm=1024, tn=256, out_dtype=f32, name="ffn_down",
                       epi="residual", res=xin, mod=mod, gate_id=ids[2], gate_scale=0.5)

        x = ffn(x, 0, (0, 1, 2))

        z = _mm(x, [(w_in_t, (l,), 0)], n=N_IN_PAD, tm=1024, tn=IN_TILE, out_dtype=f32, name="in_proj",
                pro="normmod", norm_w=norm_w[l, 1][None, :], mod=mod, shift_id=3, scale_id=4,
                w_rows=lambda j: (j * (IN_TILE // ROPE_B)
                                  - jnp.where(j >= IN_SHIFT_FROM, IN_SHIFT // ROPE_B, 0)) * ROPE_B)

        out_a, s_new = _hgrn(out_a, z, hgrn_lb, hgrn_onorm[l][None, :], None, hgrn_consts, layer=l, latent=False)
        out_a = _hgrn(out_a, z, hgrn_lb, hgrn_onorm[l][None, :], state_hgrn, hgrn_consts, layer=l, latent=True)
        st_h.append(s_new)

        q_mla, ckv, kpe = _mla_prep(z, mla_qa_norm[l][None, :], mla_kva_norm[l][None, :], rope_norm_p[l],
                                    mla_nope_norm[l, 0][None, :], wuq_b, l, tabs_m)
        nn_k = mla_nope_norm[l, 1][None, :]
        out_b = _mla_ctx(out_b, q_mla, ckv, kpe, wukv_b, l, nn_k, mla_scale)
        out_b = _mla_lat(out_b, q_mla, ckv, kpe, cache_mla_ckv, cache_kpe_p, wukv_b, l, nn_k, mla_scale)
        st_ckv.append(ckv[:MC].reshape(BATCH, SEQ, KV_LORA))
        st_kpe.append(kpe[:MC, :ROPE_B].reshape(BATCH, SEQ, ROPE_B))

        out_c, k_g = _gqa_ctx(out_c, z, gqa_qk_norm[l], gqa_scale)
        out_c = _gqa_lat(out_c, z, cache_k, cache_v, l, gqa_qk_norm[l], tabs_g_lat, gqa_scale)
        st_k.append(k_g.reshape(BATCH, SEQ, KVH_C, HD_C))
        st_v.append(z[:MC, COL_CV:COL_CV + KVH_C * HD_C].reshape(BATCH, SEQ, KVH_C, HD_C))

        merged = _merge(out_a, out_b, out_c, w_branch, l, z)
        x = _out_proj(merged, w_out, l, x, mod, gate_id=5)

        x = ffn(x, 1, (6, 7, 8))

    y_p = x[:MC].reshape(BATCH, SEQ, D_MODEL)
    y_s = x[MC:].reshape(DEC_BATCH, DEC_SEQ, D_MODEL)
    return (y_p, y_s,
            jnp.stack(st_h, axis=1), jnp.stack(st_ckv, axis=1), jnp.stack(st_kpe, axis=1),
            jnp.stack(st_k, axis=1), jnp.stack(st_v, axis=1))
```

```python
import functools

import numpy as np
import jax
import jax.numpy as jnp
from jax import lax
from jax.experimental import pallas as pl
from jax.experimental.pallas import tpu as pltpu

f32 = jnp.float32
bf16 = jnp.bfloat16

D_MODEL = 2048
BATCH = 16
SEQ = 256
DEPTH = 4
DEC_BATCH = 4
DEC_SEQ = 1024
PAST_LEN = 256
GRID_W = 64
ROPE_THETA = 10000.0
EPS = 1e-6
D_FF = 5632
N_BRANCH = 3
BR_W = 1024
H_A = 8
DK_A = 128
DV_A = 128
H_B = 8
Q_LORA = 512
KV_LORA = 256
NOPE_B = 128
ROPE_B = 64
V_B = 128
H_C = 8
KVH_C = 4
HD_C = 128

LANE = 128
MC = BATCH * SEQ
ML = DEC_BATCH * DEC_SEQ
M_ROWS = MC + ML
KV_SEQ = PAST_LEN + DEC_SEQ
N_MOD_ROWS = 16

COL_AQ, COL_AV, COL_AFF, COL_AFB, COL_AG = 0, 1024, 2048, 3072, 4096
COL_BQ = 5120
COL_BKV = 5632
COL_BPE = 5888
COL_CQ = 6144
COL_CK = 7168
COL_CV = 7680
COL_GBR = 8192
N_IN_PAD = 14336
IN_TILE = 1024
IN_SHIFT_FROM = COL_CQ // IN_TILE
IN_SHIFT = COL_CQ - (COL_BPE + ROPE_B)

ROW_TILE = 256
HGRN_C = 128
HGRN_HP = 2
HGRN_SPLIT = 2
HGRN_BLOCK = 32
HGRN_FAST_LIMIT = 60.0
VMEM_LIMIT = 56 * 1024 * 1024


def _cparams(sem):
    return pltpu.CompilerParams(dimension_semantics=sem, vmem_limit_bytes=VMEM_LIMIT)


def _sigmoid(x):
    return 1.0 / (1.0 + jnp.exp(-x))


def _rms(x, w, n=None):
    n = x.shape[-1] if n is None else n
    ms = jnp.sum(x * x, axis=-1, keepdims=True) * (1.0 / n)
    return x * lax.rsqrt(ms + EPS) * w


def _mod_row(i, tm):
    n_ctx = MC // tm
    per = DEC_SEQ // tm
    return jnp.where(i < n_ctx, 0, 1 + jnp.maximum(i - n_ctx, 0) // per)


def _mm_kernel(*refs, pro, epi, n_w, shift_id, scale_id, gate_id, gate_scale, w_rows, ahead):
    it = iter(refs)
    x_ref = next(it)
    nw_ref = next(it) if pro in ("norm", "normmod") else None
    modk_ref = next(it) if pro == "normmod" else None
    w_refs = [next(it) for _ in range(n_w)]
    res_ref = next(it) if epi == "residual" else None
    modn_ref = next(it) if epi == "residual" else None
    bias_ref = next(it) if epi == "bias" else None
    o_ref = next(it)
    h_ref = next(it) if pro is not None else None

    def prologue(dst):
        x = x_ref[...].astype(f32)
        if pro == "silu":
            y = x * _sigmoid(x)
        else:
            y = _rms(x, nw_ref[...])
            if pro == "normmod":
                y = y * (1.0 + modk_ref[0, scale_id:scale_id + 1, :]) + modk_ref[0, shift_id:shift_id + 1, :]
        h_ref[dst] = y.astype(bf16)

    def product(lhs):
        if w_rows is None:
            accs = [jnp.dot(lhs, w[...].astype(bf16), preferred_element_type=f32) for w in w_refs]
        else:
            accs = [lax.dot_general(lhs, w[(0,) * (len(w.shape) - 2)].astype(bf16), (((1,), (1,)), ((), ())),
                                    preferred_element_type=f32) for w in w_refs]
        if epi == "swiglu":
            a, u = accs
            out = a * _sigmoid(a) * u
        elif epi == "residual":
            out = res_ref[...] + (gate_scale * modn_ref[0, gate_id:gate_id + 1, :]) * accs[0]
        elif epi == "bias":
            out = accs[0] + bias_ref[...]
        else:
            out = accs[0]
        o_ref[...] = out.astype(o_ref.dtype)

    i, j = pl.program_id(0), pl.program_id(1)
    if pro is None:
        product(x_ref[...])
    elif not ahead:
        pl.when(j == 0)(lambda: prologue(0))
        product(h_ref[0])
    else:
        slot = i % 2
        pl.when((i == 0) & (j == 0))(lambda: prologue(0))
        with_next = (j == pl.num_programs(1) - 1) & (i < pl.num_programs(0) - 1)

        @pl.when(with_next)
        def _():
            product(h_ref[slot])
            prologue(1 - slot)

        @pl.when(jnp.logical_not(with_next))
        def _():
            product(h_ref[slot])


def _w_spec(w, lead, col0, k, tn):
    base = col0 // tn
    return pl.BlockSpec((None,) * len(lead) + (k, tn), lambda i, j: tuple(lead) + (0, base + j))


def _mm(x, ws, *, n, tm, tn, out_dtype, name, pro=None, epi="store", norm_w=None, mod=None,
        shift_id=0, scale_id=0, gate_id=0, gate_scale=1.0, res=None, bias=None, w_rows=None, ahead=False):
    m, k = x.shape
    grid = (m // tm, n // tn)

    def x_tile(i, j):
        if not ahead:
            return i
        return jnp.minimum(i + jnp.where(j == grid[1] - 1, 1, 0), grid[0] - 1)

    in_specs = [pl.BlockSpec((tm, k), lambda i, j: (x_tile(i, j), 0))]
    args = [x]
    if pro in ("norm", "normmod"):
        in_specs.append(pl.BlockSpec((1, k), lambda i, j: (0, 0)))
        args.append(norm_w)
    if pro == "normmod":
        in_specs.append(pl.BlockSpec((1, 9, k), lambda i, j: (_mod_row(x_tile(i, j), tm), 0, 0)))
        args.append(mod)
    for w, lead, col0 in ws:
        if w_rows is None:
            in_specs.append(_w_spec(w, lead, col0, k, tn))
        else:
            in_specs.append(pl.BlockSpec((pl.Element(1),) * len(lead) + (pl.Element(tn), pl.Element(k)),
                                         lambda i, j, lead=lead: tuple(lead) + (w_rows(j), 0)))
        args.append(w)
    if epi == "residual":
        in_specs.append(pl.BlockSpec((tm, tn), lambda i, j: (i, j)))
        args.append(res)
        in_specs.append(pl.BlockSpec((1, 9, tn), lambda i, j: (_mod_row(i, tm), 0, j)))
        args.append(mod)
    if epi == "bias":
        in_specs.append(pl.BlockSpec((1, tn), lambda i, j: (0, j)))
        args.append(bias)
    scratch = [pltpu.VMEM((2 if ahead else 1, tm, k), bf16)] if pro is not None else []
    kern = functools.partial(_mm_kernel, pro=pro, epi=epi, n_w=len(ws), shift_id=shift_id,
                             scale_id=scale_id, gate_id=gate_id, gate_scale=gate_scale, w_rows=w_rows, ahead=ahead)
    return pl.pallas_call(
        kern,
        grid=grid,
        in_specs=in_specs,
        out_specs=pl.BlockSpec((tm, tn), lambda i, j: (i, j)),
        out_shape=jax.ShapeDtypeStruct((m, n), out_dtype),
        scratch_shapes=scratch,
        compiler_params=_cparams(("arbitrary" if ahead else "parallel", "arbitrary")),
        name=name,
    )(*args)


def _adaln_kernel(c_ref, w_ref, b_ref, o_ref):
    x = c_ref[...]
    h = (x * _sigmoid(x)).astype(bf16)
    o_ref[...] = jnp.dot(h, w_ref[...].astype(bf16), preferred_element_type=f32) + b_ref[...]


def _adaln(cond, w_mod, b_mod, *, tn=1024):
    n = 9 * D_MODEL
    return pl.pallas_call(
        _adaln_kernel,
        grid=(DEPTH, n // tn),
        in_specs=[pl.BlockSpec((N_MOD_ROWS, D_MODEL), lambda l, j: (0, 0)),
                  pl.BlockSpec((None, D_MODEL, tn), lambda l, j: (l, 0, j)),
                  pl.BlockSpec((None, 1, tn), lambda l, j: (l, 0, j))],
        out_specs=pl.BlockSpec((None, N_MOD_ROWS, tn), lambda l, j: (l, 0, j)),
        out_shape=jax.ShapeDtypeStruct((DEPTH, N_MOD_ROWS, n), f32),
        compiler_params=_cparams(("parallel", "parallel")),
        name="adaln",
    )(cond, w_mod, b_mod.reshape(DEPTH, 1, n))


def _out_proj_kernel(m_ref, w_ref, res_ref, mod_ref, o_ref, wb_ref, *, gate_id):
    @pl.when(pl.program_id(0) == 0)
    def _():
        wb_ref[...] = w_ref[...].astype(bf16)

    acc = jnp.dot(m_ref[...], wb_ref[...], preferred_element_type=f32)
    o_ref[...] = res_ref[...] + mod_ref[0, gate_id:gate_id + 1, :] * acc


def _out_proj(merged, w_out, layer, res, mod, *, gate_id, tm=512):
    row = pl.BlockSpec((tm, D_MODEL), lambda i: (i, 0))
    return pl.pallas_call(
        functools.partial(_out_proj_kernel, gate_id=gate_id),
        grid=(M_ROWS // tm,),
        in_specs=[row,
                  pl.BlockSpec((None, D_MODEL, D_MODEL), lambda i: (layer, 0, 0), pipeline_mode=pl.Buffered(1)),
                  row,
                  pl.BlockSpec((1, 9, D_MODEL), lambda i: (_mod_row(i, tm), 0, 0))],
        out_specs=row,
        out_shape=jax.ShapeDtypeStruct((M_ROWS, D_MODEL), f32),
        scratch_shapes=[pltpu.VMEM((D_MODEL, D_MODEL), bf16)],
        compiler_params=_cparams(("arbitrary",)),
        name="out_proj",
    )(merged, w_out, res, mod)


def _merge_kernel(oa_ref, ob_ref, oc_ref, wb_ref, ga_ref, gb_ref, gc_ref, o_ref):
    acc = None
    for n, (o, g) in enumerate(((oa_ref, ga_ref), (ob_ref, gb_ref), (oc_ref, gc_ref))):
        br = jnp.dot(o[...], wb_ref[n].astype(bf16), preferred_element_type=f32)
        term = _sigmoid(g[...]) * br
        acc = term if acc is None else acc + term
    o_ref[...] = acc.astype(o_ref.dtype)


def _merge(out_a, out_b, out_c, wb, layer, z, *, tm=2048, tn=256):
    grid = (M_ROWS // tm, D_MODEL // tn)
    o_spec = pl.BlockSpec((tm, BR_W), lambda i, j: (i, 0))

    def g_spec(n):
        base = (COL_GBR + n * D_MODEL) // tn
        return pl.BlockSpec((tm, tn), lambda i, j: (i, base + j))

    return pl.pallas_call(
        _merge_kernel,
        grid=grid,
        in_specs=[o_spec, o_spec, o_spec,
                  pl.BlockSpec((None, N_BRANCH, BR_W, tn), lambda i, j: (layer, 0, 0, j)),
                  g_spec(0), g_spec(1), g_spec(2)],
        out_specs=pl.BlockSpec((tm, tn), lambda i, j: (i, j)),
        out_shape=jax.ShapeDtypeStruct((M_ROWS, D_MODEL), bf16),
        compiler_params=_cparams(("parallel", "arbitrary")),
        name="merge",
    )(out_a, out_b, out_c, wb, z, z, z)


def _rope(x, cos, sin, quarter):
    lane = lax.broadcasted_iota(jnp.int32, x.shape, 1)
    first = (lane % (2 * quarter)) < quarter
    partner = jnp.where(first, -pltpu.roll(x, LANE - quarter, axis=1), pltpu.roll(x, quarter, axis=1))
    return x * cos + partner * sin


def _rope_tables(width):
    quarter = width // 4
    t = np.arange(DEC_SEQ)
    inv = ROPE_THETA ** (-np.arange(quarter, dtype=np.float32) / quarter)
    ang_r = (t // GRID_W).astype(np.float32)[:, None] * inv[None, :]
    ang_c = (t % GRID_W).astype(np.float32)[:, None] * inv[None, :]
    ang = np.concatenate([ang_r, ang_r, ang_c, ang_c], axis=1).astype(np.float32)
    cos = np.ones((KV_SEQ, LANE), np.float32)
    sin = np.zeros((KV_SEQ, LANE), np.float32)
    cos[PAST_LEN:, :width] = np.cos(ang)
    sin[PAST_LEN:, :width] = np.sin(ang)
    return cos, sin


def _q_table_block(i):
    n_ctx = MC // ROW_TILE
    per = DEC_SEQ // ROW_TILE
    return jnp.where(i < n_ctx, 0, 1 + jnp.maximum(i - n_ctx, 0) % per)


LOG2_E = 1.4426950408889634


def _softmax_pv(s2, v_b):
    m = jnp.max(s2, axis=-1, keepdims=True)
    p = jnp.exp2(s2 - m)
    l = jnp.sum(p, axis=-1, keepdims=True)
    return jnp.dot(p.astype(bf16), v_b, preferred_element_type=f32) / l


_TRANS_B = (((1,), (1,)), ((), ()))


def _attend(qs, ks, vs):
    outs = []
    for i in range(0, len(qs), 2):
        ss = [lax.dot_general(q, k, _TRANS_B, preferred_element_type=f32)
              for q, k in zip(qs[i:i + 2], ks[i:i + 2])]
        outs += [_softmax_pv(s, v) for s, v in zip(ss, vs[i:i + 2])]
    return outs


def _mla_prep_kernel(bq_ref, bkv_ref, bpe_ref, qa_ref, kva_ref, rn_ref, nn_ref, wuq_ref, cm_ref, sm_ref,
                     q_ref, ckv_ref, kpe_ref):
    cm, sm = cm_ref[...], sm_ref[...]
    ckv_ref[...] = _rms(bkv_ref[...], kva_ref[...])
    bpe = bpe_ref[...]
    bpe = jnp.where(lax.broadcasted_iota(jnp.int32, bpe.shape, 1) < ROPE_B, bpe, 0.0)
    kpe_ref[...] = _rope(_rms(bpe, rn_ref[1:2, :], n=ROPE_B), cm, sm, ROPE_B // 4)
    cq = jnp.dot(_rms(bq_ref[...], qa_ref[...]).astype(bf16), wuq_ref[...], preferred_element_type=f32)
    for h in range(H_B):
        nope = _rms(cq[:, h * LANE:(h + 1) * LANE], nn_ref[...])
        pe = _rms(cq[:, (H_B + h) * LANE:(H_B + h + 1) * LANE], rn_ref[0:1, :], n=ROPE_B)
        q_ref[:, 2 * h * LANE:(2 * h + 1) * LANE] = nope.astype(bf16)
        q_ref[:, (2 * h + 1) * LANE:(2 * h + 2) * LANE] = _rope(pe, cm, sm, ROPE_B // 4).astype(bf16)


def _mla_prep(z, qa_norm, kva_norm, rope_norm, nope_norm_q, wuq, layer, tabs_m):
    tr = ROW_TILE
    width = 2 * H_B * LANE

    def zspec(col, w):
        return pl.BlockSpec((tr, w), lambda i: (i, col // w))

    def wspec(shape):
        return pl.BlockSpec(shape, lambda i: (0, 0))

    tspec = pl.BlockSpec((tr, LANE), lambda i: (_q_table_block(i), 0))
    return pl.pallas_call(
        _mla_prep_kernel,
        grid=(M_ROWS // tr,),
        in_specs=[zspec(COL_BQ, Q_LORA), zspec(COL_BKV, KV_LORA), zspec(COL_BPE, LANE),
                  wspec((1, Q_LORA)), wspec((1, KV_LORA)), wspec((2, LANE)), wspec((1, LANE)),
                  pl.BlockSpec((None, Q_LORA, width), lambda i: (layer, 0, 0)), tspec, tspec],
        out_specs=[pl.BlockSpec((tr, width), lambda i: (i, 0)),
                   pl.BlockSpec((tr, KV_LORA), lambda i: (i, 0)),
                   pl.BlockSpec((tr, LANE), lambda i: (i, 0))],
        out_shape=[jax.ShapeDtypeStruct((M_ROWS, width), bf16),
                   jax.ShapeDtypeStruct((M_ROWS, KV_LORA), f32),
                   jax.ShapeDtypeStruct((M_ROWS, LANE), f32)],
        compiler_params=_cparams(("parallel",)),
        name="mla_prep",
    )(z, z, z, qa_norm, kva_norm, rope_norm, nope_norm_q, wuq, *tabs_m)


def _mla_kv(ckv, kpe, w, nn, kscale):
    kv = jnp.dot(ckv.astype(bf16), w, preferred_element_type=f32)
    n_h = w.shape[1] // (2 * LANE)
    kpe_b = (kpe * kscale).astype(bf16)
    ks, vs = [], []
    for h in range(n_h):
        nope = _rms(kv[:, 2 * h * LANE:(2 * h + 1) * LANE], nn) * kscale
        ks.append(jnp.concatenate([nope.astype(bf16), kpe_b], axis=1))
        vs.append(kv[:, (2 * h + 1) * LANE:(2 * h + 2) * LANE].astype(bf16))
    return ks, vs


def _mla_ctx_kernel(prev_ref, q_ref, ckv_ref, kpe_ref, w_ref, nn_ref, o_ref, *, scale):
    del prev_ref
    ks, vs = _mla_kv(ckv_ref[...], kpe_ref[...], w_ref[...], nn_ref[...], scale * LOG2_E)
    for h in range(H_B):
        s = lax.dot_general(q_ref[:, 2 * h * LANE:(2 * h + 2) * LANE], ks[h], _TRANS_B,
                            preferred_element_type=f32)
        o_ref[:, h * V_B:(h + 1) * V_B] = _softmax_pv(s, vs[h]).astype(o_ref.dtype)


def _mla_ctx(prev, q_mla, ckv, kpe, wukv, layer, nope_norm_k, scale):
    width = 2 * H_B * LANE
    return pl.pallas_call(
        functools.partial(_mla_ctx_kernel, scale=scale),
        grid=(BATCH,),
        in_specs=[pl.BlockSpec(memory_space=pl.ANY),
                  pl.BlockSpec((SEQ, width), lambda b: (b, 0)),
                  pl.BlockSpec((SEQ, KV_LORA), lambda b: (b, 0)),
                  pl.BlockSpec((SEQ, LANE), lambda b: (b, 0)),
                  pl.BlockSpec((None, KV_LORA, width), lambda b: (layer, 0, 0)),
                  pl.BlockSpec((1, LANE), lambda b: (0, 0))],
        out_specs=pl.BlockSpec((SEQ, H_B * V_B), lambda b: (b, 0)),
        out_shape=jax.ShapeDtypeStruct((M_ROWS, H_B * V_B), bf16),
        input_output_aliases={0: 0},
        compiler_params=_cparams(("parallel",)),
        name="mla_attn_ctx",
    )(prev, q_mla, ckv, kpe, wukv, nope_norm_k)


def _mla_lat_kernel(prev_ref, q_ref, ckvc_ref, kpec_ref, ckvn_ref, kpen_ref, w_ref, nn_ref, o_ref,
                    k_scr, v_scr, *, scale):
    del prev_ref

    @pl.when(pl.program_id(2) == 0)
    def _():
        ks, vs = _mla_kv(ckvc_ref[...], kpec_ref[...], w_ref[...], nn_ref[...], scale * LOG2_E)
        k_scr[0:PAST_LEN, :] = ks[0]
        v_scr[0:PAST_LEN, :] = vs[0]
        ks, vs = _mla_kv(ckvn_ref[...], kpen_ref[...], w_ref[...], nn_ref[...], scale * LOG2_E)
        k_scr[PAST_LEN:KV_SEQ, :] = ks[0]
        v_scr[PAST_LEN:KV_SEQ, :] = vs[0]

    k_b, v_b = k_scr[...], v_scr[...]
    half = q_ref.shape[0] // 2
    qs = [q_ref[0:half, :], q_ref[half:2 * half, :]]
    for i, o in enumerate(_attend(qs, [k_b, k_b], [v_b, v_b])):
        o_ref[i * half:(i + 1) * half, :] = o.astype(o_ref.dtype)


def _mla_lat(prev, q_mla, ckv, kpe, cache_ckv, cache_kpe, wukv, layer, nope_norm_k, scale, *, tq=1024):
    nq = DEC_SEQ // tq
    lat0 = MC // DEC_SEQ
    return pl.pallas_call(
        functools.partial(_mla_lat_kernel, scale=scale),
        grid=(DEC_BATCH, H_B, nq),
        in_specs=[pl.BlockSpec(memory_space=pl.ANY),
                  pl.BlockSpec((tq, 2 * LANE), lambda b, h, i: (MC // tq + b * nq + i, h)),
                  pl.BlockSpec((None, None, PAST_LEN, KV_LORA), lambda b, h, i: (b, layer, 0, 0)),
                  pl.BlockSpec((None, None, PAST_LEN, LANE), lambda b, h, i: (b, layer, 0, 0)),
                  pl.BlockSpec((DEC_SEQ, KV_LORA), lambda b, h, i: (lat0 + b, 0)),
                  pl.BlockSpec((DEC_SEQ, LANE), lambda b, h, i: (lat0 + b, 0)),
                  pl.BlockSpec((None, KV_LORA, 2 * LANE), lambda b, h, i: (layer, 0, h)),
                  pl.BlockSpec((1, LANE), lambda b, h, i: (0, 0))],
        out_specs=pl.BlockSpec((tq, V_B), lambda b, h, i: (MC // tq + b * nq + i, h)),
        out_shape=jax.ShapeDtypeStruct((M_ROWS, H_B * V_B), bf16),
        scratch_shapes=[pltpu.VMEM((KV_SEQ, 2 * LANE), bf16), pltpu.VMEM((KV_SEQ, V_B), bf16)],
        input_output_aliases={0: 0},
        compiler_params=_cparams(("parallel", "parallel", "arbitrary")),
        name="mla_attn_lat",
    )(prev, q_mla, cache_ckv, cache_kpe, ckv, kpe, wukv, nope_norm_k)


def _gqa_ctx_kernel(prev_ref, cq_ref, ck_ref, cv_ref, qkn_ref, o_ref, kg_ref, *, scale):
    del prev_ref
    rep = H_C // KVH_C
    for g in range(KVH_C):
        gs = slice(g * HD_C, (g + 1) * HD_C)
        k = _rms(ck_ref[:, gs], qkn_ref[1:2, :])
        kg_ref[:, gs] = k
        k_b = (k * (scale * LOG2_E)).astype(bf16)
        v_b = cv_ref[:, gs].astype(bf16)
        for h in range(g * rep, (g + 1) * rep):
            hs = slice(h * HD_C, (h + 1) * HD_C)
            q = _rms(cq_ref[:, hs], qkn_ref[0:1, :]).astype(bf16)
            s = lax.dot_general(q, k_b, _TRANS_B, preferred_element_type=f32)
            o_ref[:, hs] = _softmax_pv(s, v_b).astype(o_ref.dtype)


def _gqa_ctx(prev, z, qk_norm, scale):
    def zspec(col, w):
        return pl.BlockSpec((SEQ, w), lambda b: (b, col // w))

    return pl.pallas_call(
        functools.partial(_gqa_ctx_kernel, scale=scale),
        grid=(BATCH,),
        in_specs=[pl.BlockSpec(memory_space=pl.ANY),
                  zspec(COL_CQ, H_C * HD_C), zspec(COL_CK, KVH_C * HD_C), zspec(COL_CV, KVH_C * HD_C),
                  pl.BlockSpec((2, HD_C), lambda b: (0, 0))],
        out_specs=[pl.BlockSpec((SEQ, H_C * HD_C), lambda b: (b, 0)),
                   pl.BlockSpec((SEQ, KVH_C * HD_C), lambda b: (b, 0))],
        out_shape=[jax.ShapeDtypeStruct((M_ROWS, H_C * HD_C), bf16),
                   jax.ShapeDtypeStruct((MC, KVH_C * HD_C), f32)],
        input_output_aliases={0: 0},
        compiler_params=_cparams(("parallel",)),
        name="gqa_attn_ctx",
    )(prev, z, z, z, qk_norm)


def _gqa_lat_kernel(prev_ref, cq_ref, ckn_ref, cvn_ref, kc_ref, vc_ref, qkn_ref, ck_ref, sk_ref, cqt_ref, sqt_ref,
                    o_ref, k_scr, v_scr, *, scale):
    del prev_ref
    rep = H_C // KVH_C

    @pl.when(pl.program_id(2) == 0)
    def _():
        kscale = scale * LOG2_E
        k_scr[0:PAST_LEN, :] = (kc_ref[...] * kscale).astype(bf16)
        v_scr[0:PAST_LEN, :] = vc_ref[...].astype(bf16)
        k = _rope(_rms(ckn_ref[...], qkn_ref[1:2, :]), ck_ref[...], sk_ref[...], HD_C // 4)
        k_scr[PAST_LEN:KV_SEQ, :] = (k * kscale).astype(bf16)
        v_scr[PAST_LEN:KV_SEQ, :] = cvn_ref[...].astype(bf16)

    cq, sq = cqt_ref[...], sqt_ref[...]
    k_b, v_b = k_scr[...], v_scr[...]
    half = cq_ref.shape[0] // 2
    for r in range(rep):
        rs = slice(r * HD_C, (r + 1) * HD_C)
        q = _rope(_rms(cq_ref[:, rs], qkn_ref[0:1, :]), cq, sq, HD_C // 4).astype(bf16)
        for i, o in enumerate(_attend([q[0:half], q[half:2 * half]], [k_b, k_b], [v_b, v_b])):
            o_ref[i * half:(i + 1) * half, rs] = o.astype(o_ref.dtype)


def _gqa_lat(prev, z, cache_k, cache_v, layer, qk_norm, tabs_lat, scale, *, tq=512):
    nq = DEC_SEQ // tq
    lat0 = MC // DEC_SEQ
    rep = H_C // KVH_C
    gw = rep * HD_C
    cos, sin = tabs_lat
    return pl.pallas_call(
        functools.partial(_gqa_lat_kernel, scale=scale),
        grid=(DEC_BATCH, KVH_C, nq),
        in_specs=[pl.BlockSpec(memory_space=pl.ANY),
                  pl.BlockSpec((tq, gw), lambda b, g, i: (MC // tq + b * nq + i, COL_CQ // gw + g)),
                  pl.BlockSpec((DEC_SEQ, HD_C), lambda b, g, i: (lat0 + b, COL_CK // HD_C + g)),
                  pl.BlockSpec((DEC_SEQ, HD_C), lambda b, g, i: (lat0 + b, COL_CV // HD_C + g)),
                  pl.BlockSpec((None, None, PAST_LEN, HD_C), lambda b, g, i: (b, layer, 0, g)),
                  pl.BlockSpec((None, None, PAST_LEN, HD_C), lambda b, g, i: (b, layer, 0, g)),
                  pl.BlockSpec((2, HD_C), lambda b, g, i: (0, 0)),
                  pl.BlockSpec((DEC_SEQ, LANE), lambda b, g, i: (0, 0)),
                  pl.BlockSpec((DEC_SEQ, LANE), lambda b, g, i: (0, 0)),
                  pl.BlockSpec((tq, LANE), lambda b, g, i: (i, 0)),
                  pl.BlockSpec((tq, LANE), lambda b, g, i: (i, 0))],
        out_specs=pl.BlockSpec((tq, gw), lambda b, g, i: (MC // tq + b * nq + i, g)),
        out_shape=jax.ShapeDtypeStruct((M_ROWS, H_C * HD_C), bf16),
        scratch_shapes=[pltpu.VMEM((KV_SEQ, HD_C), bf16), pltpu.VMEM((KV_SEQ, HD_C), bf16)],
        input_output_aliases={0: 0},
        compiler_params=_cparams(("parallel", "parallel", "arbitrary")),
        name="gqa_attn_lat",
    )(prev, z, z, z, cache_k, cache_v, qk_norm, cos, sin, cos, sin)


def _hgrn_levels(c, base):
    w, out = base, []
    while w < c:
        out.append(w)
        w *= 2
    return out


def _hgrn_consts(c, base):
    p = np.arange(c)[:, None]
    r = np.arange(c)[None, :]
    sums = [(r <= p).astype(np.float32)]
    if base == 1:
        masks = [r == p]
    else:
        in_block = ((r // base) == (p // base)) & (r <= p)
        sums += [in_block.astype(np.float32), -in_block.astype(np.float32)]
        masks = [in_block]
    for w in _hgrn_levels(c, base):
        start = (p // (2 * w)) * (2 * w)
        ref = start + w - 1
        later = (p % (2 * w)) >= w
        sums.append(np.where(later, (r > ref) & (r <= p), (r > p) & (r <= ref)).astype(np.float32))
        masks.append(((r // (2 * w)) * (2 * w) == start) & later & ((r % (2 * w)) < w))
    sums.append((r > p).astype(np.float32))
    g = np.stack(sums)
    m = np.stack(masks).astype(np.float32)
    flip = lambda a: a[:, ::-1, ::-1]
    g = np.stack([g, flip(g)]).reshape(2, -1, c)
    m = np.stack([m, flip(m)]).reshape(2, -1, c)
    return jnp.asarray(np.concatenate([g] * HGRN_SPLIT, axis=-1), bf16), jnp.asarray(m, f32)


def _hgrn_block_sum(t_len):
    return jnp.asarray(np.arange(t_len)[None, :] // HGRN_BLOCK == np.arange(t_len // HGRN_BLOCK)[:, None], bf16)


def _hgrn_kernel(*refs, layer, t_len, latent):
    refs = refs[1:]
    if latent:
        (lb_ref, on_ref, gm_ref, mk_ref, gmf_ref, mkf_ref, bs_ref, q_ref, v_ref, ff_ref, fb_ref, g_ref, s_ref,
         out_ref, of_ref, ob_ref, qd_ref, u_ref, dec_ref, st_ref) = refs
        sout_ref = None
    else:
        (lb_ref, on_ref, gm_ref, mk_ref, gmf_ref, mkf_ref, bs_ref, q_ref, v_ref, ff_ref, fb_ref, g_ref,
         out_ref, sout_ref, of_ref, ob_ref, qd_ref, u_ref, dec_ref, st_ref) = refs
        s_ref = None
    c = HGRN_C
    hp = HGRN_HP
    n_chunks = t_len // c
    trans_b = (((1,), (1,)), ((), ()))
    heads = [slice(i * LANE, (i + 1) * LANE) for i in range(hp)]

    row = lax.broadcasted_iota(jnp.int32, (c, LANE), 0)

    if layer > 0:
        ps = [lb_ref[i] for i in range(DEPTH)]
        pmax = functools.reduce(jnp.maximum, ps)
        es = [jnp.exp(p - pmax) for p in ps]
        lb = functools.reduce(lambda a, b: a + b, es[1:layer + 1]) / functools.reduce(lambda a, b: a + b, es)
        log_lb = jnp.log(lb)
        log_1m = jnp.log1p(-lb)

    def log_forget(pre, d):
        ls = jnp.minimum(pre, 0.0) - jnp.log(1.0 + jnp.exp(-jnp.abs(pre)))
        if layer == 0:
            return ls
        a = log_lb[d:d + 1, :]
        cc = log_1m[d:d + 1, :] + ls
        return jnp.maximum(a, cc) + jnp.log(1.0 + jnp.exp(-jnp.abs(a - cc)))

    dirs = (0, 1)
    pre_refs = (ff_ref, fb_ref)
    oacc_refs = (of_ref, ob_ref)

    def stack(a):
        return jnp.concatenate([a[:, hs] for hs in heads], axis=0)

    def head_blocks(pp):
        return [pp[i * c:(i + 1) * c, i * c:(i + 1) * c] for i in range(hp)]

    def split(a):
        pieces, rem = [], a
        for _ in range(HGRN_SPLIT):
            pieces.append(rem.astype(bf16))
            rem = rem - pieces[-1].astype(f32)
        return jnp.concatenate(pieces, axis=0)

    def intra_body(ci, carry, *, fast):
        base = HGRN_BLOCK if fast else 1
        levels = _hgrn_levels(c, base)
        lv0 = 3 if fast else 1
        last = lv0 + len(levels)
        gsel, msk = (gmf_ref, mkf_ref) if fast else (gm_ref, mk_ref)
        sl = pl.ds(pl.multiple_of(ci * c, c), c)
        qx = q_ref[sl, :]
        q = qx * _sigmoid(qx)
        q_b = q.astype(bf16)
        v_b = v_ref[sl, :].astype(bf16)
        logf = [log_forget(pre_refs[d][sl, :], d) for d in dirs]
        k = [1.0 - jnp.exp(lf) for lf in logf]
        x = [jnp.exp(jnp.dot(gsel[d], split(logf[d]), preferred_element_type=f32)) for d in dirs]
        xb = lambda d, i: x[d][i * c:(i + 1) * c]
        for d in dirs:
            qd_ref[d, sl, :] = (q * xb(d, 0)).astype(bf16)
        kd = [(k[d] * xb(d, last)).astype(bf16) for d in dirs]

        if fast:
            pps = [lax.dot_general(stack((q * xb(d, 1)).astype(bf16)), stack((k[d] * xb(d, 2)).astype(bf16)),
                                   trans_b, preferred_element_type=f32) for d in dirs]
        else:
            qs = stack(q_b)
            pps = [lax.dot_general(qs, stack(k[d].astype(bf16)), trans_b, preferred_element_type=f32)
                   for d in dirs]
        attn = [[jnp.where(msk[d, 0:c, :] > 0.5, blk, 0.0) for blk in head_blocks(pps[d])] for d in dirs]
        for li, w in enumerate(levels):
            rs = []
            for d in dirs:
                later = ((row % (2 * w)) < w) if d == 1 else ((row % (2 * w)) >= w)
                later = jnp.concatenate([later] * hp, axis=1)
                rs.append(stack((jnp.where(later, q, k[d]) * xb(d, lv0 + li)).astype(bf16)))
            pps = [lax.dot_general(r, r, trans_b, preferred_element_type=f32) for r in rs]
            for d in dirs:
                mask = msk[d, (1 + li) * c:(2 + li) * c, :]
                attn[d] = [a + mask * blk for a, blk in zip(attn[d], head_blocks(pps[d]))]
        for hh, hs in enumerate(heads):
            outs = [jnp.dot(attn[d][hh].astype(bf16), v_b[:, hs], preferred_element_type=f32) for d in dirs]
            us = [lax.dot_general(v_b[:, hs], kd[d][:, hs], (((0,), (0,)), ((), ())),
                                  preferred_element_type=f32) for d in dirs]
            for d in dirs:
                oacc_refs[d][sl, hs] = outs[d]
                u_ref[d, ci, hh] = us[d]
        for d in dirs:
            decay = x[d][0:1] if d == 1 else x[d][c - 1:c]
            dec_ref[d, ci] = jnp.broadcast_to(decay, (8, hp * LANE))
        return carry

    worst = [jnp.dot(bs_ref[...], jnp.maximum(-pre_refs[d][...], 0.0).astype(bf16), preferred_element_type=f32)
             for d in dirs]
    fast_ok = jnp.max(jnp.maximum(worst[0], worst[1])) < HGRN_FAST_LIMIT

    @pl.when(fast_ok)
    def _():
        lax.fori_loop(0, n_chunks, functools.partial(intra_body, fast=True), 0)

    @pl.when(jnp.logical_not(fast_ok))
    def _():
        lax.fori_loop(0, n_chunks, functools.partial(intra_body, fast=False), 0)

    for d in range(2):
        for hh in range(hp):
            if latent:
                st_ref[d, hh] = s_ref[0, 0, d, hh].T
            else:
                st_ref[d, hh] = jnp.zeros((DV_A, DK_A), f32)

    def scan(d, cidx, oacc_ref):
        sl = pl.ds(pl.multiple_of(cidx * c, c), c)
        dec = dec_ref[d, cidx]
        for hh, hs in enumerate(heads):
            st = st_ref[d, hh]
            oacc_ref[sl, hs] = oacc_ref[sl, hs] + lax.dot_general(
                qd_ref[d, sl, hs], st.astype(bf16), trans_b, preferred_element_type=f32)
            st_ref[d, hh] = st * dec[0:1, hs] + u_ref[d, cidx, hh]

    def scan_body(ci, carry):
        scan(0, ci, of_ref)
        scan(1, n_chunks - 1 - ci, ob_ref)
        return carry

    lax.fori_loop(0, n_chunks, scan_body, 0)

    if not latent:
        for d in range(2):
            for hh in range(hp):
                sout_ref[0, d, hh] = st_ref[d, hh].T

    gx = g_ref[...]
    gate = gx * _sigmoid(gx)
    for hs in heads:
        o = of_ref[:, hs] + ob_ref[:, hs]
        out_ref[:, hs] = (_rms(o, on_ref[...]) * gate[:, hs]).astype(out_ref.dtype)


def _hgrn(prev, z, hgrn_lb, onorm, state, consts, *, layer, latent):
    t_len = DEC_SEQ if latent else SEQ
    n_samples = DEC_BATCH if latent else BATCH
    row0 = (MC // t_len) if latent else 0
    (gmat, masks), (gmat_f, masks_f) = consts
    bsum = _hgrn_block_sum(t_len)
    hp = HGRN_HP
    wide = hp * LANE

    def zspec(col):
        base = col // wide
        return pl.BlockSpec((t_len, wide), lambda b, h: (row0 + b, base + h))

    in_specs = [pl.BlockSpec(memory_space=pl.ANY),
                pl.BlockSpec((DEPTH, 2, wide), lambda b, h: (0, 0, h)),
                pl.BlockSpec((1, LANE), lambda b, h: (0, 0)),
                pl.BlockSpec(gmat.shape, lambda b, h: (0, 0, 0)),
                pl.BlockSpec(masks.shape, lambda b, h: (0, 0, 0)),
                pl.BlockSpec(gmat_f.shape, lambda b, h: (0, 0, 0)),
                pl.BlockSpec(masks_f.shape, lambda b, h: (0, 0, 0)),
                pl.BlockSpec(bsum.shape, lambda b, h: (0, 0)),
                zspec(COL_AQ), zspec(COL_AV), zspec(COL_AFF), zspec(COL_AFB), zspec(COL_AG)]
    args = [prev, hgrn_lb, onorm, gmat, masks, gmat_f, masks_f, bsum, z, z, z, z, z]
    o_spec = pl.BlockSpec((t_len, wide), lambda b, h: (row0 + b, h))
    o_shape = jax.ShapeDtypeStruct((M_ROWS, H_A * DV_A), bf16)
    if latent:
        in_specs.append(pl.BlockSpec((1, 1, 2, hp, DK_A, DV_A), lambda b, h: (b, layer, 0, h, 0, 0)))
        args.append(state)
        out_specs, out_shape = o_spec, o_shape
    else:
        out_specs = [o_spec, pl.BlockSpec((1, 2, hp, DK_A, DV_A), lambda b, h: (b, 0, h, 0, 0))]
        out_shape = [o_shape, jax.ShapeDtypeStruct((BATCH, 2, H_A, DK_A, DV_A), f32)]
    n_chunks = t_len // HGRN_C
    return pl.pallas_call(
        functools.partial(_hgrn_kernel, layer=layer, t_len=t_len, latent=latent),
        grid=(n_samples, H_A // hp),
        in_specs=in_specs,
        out_specs=out_specs,
        out_shape=out_shape,
        scratch_shapes=[pltpu.VMEM((t_len, wide), f32), pltpu.VMEM((t_len, wide), f32),
                        pltpu.VMEM((2, t_len, wide), bf16),
                        pltpu.VMEM((2, n_chunks, hp, DV_A, DK_A), f32),
                        pltpu.VMEM((2, n_chunks, 8, wide), f32),
                        pltpu.VMEM((2, hp, DV_A, DK_A), f32)],
        input_output_aliases={0: 0},
        compiler_params=_cparams(("parallel", "parallel")),
        name="hgrn_lat" if latent else "hgrn_ctx",
    )(*args)


def _permute_wuq(wuq):
    w = wuq.reshape(DEPTH, Q_LORA, H_B, NOPE_B + ROPE_B)
    nope = w[..., :NOPE_B].reshape(DEPTH, Q_LORA, H_B * NOPE_B)
    pe = jnp.pad(w[..., NOPE_B:], ((0, 0), (0, 0), (0, 0), (0, LANE - ROPE_B))).reshape(DEPTH, Q_LORA, H_B * LANE)
    return jnp.concatenate([nope, pe], axis=-1).astype(bf16)


def _pad_lane(v):
    return jnp.pad(v, ((0, 0),) * (v.ndim - 1) + ((0, LANE - v.shape[-1]),))


def kernel(x_prompt, x_sample, state_hgrn, cache_mla_ckv, cache_mla_kpe, cache_gqa_k, cache_gqa_v, c, c_ctx,
           w_mod, b_mod, norm_w, ffn_w13, ffn_w2, w_in, hgrn_lb, hgrn_onorm, mla_qa_norm, mla_kva_norm,
           mla_wuq, mla_wukv, mla_nope_norm, mla_rope_norm, gqa_qk_norm, w_branch, w_out):
    w_in_t = jnp.swapaxes(w_in, 1, 2)
    wuq_b = _permute_wuq(mla_wuq)
    wukv_b = mla_wukv.astype(bf16)
    hgrn_consts = (_hgrn_consts(HGRN_C, 1), _hgrn_consts(HGRN_C, HGRN_BLOCK))
    rope_norm_p = _pad_lane(mla_rope_norm)
    cache_kpe_p = _pad_lane(cache_mla_kpe)
    cache_k = cache_gqa_k.reshape(DEC_BATCH, DEPTH, PAST_LEN, KVH_C * HD_C)
    cache_v = cache_gqa_v.reshape(DEC_BATCH, DEPTH, PAST_LEN, KVH_C * HD_C)
    tabs_m = tuple(jnp.asarray(t) for t in _rope_tables(ROPE_B))
    tabs_g_lat = tuple(jnp.asarray(t[PAST_LEN:]) for t in _rope_tables(HD_C))
    mla_scale = (NOPE_B + ROPE_B) ** -0.5
    gqa_scale = HD_C ** -0.5

    cond = jnp.zeros((N_MOD_ROWS, D_MODEL), f32).at[0].set(c_ctx).at[1:1 + DEC_BATCH].set(c)
    x = jnp.concatenate([x_prompt.reshape(MC, D_MODEL), x_sample.reshape(ML, D_MODEL)], axis=0)

    out_a, out_b, out_c = (jnp.zeros((M_ROWS, BR_W), bf16) for _ in range(N_BRANCH))
    mod_all = _adaln(cond, w_mod, b_mod).reshape(DEPTH, N_MOD_ROWS, 9, D_MODEL)
    st_h, st_ckv, st_kpe, st_k, st_v = [], [], [], [], []
    for l in range(DEPTH):
        mod = mod_all[l]

        def ffn(xin, i, ids):
            g = _mm(xin, [(ffn_w13, (l, i), 0), (ffn_w13, (l, i), D_FF)], n=D_FF, tm=1024, tn=512,
                    out_dtype=bf16, name="ffn_up", pro="normmod", epi="swiglu", ahead=True,
                    norm_w=norm_w[l, 2 * i][None, :], mod=mod, shift_id=ids[0], scale_id=ids[1])
            return _mm(g, [(ffn_w2, (l, i), 0)], n=D_MODEL, tm=1024, tn=256, out_dtype=f32, name="ffn_down",
                       epi="residual", res=xin, mod=mod, gate_id=ids[2], gate_scale=0.5)

        x = ffn(x, 0, (0, 1, 2))

        z = _mm(x, [(w_in_t, (l,), 0)], n=N_IN_PAD, tm=1024, tn=IN_TILE, out_dtype=f32, name="in_proj",
                pro="normmod", norm_w=norm_w[l, 1][None, :], mod=mod, shift_id=3, scale_id=4,
                w_rows=lambda j: (j * (IN_TILE // ROPE_B)
                                  - jnp.where(j >= IN_SHIFT_FROM, IN_SHIFT // ROPE_B, 0)) * ROPE_B)

        out_a, s_new = _hgrn(out_a, z, hgrn_lb, hgrn_onorm[l][None, :], None, hgrn_consts, layer=l, latent=False)
        out_a = _hgrn(out_a, z, hgrn_lb, hgrn_onorm[l][None, :], state_hgrn, hgrn_consts, layer=l, latent=True)
        st_h.append(s_new)

        q_mla, ckv, kpe = _mla_prep(z, mla_qa_norm[l][None, :], mla_kva_norm[l][None, :], rope_norm_p[l],
                                    mla_nope_norm[l, 0][None, :], wuq_b, l, tabs_m)
        nn_k = mla_nope_norm[l, 1][None, :]
        out_b = _mla_ctx(out_b, q_mla, ckv, kpe, wukv_b, l, nn_k, mla_scale)
        out_b = _mla_lat(out_b, q_mla, ckv, kpe, cache_mla_ckv, cache_kpe_p, wukv_b, l, nn_k, mla_scale)
        st_ckv.append(ckv[:MC].reshape(BATCH, SEQ, KV_LORA))
        st_kpe.append(kpe[:MC, :ROPE_B].reshape(BATCH, SEQ, ROPE_B))

        out_c, k_g = _gqa_ctx(out_c, z, gqa_qk_norm[l], gqa_scale)
        out_c = _gqa_lat(out_c, z, cache_k, cache_v, l, gqa_qk_norm[l], tabs_g_lat, gqa_scale)
        st_k.append(k_g.reshape(BATCH, SEQ, KVH_C, HD_C))
        st_v.append(z[:MC, COL_CV:COL_CV + KVH_C * HD_C].reshape(BATCH, SEQ, KVH_C, HD_C))

        merged = _merge(out_a, out_b, out_c, w_branch, l, z)
        x = _out_proj(merged, w_out, l, x, mod, gate_id=5)

        x = ffn(x, 1, (6, 7, 8))

    y_p = x[:MC].reshape(BATCH, SEQ, D_MODEL)
    y_s = x[MC:].reshape(DEC_BATCH, DEC_SEQ, D_MODEL)
    return (y_p, y_s,
            jnp.stack(st_h, axis=1), jnp.stack(st_ckv, axis=1), jnp.stack(st_kpe, axis=1),
            jnp.stack(st_k, axis=1), jnp.stack(st_v, axis=1))
```
